```python
import math
import jax, jax.numpy as jnp
from jax import lax
import numpy as np

D_MODEL = 1024
BATCH = 2
SEQ = 16384
DEPTH = 2

MIX_WIDTH = D_MODEL
S5_WIDTH = MIX_WIDTH // 2
S5_GROUP_DIM = 16
S5_GROUPS = S5_WIDTH // S5_GROUP_DIM
S5_STATE = 64
S5_DT_MIN = 1e-3
S5_DT_MAX = 1e-1
HG_WIDTH = MIX_WIDTH - S5_WIDTH
HG_HEADS = 4
HG_HEAD_DIM = HG_WIDTH // HG_HEADS
GLA_HEADS = 4
GLA_KEY_WIDTH = D_MODEL // 2
GLA_VALUE_WIDTH = D_MODEL
GLA_DK = GLA_KEY_WIDTH // GLA_HEADS
GLA_DV = GLA_VALUE_WIDTH // GLA_HEADS
GLA_GATE_RANK = 16
GLA_GATE_NORM = 16.0
CHUNK = 64
FFN_HIDDEN = -(-8 * D_MODEL // (3 * 256)) * 256
N_EVEN = (DEPTH + 1) // 2
N_ODD = DEPTH // 2
EVEN_IN = S5_WIDTH + 4 * HG_WIDTH
ODD_IN = 2 * GLA_KEY_WIDTH + 2 * GLA_VALUE_WIDTH
N_MOD = 6
EPS = 1e-6

kernel_name = "hybrid_s5_hgrn2_gla_adaln_block"


def rms_norm(x, g):
    xf = x.astype(jnp.float32)
    y = xf * lax.rsqrt(jnp.mean(xf * xf, axis=-1, keepdims=True) + EPS)
    return (y * g.astype(jnp.float32)).astype(x.dtype)


def modulate(h, shift, scale):
    return h * (1 + scale[:, None, :]) + shift[:, None, :]


def _linear_recurrence_combine(e_i, e_j):
    a_i, b_i = e_i
    a_j, b_j = e_j
    return a_j * a_i, a_j * b_i + b_j


def chunk_gated_linear(q, k, v, log_g):
    bsz, seq, heads, dk = q.shape
    dv = v.shape[-1]
    n_chunks = seq // CHUNK

    def to_chunks(t):
        t = t.astype(jnp.float32).reshape(bsz, n_chunks, CHUNK, heads, t.shape[-1])
        return t.transpose(1, 0, 3, 2, 4)

    causal = jnp.tril(jnp.ones((CHUNK, CHUNK), dtype=bool))

    def step(state, inp):
        qc, kc, vc, gc = inp
        cum = jnp.cumsum(gc, axis=2)
        o_inter = jnp.einsum('bhik,bhkv->bhiv', qc * jnp.exp(cum), state)
        diff = cum[:, :, :, None, :] - cum[:, :, None, :, :]
        decay = jnp.exp(jnp.where(causal[:, :, None], diff, -jnp.inf))
        scores = jnp.einsum('bhik,bhjk,bhijk->bhij', qc, kc, decay)
        o_intra = jnp.einsum('bhij,bhjv->bhiv', scores, vc)
        last = cum[:, :, -1:, :]
        new_state = (jnp.exp(last[:, :, 0, :])[..., None] * state
                     + jnp.einsum('bhjk,bhjv->bhkv', kc * jnp.exp(last - cum), vc))
        return new_state, o_inter + o_intra

    state0 = jnp.zeros((bsz, heads, dk, dv), jnp.float32)
    _, o = lax.scan(step, state0, (to_chunks(q), to_chunks(k), to_chunks(v), to_chunks(log_g)))
    return o.transpose(1, 0, 3, 2, 4).reshape(bsz, seq, heads, dv)


def s5_mixer(u, lam_re, lam_im, b_re, b_im, c_re, c_im, d_skip, log_step, w_glu, b_glu):
    f32 = jnp.float32
    bsz, seq, _ = u.shape
    uf = u.astype(f32).reshape(bsz, seq, S5_GROUPS, S5_GROUP_DIM)
    lam = lax.complex(lam_re.astype(f32), lam_im.astype(f32))
    delta = jnp.exp(log_step.astype(f32))[:, None]
    lam_bar = jnp.exp(lam * delta)
    b_mat = lax.complex(b_re.astype(f32), b_im.astype(f32))
    b_bar = ((lam_bar - 1.0) / lam)[:, :, None] * b_mat
    bu = jnp.einsum('gph,blgh->blgp', b_bar, uf)
    a = jnp.broadcast_to(lam_bar, (1, seq, S5_GROUPS, S5_STATE))
    _, states = lax.associative_scan(_linear_recurrence_combine, (a, bu), axis=1)
    c_mat = lax.complex(c_re.astype(f32), c_im.astype(f32))
    y = (jnp.einsum('ghp,blgp->blgh', c_mat, states).real
         + d_skip.astype(f32).reshape(S5_GROUPS, S5_GROUP_DIM) * uf)
    y = jax.nn.gelu(y.reshape(bsz, seq, S5_WIDTH))
    return y * jax.nn.sigmoid(y @ w_glu.astype(f32) + b_glu.astype(f32))


def hgrn2_mixer(q_raw, f_raw, i_raw, g_raw, lower_bound, norm_g):
    f32 = jnp.float32
    bsz, seq, _ = q_raw.shape
    shp = (bsz, seq, HG_HEADS, HG_HEAD_DIM)
    f_pre = f_raw.astype(f32)
    lb = lower_bound.astype(f32)
    forget = lb + (1 - lb) * jax.nn.sigmoid(f_pre)
    key = (1 - lb) * jax.nn.sigmoid(-f_pre)
    query = jax.nn.silu(q_raw.astype(f32))
    o = chunk_gated_linear(query.reshape(shp), key.reshape(shp), i_raw.reshape(shp),
                           jnp.log(forget).reshape(shp))
    o = rms_norm(o, norm_g) * jax.nn.silu(g_raw.astype(f32).reshape(shp))
    return o.reshape(bsz, seq, HG_WIDTH)


def gla_mixer(h, w_in, w_a1, w_a2, b_a, norm_g):
    f32 = jnp.float32
    bsz, seq, _ = h.shape
    kshp = (bsz, seq, GLA_HEADS, GLA_DK)
    vshp = (bsz, seq, GLA_HEADS, GLA_DV)
    z = h @ w_in
    q, k, v, r = jnp.split(z, [GLA_KEY_WIDTH, 2 * GLA_KEY_WIDTH, 2 * GLA_KEY_WIDTH + GLA_VALUE_WIDTH], axis=-1)
    log_a = jax.nn.log_sigmoid(((h @ w_a1) @ w_a2 + b_a).astype(f32)) / GLA_GATE_NORM
    o = chunk_gated_linear(q.astype(f32).reshape(kshp) * (GLA_DK ** -0.5), k.reshape(kshp),
                           v.reshape(vshp), log_a.reshape(kshp))
    o = rms_norm(o, norm_g) * jax.nn.silu(r.astype(f32).reshape(vshp))
    return o.reshape(bsz, seq, GLA_VALUE_WIDTH)


def swiglu(h, w1, w3, w2):
    return (jax.nn.silu(h @ w1) * (h @ w3)) @ w2


def setup_inputs(seed: int = 0) -> dict:
    key = jax.random.key(seed)
    keys = list(jax.random.split(key, 32))

    def nrm(shape, scale):
        return scale * jax.random.normal(keys.pop(), shape, jnp.float32)

    D = D_MODEL
    inp = {}
    inp['x'] = nrm((BATCH, SEQ, D), 1.0)
    inp['c'] = nrm((BATCH, D), 1.0)
    inp['ada_w'] = nrm((DEPTH, D, N_MOD * D), 0.5 * D ** -0.5)
    inp['ada_b'] = nrm((DEPTH, N_MOD * D), 0.02)
    inp['norm_mix_g'] = 1.0 + nrm((DEPTH, D), 0.02)
    inp['norm_ffn_g'] = 1.0 + nrm((DEPTH, D), 0.02)
    inp['ev_w_in'] = nrm((N_EVEN, D, EVEN_IN), D ** -0.5)
    inp['ev_w_out'] = nrm((N_EVEN, MIX_WIDTH, D), MIX_WIDTH ** -0.5)
    inp['s5_lam_re'] = -0.5 * (1.0 + nrm((N_EVEN, S5_GROUPS, S5_STATE), 0.01))
    n_idx = jnp.arange(S5_STATE, dtype=jnp.float32)
    inp['s5_lam_im'] = math.pi * n_idx * (1.0 + nrm((N_EVEN, S5_GROUPS, S5_STATE), 0.01))
    inp['s5_b_re'] = nrm((N_EVEN, S5_GROUPS, S5_STATE, S5_GROUP_DIM), (2 * S5_GROUP_DIM) ** -0.5)
    inp['s5_b_im'] = nrm((N_EVEN, S5_GROUPS, S5_STATE, S5_GROUP_DIM), (2 * S5_GROUP_DIM) ** -0.5)
    inp['s5_c_re'] = nrm((N_EVEN, S5_GROUPS, S5_GROUP_DIM, S5_STATE), S5_STATE ** -0.5)
    inp['s5_c_im'] = nrm((N_EVEN, S5_GROUPS, S5_GROUP_DIM, S5_STATE), S5_STATE ** -0.5)
    inp['s5_d'] = nrm((N_EVEN, S5_WIDTH), 1.0)
    inp['s5_log_step'] = jax.random.uniform(keys.pop(), (N_EVEN, S5_GROUPS), jnp.float32,
                                            minval=math.log(S5_DT_MIN), maxval=math.log(S5_DT_MAX))
    inp['s5_w_glu'] = nrm((N_EVEN, S5_WIDTH, S5_WIDTH), S5_WIDTH ** -0.5)
    inp['s5_b_glu'] = nrm((N_EVEN, S5_WIDTH), 0.02)
    inp['hg_lb_logits'] = nrm((DEPTH + 1, HG_WIDTH), 0.1)
    inp['hg_norm_g'] = 1.0 + nrm((N_EVEN, HG_HEAD_DIM), 0.02)
    inp['od_w_in'] = nrm((N_ODD, D, ODD_IN), D ** -0.5)
    inp['od_w_a1'] = nrm((N_ODD, D, GLA_GATE_RANK), D ** -0.5)
    inp['od_w_a2'] = nrm((N_ODD, GLA_GATE_RANK, GLA_KEY_WIDTH), GLA_GATE_RANK ** -0.5)
    inp['od_b_a'] = nrm((N_ODD, GLA_KEY_WIDTH), 0.02)
    inp['gla_norm_g'] = 1.0 + nrm((N_ODD, GLA_DV), 0.02)
    inp['od_w_out'] = nrm((N_ODD, GLA_VALUE_WIDTH, D), GLA_VALUE_WIDTH ** -0.5)
    inp['ffn_w1'] = nrm((DEPTH, D, FFN_HIDDEN), D ** -0.5)
    inp['ffn_w3'] = nrm((DEPTH, D, FFN_HIDDEN), D ** -0.5)
    inp['ffn_w2'] = nrm((DEPTH, FFN_HIDDEN, D), FFN_HIDDEN ** -0.5)
    inp['final_norm_g'] = 1.0 + nrm((D,), 0.02)
    return inp


def reference(x, c, ada_w, ada_b, norm_mix_g, norm_ffn_g, ev_w_in, ev_w_out,
              s5_lam_re, s5_lam_im, s5_b_re, s5_b_im, s5_c_re, s5_c_im, s5_d, s5_log_step,
              s5_w_glu, s5_b_glu, hg_lb_logits, hg_norm_g, od_w_in, od_w_a1, od_w_a2, od_b_a,
              gla_norm_g, od_w_out, ffn_w1, ffn_w3, ffn_w2, final_norm_g):
    lower_bounds = jnp.cumsum(jax.nn.softmax(hg_lb_logits.astype(jnp.float32), axis=0), axis=0)
    cond = jax.nn.silu(c)
    for layer in range(DEPTH):
        mod = cond @ ada_w[layer] + ada_b[layer]
        shift_m, scale_m, gate_m, shift_f, scale_f, gate_f = jnp.split(mod, N_MOD, axis=-1)
        h = modulate(rms_norm(x, norm_mix_g[layer]), shift_m, scale_m)
        if layer % 2 == 0:
            e = layer // 2
            z = h @ ev_w_in[e]
            u, hq, hf, hi, hg = jnp.split(
                z, [S5_WIDTH, S5_WIDTH + HG_WIDTH, S5_WIDTH + 2 * HG_WIDTH, S5_WIDTH + 3 * HG_WIDTH], axis=-1)
            y_a = s5_mixer(u, s5_lam_re[e], s5_lam_im[e], s5_b_re[e], s5_b_im[e], s5_c_re[e],
                           s5_c_im[e], s5_d[e], s5_log_step[e], s5_w_glu[e], s5_b_glu[e])
            y_b = hgrn2_mixer(hq, hf, hi, hg, lower_bounds[layer], hg_norm_g[e])
            mixed = jnp.concatenate([y_a, y_b], axis=-1).astype(x.dtype) @ ev_w_out[e]
        else:
            o = layer // 2
            y_c = gla_mixer(h, od_w_in[o], od_w_a1[o], od_w_a2[o], od_b_a[o], gla_norm_g[o])
            mixed = y_c.astype(x.dtype) @ od_w_out[o]
        x = x + gate_m[:, None, :] * mixed
        h = modulate(rms_norm(x, norm_ffn_g[layer]), shift_f, scale_f)
        x = x + gate_f[:, None, :] * swiglu(h, ffn_w1[layer], ffn_w3[layer], ffn_w2[layer])
    return rms_norm(x, final_norm_g)
```

```python
import functools
import math

import numpy as np
import jax
import jax.numpy as jnp
from jax import lax
from jax.experimental import pallas as pl
from jax.experimental.pallas import tpu as pltpu

F32 = jnp.float32
BF16 = jnp.bfloat16
EPS = 1e-6

S5_GROUP_DIM = 16
S5_STATE = 64
HG_HEADS = 4
GLA_HEADS = 4
GLA_GATE_NORM = 16.0
N_MOD = 6

VMEM_LIMIT_BYTES = 56 * 1024 * 1024
LANES = 128

TOKEN_TILE = 512
RECUR_TILE = 512
RECUR_CHUNK = 128
S5_CHUNK = 32
FFN_TILE = 256


def _dot(a, b):
    return jnp.dot(a, b, preferred_element_type=F32)


def _dot_nt(a, b):
    return lax.dot_general(a, b, (((1,), (1,)), ((), ())), preferred_element_type=F32)


def _dot_tn(a, b):
    return lax.dot_general(a, b, (((0,), (0,)), ((), ())), preferred_element_type=F32)


def _silu(x):
    return x * jax.nn.sigmoid(x)


def _params(semantics):
    return pltpu.CompilerParams(dimension_semantics=semantics, vmem_limit_bytes=VMEM_LIMIT_BYTES)


def _const_spec(shape):
    nd = len(shape)
    return pl.BlockSpec(shape, lambda *_: (0,) * nd, pipeline_mode=pl.Buffered(1))


def _norm_modulate(x, norm_g, shift, scale):
    ms = jnp.mean(x * x, axis=-1, keepdims=True)
    h = x * lax.rsqrt(ms + EPS) * norm_g
    return h * (1.0 + scale) + shift


def _adaln_kernel(ct_ref, w_ref, b_ref, o_ref):
    cond = _silu(ct_ref[...])
    w = w_ref[0]
    for b in range(cond.shape[1]):
        o_ref[0, b:b + 1, :] = jnp.sum(cond[:, b:b + 1] * w, axis=0, keepdims=True) + b_ref[0]


def _adaln(c, ada_w, ada_b):
    depth, d, n = ada_w.shape
    bsz = c.shape[0]
    nt = 1024
    return pl.pallas_call(
        _adaln_kernel,
        grid=(depth, n // nt),
        in_specs=[
            pl.BlockSpec((d, bsz), lambda l, j: (0, 0)),
            pl.BlockSpec((1, d, nt), lambda l, j: (l, 0, j)),
            pl.BlockSpec((1, 1, nt), lambda l, j: (l, 0, j)),
        ],
        out_specs=pl.BlockSpec((1, bsz, nt), lambda l, j: (l, 0, j)),
        out_shape=jax.ShapeDtypeStruct((depth, bsz, n), F32),
        compiler_params=_params(("arbitrary", "arbitrary")),
        name="adaln_mod",
    )(c.T, ada_w, ada_b.reshape(depth, 1, n))


def _inproj_even_kernel(x_ref, mod_ref, ng_ref, w_ref, lbl_ref,
                        u_ref, q_ref, k_ref, g_ref, v_ref, gt_ref, *, lb_row):
    h = _norm_modulate(x_ref[...], ng_ref[...], mod_ref[0, 0:1, :], mod_ref[0, 1:2, :])
    hb = h.astype(BF16)
    w = u_ref.shape[1]
    z = [_dot(hb, w_ref[:, p * w:(p + 1) * w]) for p in range(5)]
    lg = lbl_ref[...]
    e = jnp.exp(lg - jnp.max(lg, axis=0, keepdims=True))
    lb = jnp.sum(e[:lb_row + 1], axis=0, keepdims=True) / jnp.sum(e, axis=0, keepdims=True)
    u_ref[...] = z[0].astype(BF16)
    q_ref[...] = _silu(z[1]).astype(BF16)
    f = z[2]
    k_ref[...] = ((1.0 - lb) * jax.nn.sigmoid(-f)).astype(BF16)
    g_ref[...] = jnp.log(lb + (1.0 - lb) * jax.nn.sigmoid(f))
    v_ref[...] = z[3].astype(BF16)
    gt_ref[...] = _silu(z[4]).astype(BF16)


def _inproj_even(x2, mod, norm_g, w_in, lb_logits, lb_row, tiles_per_batch):
    t, d = x2.shape
    w = w_in.shape[1] // 5
    tb = TOKEN_TILE
    row = lambda i: (i, 0)
    out_bf = jax.ShapeDtypeStruct((t, w), BF16)
    return pl.pallas_call(
        functools.partial(_inproj_even_kernel, lb_row=lb_row),
        grid=(t // tb,),
        in_specs=[
            pl.BlockSpec((tb, d), row),
            pl.BlockSpec((1, N_MOD, d), lambda i: (i // tiles_per_batch, 0, 0)),
            _const_spec((1, d)),
            _const_spec(w_in.shape),
            _const_spec(lb_logits.shape),
        ],
        out_specs=[pl.BlockSpec((tb, w), row)] * 6,
        out_shape=[out_bf, out_bf, out_bf, jax.ShapeDtypeStruct((t, w), F32), out_bf, out_bf],
        compiler_params=_params(("arbitrary",)),
        name="inproj_even",
    )(x2, mod, norm_g, w_in, lb_logits)


def _inproj_odd_kernel(x_ref, mod_ref, ng_ref, w_ref, wa1_ref, wa2_ref, ba_ref,
                       q_ref, k_ref, g_ref, v_ref, gt_ref, *, q_scale):
    h = _norm_modulate(x_ref[...], ng_ref[...], mod_ref[0, 0:1, :], mod_ref[0, 1:2, :])
    hb = h.astype(BF16)
    kw = q_ref.shape[1]
    vw = v_ref.shape[1]
    q_ref[...] = (_dot(hb, w_ref[:, 0:kw]) * q_scale).astype(BF16)
    k_ref[...] = _dot(hb, w_ref[:, kw:2 * kw]).astype(BF16)
    v_ref[...] = _dot(hb, w_ref[:, 2 * kw:2 * kw + vw]).astype(BF16)
    gt_ref[...] = _silu(_dot(hb, w_ref[:, 2 * kw + vw:2 * kw + 2 * vw])).astype(BF16)
    a1 = _dot(hb, wa1_ref[...]).astype(BF16)
    za = _dot(a1, wa2_ref[...]) + ba_ref[...]
    log_sig = jnp.minimum(za, 0.0) - jnp.log(1.0 + jnp.exp(-jnp.abs(za)))
    g_ref[...] = log_sig * (1.0 / GLA_GATE_NORM)


def _inproj_odd(x2, mod, norm_g, w_in, w_a1, w_a2, b_a, kw, vw, q_scale, tiles_per_batch):
    t, d = x2.shape
    tb = TOKEN_TILE
    row = lambda i: (i, 0)
    return pl.pallas_call(
        functools.partial(_inproj_odd_kernel, q_scale=q_scale),
        grid=(t // tb,),
        in_specs=[
            pl.BlockSpec((tb, d), row),
            pl.BlockSpec((1, N_MOD, d), lambda i: (i // tiles_per_batch, 0, 0)),
            _const_spec((1, d)),
            _const_spec(w_in.shape),
            _const_spec(w_a1.shape),
            _const_spec(w_a2.shape),
            _const_spec(b_a.shape),
        ],
        out_specs=[pl.BlockSpec((tb, kw), row), pl.BlockSpec((tb, kw), row), pl.BlockSpec((tb, kw), row),
                   pl.BlockSpec((tb, vw), row), pl.BlockSpec((tb, vw), row)],
        out_shape=[jax.ShapeDtypeStruct((t, kw), BF16), jax.ShapeDtypeStruct((t, kw), BF16),
                   jax.ShapeDtypeStruct((t, kw), F32), jax.ShapeDtypeStruct((t, vw), BF16),
                   jax.ShapeDtypeStruct((t, vw), BF16)],
        compiler_params=_params(("arbitrary",)),
        name="inproj_odd",
    )(x2, mod, norm_g, w_in, w_a1, w_a2, b_a)


def _s5_kernel(u_ref, mt_ref, wst_ref, wint_ref, sa_ref, sb_ref, y_ref, *, chunks_per_seq):
    u = u_ref[0]
    v = _dot(u, wst_ref[0])
    r, p2 = v.shape
    n_in_seq = lax.broadcasted_iota(jnp.int32, (r, p2), 0) & (chunks_per_seq - 1)
    nlev = sa_ref.shape[1]
    for lv in range(nlev):
        step = 1 << lv
        sh = jnp.where(n_in_seq >= step, pltpu.roll(v, step, 0), 0.0)
        v = v + sa_ref[0, lv:lv + 1, :] * sh + sb_ref[0, lv:lv + 1, :] * pltpu.roll(sh, p2 // 2, 1)
    start = jnp.where(n_in_seq >= 1, pltpu.roll(v, 1, 0), 0.0)
    y = _dot(u, mt_ref[0]) + _dot(start.astype(BF16), wint_ref[0])
    y_ref[0] = y.astype(BF16)


def _s5_operators(lam_re, lam_im, b_re, b_im, c_re, c_im, log_step, chunk, chunks_per_seq):
    hp = lax.Precision.HIGHEST
    g, p = lam_re.shape
    hg = b_re.shape[-1]
    delta = jnp.exp(log_step)[:, None]
    ld_re, ld_im = lam_re * delta, lam_im * delta

    def powers(tau):
        mag = jnp.exp(ld_re[:, None, :] * tau[None, :, None])
        ang = ld_im[:, None, :] * tau[None, :, None]
        return mag * jnp.cos(ang), mag * jnp.sin(ang)

    lb_re, lb_im = powers(jnp.ones((1,), F32))
    lb_re, lb_im = lb_re[:, 0], lb_im[:, 0]
    nr, ni = lb_re - 1.0, lb_im
    den = lam_re * lam_re + lam_im * lam_im
    s_re = (nr * lam_re + ni * lam_im) / den
    s_im = (ni * lam_re - nr * lam_im) / den
    bb_re = s_re[:, :, None] * b_re - s_im[:, :, None] * b_im
    bb_im = s_re[:, :, None] * b_im + s_im[:, :, None] * b_re

    tau = jnp.arange(chunk + 1, dtype=F32)
    pw_re, pw_im = powers(tau)
    z_re = pw_re[:, :, :, None] * bb_re[:, None] - pw_im[:, :, :, None] * bb_im[:, None]
    z_im = pw_re[:, :, :, None] * bb_im[:, None] + pw_im[:, :, :, None] * bb_re[:, None]
    kt = (jnp.einsum('gop,gtpi->gtio', c_re, z_re[:, :chunk], precision=hp)
          - jnp.einsum('gop,gtpi->gtio', c_im, z_im[:, :chunk], precision=hp))
    s_idx = np.arange(chunk)[:, None]
    t_idx = np.arange(chunk)[None, :]
    lag = t_idx - s_idx
    mt = jnp.where((lag >= 0)[None, :, :, None, None], kt[:, np.maximum(lag, 0)], 0.0)
    mt = mt.transpose(0, 1, 3, 2, 4).reshape(g, chunk * hg, chunk * hg)
    wst = jnp.concatenate([z_re[:, chunk - 1::-1], z_im[:, chunk - 1::-1]], axis=2)
    wst = wst.transpose(0, 1, 3, 2).reshape(g, chunk * hg, 2 * p)
    cl_re = c_re[:, None] * pw_re[:, 1:, None, :] - c_im[:, None] * pw_im[:, 1:, None, :]
    cl_im = c_re[:, None] * pw_im[:, 1:, None, :] + c_im[:, None] * pw_re[:, 1:, None, :]
    wint = jnp.concatenate([cl_re, -cl_im], axis=3)
    wint = wint.transpose(0, 3, 1, 2).reshape(g, 2 * p, chunk * hg)
    nlev = int(math.log2(chunks_per_seq))
    sc_re, sc_im = powers(jnp.asarray([chunk * (1 << lv) for lv in range(nlev)], F32))
    sa = jnp.concatenate([sc_re, sc_re], axis=2)
    sb = jnp.concatenate([-sc_im, sc_im], axis=2)
    return mt.astype(BF16), wst.astype(BF16), wint.astype(BF16), sa, sb


def _s5_mix(u2, mt, wst, wint, sa, sb, chunks_per_seq):
    g, r, k = u2.shape
    grp = lambda i: (i, 0, 0)
    return pl.pallas_call(
        functools.partial(_s5_kernel, chunks_per_seq=chunks_per_seq),
        grid=(g,),
        in_specs=[pl.BlockSpec((1, r, k), grp), pl.BlockSpec((1,) + mt.shape[1:], grp),
                  pl.BlockSpec((1,) + wst.shape[1:], grp), pl.BlockSpec((1,) + wint.shape[1:], grp),
                  pl.BlockSpec((1,) + sa.shape[1:], grp), pl.BlockSpec((1,) + sb.shape[1:], grp)],
        out_specs=pl.BlockSpec((1, r, k), grp),
        out_shape=jax.ShapeDtypeStruct((g, r, k), BF16),
        compiler_params=_params(("arbitrary",)),
        name="s5_mix",
    )(u2, mt, wst, wint, sa, sb)


def _level_index(chunk):
    i = np.arange(chunk)[:, None]
    j = np.arange(chunk)[None, :]
    x = np.bitwise_xor(i, j)
    lvl = np.floor(np.log2(np.maximum(x, 1))).astype(np.int32)
    return np.where(i > j, lvl, np.where(i == j, -1, -2)).astype(np.int32)


def _recur_kernel(lvl_ref, q_ref, k_ref, g_ref, v_ref, gt_ref, ng_ref, y_ref, st_ref,
                  *, chunk, heads, dk, dv):
    @pl.when(pl.program_id(1) == 0)
    def _():
        st_ref[...] = jnp.zeros(st_ref.shape, F32)

    n_chunks = q_ref.shape[0] // chunk
    n_levels = int(math.log2(chunk))
    lvl = lvl_ref[...]
    row = lax.broadcasted_iota(jnp.int32, (chunk, dk), 0)
    ng = ng_ref[...]

    def one_chunk(c, carry):
        r0 = pl.multiple_of(c * chunk, chunk)
        rows = pl.ds(r0, chunk)
        for hd in range(heads):
            ks = slice(hd * dk, (hd + 1) * dk)
            vs = slice(hd * dv, (hd + 1) * dv)
            qb = q_ref[rows, ks]
            kb = k_ref[rows, ks]
            vb = v_ref[rows, vs]
            q = qb.astype(F32)
            k = kb.astype(F32)
            pre = g_ref[rows, ks]
            tot = pre
            scores = jnp.where(lvl == -1, _dot_nt(qb, kb), 0.0)
            for lv in range(n_levels):
                h = 1 << lv
                second = (row & h) != 0
                decay = jnp.exp(jnp.where(second, pre, tot - pre))
                w = (jnp.where(second, q, k) * decay).astype(BF16)
                scores = jnp.where(lvl == lv, _dot_nt(w, w), scores)
                below = pltpu.roll(tot, h, 0)
                above = pltpu.roll(tot, chunk - h, 0)
                pre = pre + jnp.where(second, below, 0.0)
                tot = tot + jnp.where(second, below, above)
            st = st_ref[hd]
            o = _dot(scores.astype(BF16), vb)
            o = o + _dot_nt((q * jnp.exp(pre)).astype(BF16), st.astype(BF16))
            kd = (k * jnp.exp(tot - pre)).astype(BF16)
            st_ref[hd] = st * jnp.exp(tot[0:1, :]) + _dot_tn(vb, kd)
            ms = jnp.mean(o * o, axis=-1, keepdims=True)
            o = o * lax.rsqrt(ms + EPS) * ng * gt_ref[rows, vs].astype(F32)
            y_ref[rows, vs] = o.astype(BF16)
        return carry

    lax.fori_loop(0, n_chunks, one_chunk, 0)


def _gated_recurrence(q, k, g, v, gate, norm_g, bsz, heads):
    t, kw = q.shape
    vw = v.shape[1]
    dk, dv = kw // heads, vw // heads
    tt = RECUR_TILE
    per_seq = (t // bsz) // tt
    row = lambda b, i: (b * per_seq + i, 0)
    lvl = jnp.asarray(_level_index(RECUR_CHUNK))
    return pl.pallas_call(
        functools.partial(_recur_kernel, chunk=RECUR_CHUNK, heads=heads, dk=dk, dv=dv),
        grid=(bsz, per_seq),
        in_specs=[
            pl.BlockSpec(lvl.shape, lambda b, i: (0, 0)),
            pl.BlockSpec((tt, kw), row), pl.BlockSpec((tt, kw), row), pl.BlockSpec((tt, kw), row),
            pl.BlockSpec((tt, vw), row), pl.BlockSpec((tt, vw), row),
            pl.BlockSpec((1, dv), lambda b, i: (0, 0)),
        ],
        out_specs=pl.BlockSpec((tt, vw), row),
        out_shape=jax.ShapeDtypeStruct((t, vw), BF16),
        scratch_shapes=[pltpu.VMEM((heads, dv, dk), F32)],
        compiler_params=_params(("arbitrary", "arbitrary")),
        name="gated_recurrence",
    )(lvl, q, k, g, v, gate, norm_g)


def _gelu_tanh(x):
    return 0.5 * x * (1.0 + jnp.tanh(math.sqrt(2.0 / math.pi) * (x + 0.044715 * (x * x * x))))


def _ffn_tail(x, mod_ref, nf_ref, w1_ref, w3_ref, w2_ref, acc_ref):
    h = _norm_modulate(x, nf_ref[...], mod_ref[0, 3:4, :], mod_ref[0, 4:5, :]).astype(BF16)
    acc_ref[...] = jnp.zeros(acc_ref.shape, F32)

    def step(j, carry):
        a = _dot(h, w1_ref[j])
        b = _dot(h, w3_ref[j])
        acc_ref[...] += _dot((_silu(a) * b).astype(BF16), w2_ref[j])
        return carry

    lax.fori_loop(0, w1_ref.shape[0], step, 0)
    return x + mod_ref[0, 5:6, :] * acc_ref[...]


def _block_even_kernel(x_ref, yc_ref, u_ref, yb_ref, mod_ref, sd_ref, wg_ref, bg_ref, wo_ref,
                       nf_ref, w1_ref, w3_ref, w2_ref, o_ref, acc_ref):
    y = yc_ref[...].astype(F32) + sd_ref[...] * u_ref[...].astype(F32)
    y = _gelu_tanh(y)
    ya = y * jax.nn.sigmoid(_dot(y.astype(BF16), wg_ref[...]) + bg_ref[...])
    sw = ya.shape[1]
    mixed = _dot(ya.astype(BF16), wo_ref[0:sw, :]) + _dot(yb_ref[...], wo_ref[sw:, :])
    x = x_ref[...] + mod_ref[0, 2:3, :] * mixed
    o_ref[...] = _ffn_tail(x, mod_ref, nf_ref, w1_ref, w3_ref, w2_ref, acc_ref)


def _block_odd_kernel(x_ref, yc_ref, mod_ref, wo_ref, nf_ref, w1_ref, w3_ref, w2_ref, fg_ref,
                      o_ref, acc_ref, *, final_norm):
    x = x_ref[...] + mod_ref[0, 2:3, :] * _dot(yc_ref[...], wo_ref[...])
    x = _ffn_tail(x, mod_ref, nf_ref, w1_ref, w3_ref, w2_ref, acc_ref)
    if final_norm:
        ms = jnp.mean(x * x, axis=-1, keepdims=True)
        x = x * lax.rsqrt(ms + EPS) * fg_ref[...]
    o_ref[...] = x


def _ffn_weights(w1, w3, w2):
    d, hid = w1.shape
    nj = hid // FFN_TILE
    w1t = w1.astype(BF16).reshape(d, nj, FFN_TILE).transpose(1, 0, 2)
    w3t = w3.astype(BF16).reshape(d, nj, FFN_TILE).transpose(1, 0, 2)
    w2t = w2.astype(BF16).reshape(nj, FFN_TILE, d)
    return w1t, w3t, w2t


def _block_even(x2, yc, u, yb, mod, s5_d, w_glu, b_glu, w_out, nf, w1t, w3t, w2t, tiles_per_batch):
    t, d = x2.shape
    tb = TOKEN_TILE
    row = lambda i: (i, 0)
    sw = yc.shape[1]
    return pl.pallas_call(
        _block_even_kernel,
        grid=(t // tb,),
        in_specs=[
            pl.BlockSpec((tb, d), row), pl.BlockSpec((tb, sw), row), pl.BlockSpec((tb, sw), row),
            pl.BlockSpec((tb, yb.shape[1]), row),
            pl.BlockSpec((1, N_MOD, d), lambda i: (i // tiles_per_batch, 0, 0)),
            _const_spec(s5_d.shape), _const_spec(w_glu.shape), _const_spec(b_glu.shape),
            _const_spec(w_out.shape), _const_spec(nf.shape),
            _const_spec(w1t.shape), _const_spec(w3t.shape), _const_spec(w2t.shape),
        ],
        out_specs=pl.BlockSpec((tb, d), row),
        out_shape=jax.ShapeDtypeStruct((t, d), F32),
        scratch_shapes=[pltpu.VMEM((tb, d), F32)],
        compiler_params=_params(("arbitrary",)),
        name="block_even",
    )(x2, yc, u, yb, mod, s5_d, w_glu, b_glu, w_out, nf, w1t, w3t, w2t)


def _block_odd(x2, yc, mod, w_out, nf, w1t, w3t, w2t, final_g, final_norm, tiles_per_batch):
    t, d = x2.shape
    tb = TOKEN_TILE
    row = lambda i: (i, 0)
    return pl.pallas_call(
        functools.partial(_block_odd_kernel, final_norm=final_norm),
        grid=(t // tb,),
        in_specs=[
            pl.BlockSpec((tb, d), row), pl.BlockSpec((tb, yc.shape[1]), row),
            pl.BlockSpec((1, N_MOD, d), lambda i: (i // tiles_per_batch, 0, 0)),
            _const_spec(w_out.shape), _const_spec(nf.shape),
            _const_spec(w1t.shape), _const_spec(w3t.shape), _const_spec(w2t.shape),
            _const_spec(final_g.shape),
        ],
        out_specs=pl.BlockSpec((tb, d), row),
        out_shape=jax.ShapeDtypeStruct((t, d), F32),
        scratch_shapes=[pltpu.VMEM((tb, d), F32)],
        compiler_params=_params(("arbitrary",)),
        name="block_odd",
    )(x2, yc, mod, w_out, nf, w1t, w3t, w2t, final_g)


def kernel(x, c, ada_w, ada_b, norm_mix_g, norm_ffn_g, ev_w_in, ev_w_out, s5_lam_re, s5_lam_im, s5_b_re, s5_b_im, s5_c_re, s5_c_im, s5_d, s5_log_step, s5_w_glu, s5_b_glu, hg_lb_logits, hg_norm_g, od_w_in, od_w_a1, od_w_a2, od_b_a, gla_norm_g, od_w_out, ffn_w1, ffn_w3, ffn_w2, final_norm_g):
    bsz, seq, d = x.shape
    depth = ada_w.shape[0]
    assert depth % 2 == 0, "the final norm is fused into the last (odd) layer"
    t = bsz * seq
    tiles_per_batch = seq // TOKEN_TILE
    x2 = x.reshape(t, d)
    mod = _adaln(c, ada_w, ada_b).reshape(depth, bsz, N_MOD, d)

    for layer in range(depth):
        nm = norm_mix_g[layer].reshape(1, d)
        nf = norm_ffn_g[layer].reshape(1, d)
        w1t, w3t, w2t = _ffn_weights(ffn_w1[layer], ffn_w3[layer], ffn_w2[layer])
        if layer % 2 == 0:
            e = layer // 2
            u, hq, hk, hgl, hv, hgate = _inproj_even(
                x2, mod[layer], nm, ev_w_in[e].astype(BF16), hg_lb_logits, layer, tiles_per_batch)
            sw = u.shape[1]
            groups = sw // S5_GROUP_DIM
            n_chunks = seq // S5_CHUNK
            ops = _s5_operators(s5_lam_re[e], s5_lam_im[e], s5_b_re[e], s5_b_im[e], s5_c_re[e],
                                s5_c_im[e], s5_log_step[e], S5_CHUNK, n_chunks)
            u2 = (u.reshape(bsz, n_chunks, S5_CHUNK, groups, S5_GROUP_DIM)
                  .transpose(3, 0, 1, 2, 4).reshape(groups, bsz * n_chunks, S5_CHUNK * S5_GROUP_DIM))
            y2 = _s5_mix(u2, *ops, n_chunks)
            yc = (y2.reshape(groups, bsz, n_chunks, S5_CHUNK, S5_GROUP_DIM)
                  .transpose(1, 2, 3, 0, 4).reshape(t, sw))
            yb = _gated_recurrence(hq, hk, hgl, hv, hgate, hg_norm_g[e].reshape(1, -1), bsz, HG_HEADS)
            x2 = _block_even(x2, yc, u, yb, mod[layer], s5_d[e].reshape(1, sw),
                             s5_w_glu[e].astype(BF16), s5_b_glu[e].reshape(1, sw),
                             ev_w_out[e].astype(BF16), nf, w1t, w3t, w2t, tiles_per_batch)
        else:
            o = layer // 2
            kw = od_w_a2.shape[-1]
            vw = od_w_out.shape[1]
            rank = od_w_a1.shape[-1]
            w_a1 = jnp.pad(od_w_a1[o], ((0, 0), (0, LANES - rank))).astype(BF16)
            w_a2 = jnp.pad(od_w_a2[o], ((0, LANES - rank), (0, 0))).astype(BF16)
            q, k, g, v, gate = _inproj_odd(
                x2, mod[layer], nm, od_w_in[o].astype(BF16), w_a1, w_a2, od_b_a[o].reshape(1, kw),
                kw, vw, (kw // GLA_HEADS) ** -0.5, tiles_per_batch)
            yc = _gated_recurrence(q, k, g, v, gate, gla_norm_g[o].reshape(1, -1), bsz, GLA_HEADS)
            x2 = _block_odd(x2, yc, mod[layer], od_w_out[o].astype(BF16), nf, w1t, w3t, w2t,
                            final_norm_g.reshape(1, d), layer == depth - 1, tiles_per_batch)
    return x2.reshape(bsz, seq, d)
```

```python
import functools
import math

import numpy as np
import jax
import jax.numpy as jnp
from jax import lax
from jax.experimental import pallas as pl
from jax.experimental.pallas import tpu as pltpu

F32 = jnp.float32
BF16 = jnp.bfloat16
EPS = 1e-6

HG_HEADS = 4
GLA_HEADS = 4
GLA_GATE_NORM = 16.0
N_MOD = 6

VMEM_LIMIT_BYTES = 56 * 1024 * 1024
LANES = 128
GRANULE = 16
GRANULES = LANES // GRANULE

TOKEN_TILE = 512
RECUR_TILE = 512
RECUR_CHUNK = 128
S5_CHUNK = 32
FFN_TILE = 256


def _dot(a, b):
    return jnp.dot(a, b, preferred_element_type=F32)


def _dot_nt(a, b):
    return lax.dot_general(a, b, (((1,), (1,)), ((), ())), preferred_element_type=F32)


def _dot_tn(a, b):
    return lax.dot_general(a, b, (((0,), (0,)), ((), ())), preferred_element_type=F32)


def _silu(x):
    return x * jax.nn.sigmoid(x)


def _params(semantics):
    return pltpu.CompilerParams(dimension_semantics=semantics, vmem_limit_bytes=VMEM_LIMIT_BYTES)


def _const_spec(shape):
    nd = len(shape)
    return pl.BlockSpec(shape, lambda *_: (0,) * nd, pipeline_mode=pl.Buffered(1))


def _norm_modulate(x, norm_g, shift, scale):
    ms = jnp.mean(x * x, axis=-1, keepdims=True)
    h = x * lax.rsqrt(ms + EPS) * norm_g
    return h * (1.0 + scale) + shift


def _adaln_kernel(ct_ref, w_ref, b_ref, o_ref):
    cond = _silu(ct_ref[...])
    w = w_ref[0]
    for b in range(cond.shape[1]):
        o_ref[0, b:b + 1, :] = jnp.sum(cond[:, b:b + 1] * w, axis=0, keepdims=True) + b_ref[0]


def _adaln(c, ada_w, ada_b):
    depth, d, n = ada_w.shape
    bsz = c.shape[0]
    nt = 1024
    return pl.pallas_call(
        _adaln_kernel,
        grid=(depth, n // nt),
        in_specs=[
            pl.BlockSpec((d, bsz), lambda l, j: (0, 0)),
            pl.BlockSpec((1, d, nt), lambda l, j: (l, 0, j)),
            pl.BlockSpec((1, 1, nt), lambda l, j: (l, 0, j)),
        ],
        out_specs=pl.BlockSpec((1, bsz, nt), lambda l, j: (l, 0, j)),
        out_shape=jax.ShapeDtypeStruct((depth, bsz, n), F32),
        compiler_params=_params(("arbitrary", "arbitrary")),
        name="adaln_mod",
    )(c.T, ada_w, ada_b.reshape(depth, 1, n))


def _granule_transpose(blocks, slot):
    blocks = list(blocks)
    for d in (4, 2, 1):
        upper = (slot & d) != 0
        for a in range(GRANULES):
            if a & d:
                continue
            lo, hi = blocks[a], blocks[a + d]
            blocks[a] = jnp.where(upper, pltpu.roll(hi, GRANULE * d, 1), lo)
            blocks[a + d] = jnp.where(upper, hi, pltpu.roll(lo, LANES - GRANULE * d, 1))
    return blocks


def _to_chunk_rows(tok_ref, out_ref):
    n_grp, n_rows, k = out_ref.shape
    chunk = k // GRANULE
    slot = lax.broadcasted_iota(jnp.int32, (n_rows, LANES), 1) // GRANULE
    for tb in range(chunk // GRANULES):
        for j in range(n_grp // GRANULES):
            rows = [tok_ref[j, pl.ds(GRANULES * tb + tp, n_rows, stride=chunk), :]
                    for tp in range(GRANULES)]
            for gp, blk in enumerate(_granule_transpose(rows, slot)):
                out_ref[GRANULES * j + gp, :, tb * LANES:(tb + 1) * LANES] = blk.astype(out_ref.dtype)


def _to_token_rows(grp_ref, tok_ref):
    n_grp, n_rows, k = grp_ref.shape
    chunk = k // GRANULE
    slot = lax.broadcasted_iota(jnp.int32, (n_rows, LANES), 1) // GRANULE
    for tb in range(chunk // GRANULES):
        for j in range(n_grp // GRANULES):
            cols = [grp_ref[GRANULES * j + gp, :, tb * LANES:(tb + 1) * LANES].astype(F32)
                    for gp in range(GRANULES)]
            for tp, blk in enumerate(_granule_transpose(cols, slot)):
                tok_ref[j, pl.ds(GRANULES * tb + tp, n_rows, stride=chunk), :] = blk


def _inproj_even_kernel(x_ref, mod_ref, ng_ref, w_ref, lbl_ref,
                        u_ref, u2_ref, q_ref, k_ref, g_ref, v_ref, gt_ref, tok_ref, *, lb_row):
    h = _norm_modulate(x_ref[...], ng_ref[...], mod_ref[0, 0:1, :], mod_ref[0, 1:2, :])
    hb = h.astype(BF16)
    w = u_ref.shape[1]
    z = [_dot(hb, w_ref[:, p * w:(p + 1) * w]) for p in range(5)]
    lg = lbl_ref[...]
    e = jnp.exp(lg - jnp.max(lg, axis=0, keepdims=True))
    lb = jnp.sum(e[:lb_row + 1], axis=0, keepdims=True) / jnp.sum(e, axis=0, keepdims=True)
    for j in range(tok_ref.shape[0]):
        tok_ref[j] = z[0][:, j * LANES:(j + 1) * LANES]
    _to_chunk_rows(tok_ref, u2_ref)
    u_ref[...] = z[0].astype(BF16)
    q_ref[...] = _silu(z[1]).astype(BF16)
    f = z[2]
    k_ref[...] = ((1.0 - lb) * jax.nn.sigmoid(-f)).astype(BF16)
    g_ref[...] = jnp.log(lb + (1.0 - lb) * jax.nn.sigmoid(f))
    v_ref[...] = z[3].astype(BF16)
    gt_ref[...] = _silu(z[4]).astype(BF16)


def _inproj_even(x2, mod, norm_g, w_in, lb_logits, lb_row, tiles_per_batch):
    t, d = x2.shape
    w = w_in.shape[1] // 5
    tb = TOKEN_TILE
    row = lambda i: (i, 0)
    out_bf = jax.ShapeDtypeStruct((t, w), BF16)
    groups = w // GRANULE
    k = S5_CHUNK * GRANULE
    tok = pl.BlockSpec((tb, w), row)
    return pl.pallas_call(
        functools.partial(_inproj_even_kernel, lb_row=lb_row),
        grid=(t // tb,),
        in_specs=[
            pl.BlockSpec((tb, d), row),
            pl.BlockSpec((1, N_MOD, d), lambda i: (i // tiles_per_batch, 0, 0)),
            _const_spec((1, d)),
            _const_spec(w_in.shape),
            _const_spec(lb_logits.shape),
        ],
        out_specs=[tok, pl.BlockSpec((groups, tb // S5_CHUNK, k), lambda i: (0, i, 0)),
                   tok, tok, tok, tok, tok],
        out_shape=[out_bf, jax.ShapeDtypeStruct((groups, t // S5_CHUNK, k), BF16),
                   out_bf, out_bf, jax.ShapeDtypeStruct((t, w), F32), out_bf, out_bf],
        scratch_shapes=[pltpu.VMEM((w // LANES, tb, LANES), F32)],
        compiler_params=_params(("arbitrary",)),
        name="inproj_even",
    )(x2, mod, norm_g, w_in, lb_logits)


def _inproj_odd_kernel(x_ref, mod_ref, ng_ref, w_ref, wa1_ref, wa2_ref, ba_ref,
                       q_ref, k_ref, g_ref, v_ref, gt_ref, *, q_scale):
    h = _norm_modulate(x_ref[...], ng_ref[...], mod_ref[0, 0:1, :], mod_ref[0, 1:2, :])
    hb = h.astype(BF16)
    kw = q_ref.shape[1]
    vw = v_ref.shape[1]
    q_ref[...] = (_dot(hb, w_ref[:, 0:kw]) * q_scale).astype(BF16)
    k_ref[...] = _dot(hb, w_ref[:, kw:2 * kw]).astype(BF16)
    v_ref[...] = _dot(hb, w_ref[:, 2 * kw:2 * kw + vw]).astype(BF16)
    gt_ref[...] = _silu(_dot(hb, w_ref[:, 2 * kw + vw:2 * kw + 2 * vw])).astype(BF16)
    a1 = _dot(hb, wa1_ref[...]).astype(BF16)
    za = _dot(a1, wa2_ref[...]) + ba_ref[...]
    log_sig = jnp.minimum(za, 0.0) - jnp.log(1.0 + jnp.exp(-jnp.abs(za)))
    g_ref[...] = log_sig * (1.0 / GLA_GATE_NORM)


def _inproj_odd(x2, mod, norm_g, w_in, w_a1, w_a2, b_a, kw, vw, q_scale, tiles_per_batch):
    t, d = x2.shape
    tb = TOKEN_TILE
    row = lambda i: (i, 0)
    return pl.pallas_call(
        functools.partial(_inproj_odd_kernel, q_scale=q_scale),
        grid=(t // tb,),
        in_specs=[
            pl.BlockSpec((tb, d), row),
            pl.BlockSpec((1, N_MOD, d), lambda i: (i // tiles_per_batch, 0, 0)),
            _const_spec((1, d)),
            _const_spec(w_in.shape),
            _const_spec(w_a1.shape),
            _const_spec(w_a2.shape),
            _const_spec(b_a.shape),
        ],
        out_specs=[pl.BlockSpec((tb, kw), row), pl.BlockSpec((tb, kw), row), pl.BlockSpec((tb, kw), row),
                   pl.BlockSpec((tb, vw), row), pl.BlockSpec((tb, vw), row)],
        out_shape=[jax.ShapeDtypeStruct((t, kw), BF16), jax.ShapeDtypeStruct((t, kw), BF16),
                   jax.ShapeDtypeStruct((t, kw), F32), jax.ShapeDtypeStruct((t, vw), BF16),
                   jax.ShapeDtypeStruct((t, vw), BF16)],
        compiler_params=_params(("arbitrary",)),
        name="inproj_odd",
    )(x2, mod, norm_g, w_in, w_a1, w_a2, b_a)


def _s5_kernel(u_ref, mt_ref, wsr_ref, wsi_ref, wir_ref, wii_ref, sr_ref, si_ref, y_ref,
               *, chunks_per_seq):
    u = u_ref[0]
    v_re = _dot(u, wsr_ref[0])
    v_im = _dot(u, wsi_ref[0])
    n_in_seq = lax.broadcasted_iota(jnp.int32, v_re.shape, 0) & (chunks_per_seq - 1)
    for lv in range(sr_ref.shape[1]):
        step = 1 << lv
        keep = n_in_seq >= step
        s_re = jnp.where(keep, pltpu.roll(v_re, step, 0), 0.0)
        s_im = jnp.where(keep, pltpu.roll(v_im, step, 0), 0.0)
        a_re, a_im = sr_ref[0, lv:lv + 1, :], si_ref[0, lv:lv + 1, :]
        v_re, v_im = v_re + a_re * s_re - a_im * s_im, v_im + a_re * s_im + a_im * s_re
    keep = n_in_seq >= 1
    x_re = jnp.where(keep, pltpu.roll(v_re, 1, 0), 0.0).astype(BF16)
    x_im = jnp.where(keep, pltpu.roll(v_im, 1, 0), 0.0).astype(BF16)
    y = _dot(u, mt_ref[0]) + _dot_nt(x_re, wir_ref[0]) + _dot_nt(x_im, wii_ref[0])
    y_ref[0] = y.astype(BF16)


def _s5_prep_kernel(bt_re_ref, bt_im_ref, c_re_ref, c_im_ref, lp_re_ref, lp_im_ref,
                    mt_ref, wsr_ref, wsi_ref, wir_ref, wii_ref, toep_ref, *, chunk):
    hg, p = bt_re_ref.shape[1:]
    k = chunk * hg
    t = lax.broadcasted_iota(jnp.int32, (k, p), 0) // hg

    def lam_pow(e):
        re = jnp.ones((k, p), F32)
        im = jnp.zeros((k, p), F32)
        for b in range(lp_re_ref.shape[1]):
            b_re, b_im = lp_re_ref[0, b:b + 1, :], lp_im_ref[0, b:b + 1, :]
            bit = ((e >> b) & 1) == 1
            re, im = (jnp.where(bit, re * b_re - im * b_im, re),
                      jnp.where(bit, re * b_im + im * b_re, im))
        return re, im

    tile = lambda x: jnp.concatenate([x] * chunk, axis=0)
    b_re, b_im = tile(bt_re_ref[0]), tile(bt_im_ref[0])
    c_re, c_im = tile(c_re_ref[0]), tile(c_im_ref[0])

    p_re, p_im = lam_pow(t)
    z_re, z_im = c_re * p_re - c_im * p_im, c_re * p_im + c_im * p_re
    hp = lax.Precision.HIGHEST
    nt = (((1,), (1,)), ((), ()))
    kt = (lax.dot_general(bt_re_ref[0], z_re, nt, precision=hp, preferred_element_type=F32)
          - lax.dot_general(bt_im_ref[0], z_im, nt, precision=hp, preferred_element_type=F32))
    col_t = lax.broadcasted_iota(jnp.int32, (chunk, k), 1) // hg
    row_s = lax.broadcasted_iota(jnp.int32, (chunk, k), 0)
    for hi in range(hg):
        rows = jnp.broadcast_to(kt[hi:hi + 1, :], (chunk, k))
        rows = pltpu.roll(rows, 0, 1, stride=hg, stride_axis=0)
        rows = jnp.where(col_t >= row_s, rows, 0.0)
        for j in range(k // LANES):
            toep_ref[j, pl.ds(hi, chunk, stride=hg), :] = rows[:, j * LANES:(j + 1) * LANES]
    for j in range(k // LANES):
        mt_ref[0, :, j * LANES:(j + 1) * LANES] = toep_ref[j].astype(BF16)
    p_re, p_im = lam_pow(chunk - 1 - t)
    wsr_ref[0] = (b_re * p_re - b_im * p_im).astype(BF16)
    wsi_ref[0] = (b_re * p_im + b_im * p_re).astype(BF16)
    l_re, l_im = lp_re_ref[0, 0:1, :], lp_im_ref[0, 0:1, :]
    wir_ref[0] = (z_re * l_re - z_im * l_im).astype(BF16)
    wii_ref[0] = (-(z_re * l_im + z_im * l_re)).astype(BF16)


def _s5_discretise(lam_re, lam_im, b_re, b_im, log_step, chunk, chunks_per_seq):
    delta = jnp.exp(log_step)[:, None]
    ld_re, ld_im = lam_re * delta, lam_im * delta

    def powers(exponents):
        e = jnp.asarray(exponents, F32)[None, :, None]
        mag = jnp.exp(ld_re[:, None, :] * e)
        ang = ld_im[:, None, :] * e
        return mag * jnp.cos(ang), mag * jnp.sin(ang)

    lp_re, lp_im = powers([1 << b for b in range(chunk.bit_length())])
    nr, ni = lp_re[:, 0] - 1.0, lp_im[:, 0]
    den = lam_re * lam_re + lam_im * lam_im
    s_re = (nr * lam_re + ni * lam_im) / den
    s_im = (ni * lam_re - nr * lam_im) / den
    bt_re = (s_re[:, :, None] * b_re - s_im[:, :, None] * b_im).transpose(0, 2, 1)
    bt_im = (s_re[:, :, None] * b_im + s_im[:, :, None] * b_re).transpose(0, 2, 1)
    n_lev = int(math.log2(chunks_per_seq))
    sc_re, sc_im = powers([chunk * (1 << lv) for lv in range(n_lev)])
    return bt_re, bt_im, lp_re, lp_im, sc_re, sc_im


def _s5_mix(u2, lam_re, lam_im, b_re, b_im, c_re, c_im, log_step, chunks_per_seq):
    g, r, k = u2.shape
    chunk = k // GRANULE
    p = lam_re.shape[1]
    bt_re, bt_im, lp_re, lp_im, sc_re, sc_im = _s5_discretise(
        lam_re, lam_im, b_re, b_im, log_step, chunk, chunks_per_seq)
    grp = lambda i: (i, 0, 0)
    small = lambda a: pl.BlockSpec((1,) + a.shape[1:], grp)
    kp = jax.ShapeDtypeStruct((g, k, p), BF16)
    ops = pl.pallas_call(
        functools.partial(_s5_prep_kernel, chunk=chunk),
        grid=(g,),
        in_specs=[small(bt_re), small(bt_im), small(c_re), small(c_im), small(lp_re), small(lp_im)],
        out_specs=[pl.BlockSpec((1, k, k), grp)] + [pl.BlockSpec((1, k, p), grp)] * 4,
        out_shape=[jax.ShapeDtypeStruct((g, k, k), BF16), kp, kp, kp, kp],
        scratch_shapes=[pltpu.VMEM((k // LANES, k, LANES), F32)],
        compiler_params=_params(("arbitrary",)),
        name="s5_prep",
    )(bt_re, bt_im, c_re, c_im, lp_re, lp_im)
    return pl.pallas_call(
        functools.partial(_s5_kernel, chunks_per_seq=chunks_per_seq),
        grid=(g,),
        in_specs=[pl.BlockSpec((1, r, k), grp), pl.BlockSpec((1, k, k), grp)]
                 + [pl.BlockSpec((1, k, p), grp)] * 4 + [small(sc_re), small(sc_im)],
        out_specs=pl.BlockSpec((1, r, k), grp),
        out_shape=jax.ShapeDtypeStruct((g, r, k), BF16),
        compiler_params=_params(("arbitrary",)),
        name="s5_mix",
    )(u2, *ops, sc_re, sc_im)


def _level_index(chunk):
    i = np.arange(chunk)[:, None]
    j = np.arange(chunk)[None, :]
    x = np.bitwise_xor(i, j)
    lvl = np.floor(np.log2(np.maximum(x, 1))).astype(np.int32)
    return np.where(i > j, lvl, np.where(i == j, -1, -2)).astype(np.int32)


def _recur_kernel(lvl_ref, q_ref, k_ref, g_ref, v_ref, gt_ref, ng_ref, y_ref, st_ref,
                  *, chunk, heads, dk, dv):
    @pl.when(pl.program_id(1) == 0)
    def _():
        st_ref[...] = jnp.zeros(st_ref.shape, F32)

    n_chunks = q_ref.shape[0] // chunk
    n_levels = int(math.log2(chunk))
    lvl = lvl_ref[...]
    row = lax.broadcasted_iota(jnp.int32, (chunk, dk), 0)
    ng = ng_ref[...]

    def one_chunk(c, carry):
        r0 = pl.multiple_of(c * chunk, chunk)
        rows = pl.ds(r0, chunk)
        for hd in range(heads):
            ks = slice(hd * dk, (hd + 1) * dk)
            vs = slice(hd * dv, (hd + 1) * dv)
            qb = q_ref[rows, ks]
            kb = k_ref[rows, ks]
            vb = v_ref[rows, vs]
            q = qb.astype(F32)
            k = kb.astype(F32)
            pre = g_ref[rows, ks]
            tot = pre
            scores = jnp.where(lvl == -1, _dot_nt(qb, kb), 0.0)
            for lv in range(n_levels):
                h = 1 << lv
                second = (row & h) != 0
                decay = jnp.exp(jnp.where(second, pre, tot - pre))
                w = (jnp.where(second, q, k) * decay).astype(BF16)
                scores = jnp.where(lvl == lv, _dot_nt(w, w), scores)
                below = pltpu.roll(tot, h, 0)
                above = pltpu.roll(tot, chunk - h, 0)
                pre = pre + jnp.where(second, below, 0.0)
                tot = tot + jnp.where(second, below, above)
            st = st_ref[hd]
            o = _dot(scores.astype(BF16), vb)
            o = o + _dot_nt((q * jnp.exp(pre)).astype(BF16), st.astype(BF16))
            kd = (k * jnp.exp(tot - pre)).astype(BF16)
            st_ref[hd] = st * jnp.exp(tot[0:1, :]) + _dot_tn(vb, kd)
            ms = jnp.mean(o * o, axis=-1, keepdims=True)
            o = o * lax.rsqrt(ms + EPS) * ng * gt_ref[rows, vs].astype(F32)
            y_ref[rows, vs] = o.astype(BF16)
        return carry

    lax.fori_loop(0, n_chunks, one_chunk, 0)


def _gated_recurrence(q, k, g, v, gate, norm_g, bsz, heads):
    t, kw = q.shape
    vw = v.shape[1]
    dk, dv = kw // heads, vw // heads
    tt = RECUR_TILE
    per_seq = (t // bsz) // tt
    row = lambda b, i: (b * per_seq + i, 0)
    lvl = jnp.asarray(_level_index(RECUR_CHUNK))
    return pl.pallas_call(
        functools.partial(_recur_kernel, chunk=RECUR_CHUNK, heads=heads, dk=dk, dv=dv),
        grid=(bsz, per_seq),
        in_specs=[
            pl.BlockSpec(lvl.shape, lambda b, i: (0, 0)),
            pl.BlockSpec((tt, kw), row), pl.BlockSpec((tt, kw), row), pl.BlockSpec((tt, kw), row),
            pl.BlockSpec((tt, vw), row), pl.BlockSpec((tt, vw), row),
            pl.BlockSpec((1, dv), lambda b, i: (0, 0)),
        ],
        out_specs=pl.BlockSpec((tt, vw), row),
        out_shape=jax.ShapeDtypeStruct((t, vw), BF16),
        scratch_shapes=[pltpu.VMEM((heads, dv, dk), F32)],
        compiler_params=_params(("arbitrary", "arbitrary")),
        name="gated_recurrence",
    )(lvl, q, k, g, v, gate, norm_g)


def _gelu_tanh(x):
    return 0.5 * x * (1.0 + jnp.tanh(math.sqrt(2.0 / math.pi) * (x + 0.044715 * (x * x * x))))


def _ffn_tail(x, mod_ref, nf_ref, w1_ref, w3_ref, w2_ref, acc_ref):
    h = _norm_modulate(x, nf_ref[...], mod_ref[0, 3:4, :], mod_ref[0, 4:5, :]).astype(BF16)
    acc_ref[...] = jnp.zeros(acc_ref.shape, F32)

    def step(j, carry):
        a = _dot(h, w1_ref[j])
        b = _dot(h, w3_ref[j])
        acc_ref[...] += _dot((_silu(a) * b).astype(BF16), w2_ref[j])
        return carry

    lax.fori_loop(0, w1_ref.shape[0], step, 0)
    return x + mod_ref[0, 5:6, :] * acc_ref[...]


def _block_even_kernel(x_ref, y2_ref, u_ref, yb_ref, mod_ref, sd_ref, wg_ref, bg_ref, wo_ref,
                       nf_ref, w1_ref, w3_ref, w2_ref, o_ref, acc_ref, tok_ref):
    _to_token_rows(y2_ref, tok_ref)
    y = jnp.concatenate([tok_ref[j] for j in range(tok_ref.shape[0])], axis=1)
    y = y + sd_ref[...] * u_ref[...].astype(F32)
    y = _gelu_tanh(y)
    ya = y * jax.nn.sigmoid(_dot(y.astype(BF16), wg_ref[...]) + bg_ref[...])
    sw = ya.shape[1]
    mixed = _dot(ya.astype(BF16), wo_ref[0:sw, :]) + _dot(yb_ref[...], wo_ref[sw:, :])
    x = x_ref[...] + mod_ref[0, 2:3, :] * mixed
    o_ref[...] = _ffn_tail(x, mod_ref, nf_ref, w1_ref, w3_ref, w2_ref, acc_ref)


def _block_odd_kernel(x_ref, yc_ref, mod_ref, wo_ref, nf_ref, w1_ref, w3_ref, w2_ref, fg_ref,
                      o_ref, acc_ref, *, final_norm):
    x = x_ref[...] + mod_ref[0, 2:3, :] * _dot(yc_ref[...], wo_ref[...])
    x = _ffn_tail(x, mod_ref, nf_ref, w1_ref, w3_ref, w2_ref, acc_ref)
    if final_norm:
        ms = jnp.mean(x * x, axis=-1, keepdims=True)
        x = x * lax.rsqrt(ms + EPS) * fg_ref[...]
    o_ref[...] = x


def _ffn_weights(w1, w3, w2):
    d, hid = w1.shape
    nj = hid // FFN_TILE
    w1t = w1.astype(BF16).reshape(d, nj, FFN_TILE).transpose(1, 0, 2)
    w3t = w3.astype(BF16).reshape(d, nj, FFN_TILE).transpose(1, 0, 2)
    w2t = w2.astype(BF16).reshape(nj, FFN_TILE, d)
    return w1t, w3t, w2t


def _block_even(x2, y2, u, yb, mod, s5_d, w_glu, b_glu, w_out, nf, w1t, w3t, w2t, tiles_per_batch):
    t, d = x2.shape
    tb = TOKEN_TILE
    row = lambda i: (i, 0)
    sw = u.shape[1]
    groups, _, k = y2.shape
    return pl.pallas_call(
        _block_even_kernel,
        grid=(t // tb,),
        in_specs=[
            pl.BlockSpec((tb, d), row),
            pl.BlockSpec((groups, tb * GRANULE // k, k), lambda i: (0, i, 0)),
            pl.BlockSpec((tb, sw), row),
            pl.BlockSpec((tb, yb.shape[1]), row),
            pl.BlockSpec((1, N_MOD, d), lambda i: (i // tiles_per_batch, 0, 0)),
            _const_spec(s5_d.shape), _const_spec(w_glu.shape), _const_spec(b_glu.shape),
            _const_spec(w_out.shape), _const_spec(nf.shape),
            _const_spec(w1t.shape), _const_spec(w3t.shape), _const_spec(w2t.shape),
        ],
        out_specs=pl.BlockSpec((tb, d), row),
        out_shape=jax.ShapeDtypeStruct((t, d), F32),
        scratch_shapes=[pltpu.VMEM((tb, d), F32), pltpu.VMEM((sw // LANES, tb, LANES), F32)],
        compiler_params=_params(("arbitrary",)),
        name="block_even",
    )(x2, y2, u, yb, mod, s5_d, w_glu, b_glu, w_out, nf, w1t, w3t, w2t)


def _block_odd(x2, yc, mod, w_out, nf, w1t, w3t, w2t, final_g, final_norm, tiles_per_batch):
    t, d = x2.shape
    tb = TOKEN_TILE
    row = lambda i: (i, 0)
    return pl.pallas_call(
        functools.partial(_block_odd_kernel, final_norm=final_norm),
        grid=(t // tb,),
        in_specs=[
            pl.BlockSpec((tb, d), row), pl.BlockSpec((tb, yc.shape[1]), row),
            pl.BlockSpec((1, N_MOD, d), lambda i: (i // tiles_per_batch, 0, 0)),
            _const_spec(w_out.shape), _const_spec(nf.shape),
            _const_spec(w1t.shape), _const_spec(w3t.shape), _const_spec(w2t.shape),
            _const_spec(final_g.shape),
        ],
        out_specs=pl.BlockSpec((tb, d), row),
        out_shape=jax.ShapeDtypeStruct((t, d), F32),
        scratch_shapes=[pltpu.VMEM((tb, d), F32)],
        compiler_params=_params(("arbitrary",)),
        name="block_odd",
    )(x2, yc, mod, w_out, nf, w1t, w3t, w2t, final_g)


def kernel(x, c, ada_w, ada_b, norm_mix_g, norm_ffn_g, ev_w_in, ev_w_out, s5_lam_re, s5_lam_im, s5_b_re, s5_b_im, s5_c_re, s5_c_im, s5_d, s5_log_step, s5_w_glu, s5_b_glu, hg_lb_logits, hg_norm_g, od_w_in, od_w_a1, od_w_a2, od_b_a, gla_norm_g, od_w_out, ffn_w1, ffn_w3, ffn_w2, final_norm_g):
    bsz, seq, d = x.shape
    depth = ada_w.shape[0]
    assert depth % 2 == 0, "the final norm is fused into the last (odd) layer"
    t = bsz * seq
    tiles_per_batch = seq // TOKEN_TILE
    x2 = x.reshape(t, d)
    mod = _adaln(c, ada_w, ada_b).reshape(depth, bsz, N_MOD, d)

    for layer in range(depth):
        nm = norm_mix_g[layer].reshape(1, d)
        nf = norm_ffn_g[layer].reshape(1, d)
        w1t, w3t, w2t = _ffn_weights(ffn_w1[layer], ffn_w3[layer], ffn_w2[layer])
        if layer % 2 == 0:
            e = layer // 2
            u, u2, hq, hk, hgl, hv, hgate = _inproj_even(
                x2, mod[layer], nm, ev_w_in[e].astype(BF16), hg_lb_logits, layer, tiles_per_batch)
            sw = u.shape[1]
            y2 = _s5_mix(u2, s5_lam_re[e], s5_lam_im[e], s5_b_re[e], s5_b_im[e], s5_c_re[e],
                         s5_c_im[e], s5_log_step[e], seq // S5_CHUNK)
            yb = _gated_recurrence(hq, hk, hgl, hv, hgate, hg_norm_g[e].reshape(1, -1), bsz, HG_HEADS)
            x2 = _block_even(x2, y2, u, yb, mod[layer], s5_d[e].reshape(1, sw),
                             s5_w_glu[e].astype(BF16), s5_b_glu[e].reshape(1, sw),
                             ev_w_out[e].astype(BF16), nf, w1t, w3t, w2t, tiles_per_batch)
        else:
            o = layer // 2
            kw = od_w_a2.shape[-1]
            vw = od_w_out.shape[1]
            rank = od_w_a1.shape[-1]
            w_a1 = jnp.pad(od_w_a1[o], ((0, 0), (0, LANES - rank))).astype(BF16)
            w_a2 = jnp.pad(od_w_a2[o], ((0, LANES - rank), (0, 0))).astype(BF16)
            q, k, g, v, gate = _inproj_odd(
                x2, mod[layer], nm, od_w_in[o].astype(BF16), w_a1, w_a2, od_b_a[o].reshape(1, kw),
                kw, vw, (kw // GLA_HEADS) ** -0.5, tiles_per_batch)
            yc = _gated_recurrence(q, k, g, v, gate, gla_norm_g[o].reshape(1, -1), bsz, GLA_HEADS)
            x2 = _block_odd(x2, yc, mod[layer], od_w_out[o].astype(BF16), nf, w1t, w3t, w2t,
                            final_norm_g.reshape(1, d), layer == depth - 1, tiles_per_batch)
    return x2.reshape(bsz, seq, d)
```

```python
import functools
import math

import numpy as np
import jax
import jax.numpy as jnp
from jax import lax
from jax.experimental import pallas as pl
from jax.experimental.pallas import tpu as pltpu

F32 = jnp.float32
BF16 = jnp.bfloat16
EPS = 1e-6

HG_HEADS = 4
GLA_HEADS = 4
GLA_GATE_NORM = 16.0
N_MOD = 6

VMEM_LIMIT_BYTES = 56 * 1024 * 1024
LANES = 128
GRANULE = 16
GRANULES = LANES // GRANULE

TOKEN_TILE = 1024
RECUR_TILE = 512
RECUR_CHUNK = 128
S5_CHUNK = 32
FFN_TILE = 256
PROJ_SUB_TILES = 1
BLOCK_SUB_TILES = 2


def _dot(a, b):
    return jnp.dot(a, b, preferred_element_type=F32)


def _dot_nt(a, b):
    return lax.dot_general(a, b, (((1,), (1,)), ((), ())), preferred_element_type=F32)


def _dot_tn(a, b):
    return lax.dot_general(a, b, (((0,), (0,)), ((), ())), preferred_element_type=F32)


def _silu(x):
    return x * jax.nn.sigmoid(x)


def _params(semantics):
    return pltpu.CompilerParams(dimension_semantics=semantics, vmem_limit_bytes=VMEM_LIMIT_BYTES)


def _const_spec(shape):
    nd = len(shape)
    return pl.BlockSpec(shape, lambda *_: (0,) * nd, pipeline_mode=pl.Buffered(1))


def _sub_tiles(n_rows, n_sub):
    step = n_rows // n_sub
    return [slice(i * step, (i + 1) * step) for i in range(n_sub)]


def _norm_modulate(x, norm_g, shift, scale):
    ms = jnp.mean(x * x, axis=-1, keepdims=True)
    h = x * lax.rsqrt(ms + EPS) * norm_g
    return h * (1.0 + scale) + shift


def _adaln_kernel(ct_ref, w_ref, b_ref, o_ref):
    cond = _silu(ct_ref[...])
    w = w_ref[0]
    for b in range(cond.shape[1]):
        o_ref[0, b:b + 1, :] = jnp.sum(cond[:, b:b + 1] * w, axis=0, keepdims=True) + b_ref[0]


def _adaln(c, ada_w, ada_b):
    depth, d, n = ada_w.shape
    bsz = c.shape[0]
    nt = 1024
    return pl.pallas_call(
        _adaln_kernel,
        grid=(depth, n // nt),
        in_specs=[
            pl.BlockSpec((d, bsz), lambda l, j: (0, 0)),
            pl.BlockSpec((1, d, nt), lambda l, j: (l, 0, j)),
            pl.BlockSpec((1, 1, nt), lambda l, j: (l, 0, j)),
        ],
        out_specs=pl.BlockSpec((1, bsz, nt), lambda l, j: (l, 0, j)),
        out_shape=jax.ShapeDtypeStruct((depth, bsz, n), F32),
        compiler_params=_params(("arbitrary", "arbitrary")),
        name="adaln_mod",
    )(c.T, ada_w, ada_b.reshape(depth, 1, n))


def _granule_transpose(blocks, slot):
    blocks = list(blocks)
    for d in (4, 2, 1):
        upper = (slot & d) != 0
        for a in range(GRANULES):
            if a & d:
                continue
            lo, hi = blocks[a], blocks[a + d]
            blocks[a] = jnp.where(upper, pltpu.roll(hi, GRANULE * d, 1), lo)
            blocks[a + d] = jnp.where(upper, hi, pltpu.roll(lo, LANES - GRANULE * d, 1))
    return blocks


def _to_chunk_rows(tok_ref, out_ref, tok_rows):
    n_grp, _, k = out_ref.shape
    chunk = k // GRANULE
    n_rows = (tok_rows.stop - tok_rows.start) // chunk
    out_rows = slice(tok_rows.start // chunk, tok_rows.stop // chunk)
    slot = lax.broadcasted_iota(jnp.int32, (n_rows, LANES), 1) // GRANULE
    for tb in range(chunk // GRANULES):
        for j in range(n_grp // GRANULES):
            rows = [tok_ref[j, pl.ds(tok_rows.start + GRANULES * tb + tp, n_rows, stride=chunk), :]
                    for tp in range(GRANULES)]
            for gp, blk in enumerate(_granule_transpose(rows, slot)):
                out_ref[GRANULES * j + gp, out_rows, tb * LANES:(tb + 1) * LANES] = (
                    blk.astype(out_ref.dtype))


def _to_token_rows(grp_ref, tok_ref, tok_rows):
    n_grp, _, k = grp_ref.shape
    chunk = k // GRANULE
    n_rows = (tok_rows.stop - tok_rows.start) // chunk
    in_rows = slice(tok_rows.start // chunk, tok_rows.stop // chunk)
    slot = lax.broadcasted_iota(jnp.int32, (n_rows, LANES), 1) // GRANULE
    for tb in range(chunk // GRANULES):
        for j in range(n_grp // GRANULES):
            cols = [grp_ref[GRANULES * j + gp, in_rows, tb * LANES:(tb + 1) * LANES].astype(F32)
                    for gp in range(GRANULES)]
            for tp, blk in enumerate(_granule_transpose(cols, slot)):
                tok_ref[j, pl.ds(tok_rows.start + GRANULES * tb + tp, n_rows, stride=chunk), :] = blk


def _inproj_even_kernel(x_ref, mod_ref, ng_ref, w_ref, lbl_ref,
                        u_ref, u2_ref, q_ref, k_ref, g_ref, v_ref, gt_ref, tok_ref, *, lb_row):
    w = u_ref.shape[1]
    lg = lbl_ref[...]
    e = jnp.exp(lg - jnp.max(lg, axis=0, keepdims=True))
    lb = jnp.sum(e[:lb_row + 1], axis=0, keepdims=True) / jnp.sum(e, axis=0, keepdims=True)
    for rows in _sub_tiles(x_ref.shape[0], PROJ_SUB_TILES):
        h = _norm_modulate(x_ref[rows, :], ng_ref[...], mod_ref[0, 0:1, :], mod_ref[0, 1:2, :])
        hb = h.astype(BF16)
        z = [_dot(hb, w_ref[:, p * w:(p + 1) * w]) for p in range(5)]
        for j in range(tok_ref.shape[0]):
            tok_ref[j, rows, :] = z[0][:, j * LANES:(j + 1) * LANES]
        _to_chunk_rows(tok_ref, u2_ref, rows)
        u_ref[rows, :] = z[0].astype(BF16)
        q_ref[rows, :] = _silu(z[1]).astype(BF16)
        f = z[2]
        k_ref[rows, :] = ((1.0 - lb) * jax.nn.sigmoid(-f)).astype(BF16)
        g_ref[rows, :] = jnp.log(lb + (1.0 - lb) * jax.nn.sigmoid(f))
        v_ref[rows, :] = z[3].astype(BF16)
        gt_ref[rows, :] = _silu(z[4]).astype(BF16)


def _inproj_even(x2, mod, norm_g, w_in, lb_logits, lb_row, tiles_per_batch):
    t, d = x2.shape
    w = w_in.shape[1] // 5
    tb = TOKEN_TILE
    row = lambda i: (i, 0)
    out_bf = jax.ShapeDtypeStruct((t, w), BF16)
    groups = w // GRANULE
    k = S5_CHUNK * GRANULE
    tok = pl.BlockSpec((tb, w), row)
    return pl.pallas_call(
        functools.partial(_inproj_even_kernel, lb_row=lb_row),
        grid=(t // tb,),
        in_specs=[
            pl.BlockSpec((tb, d), row),
            pl.BlockSpec((1, N_MOD, d), lambda i: (i // tiles_per_batch, 0, 0)),
            _const_spec((1, d)),
            _const_spec(w_in.shape),
            _const_spec(lb_logits.shape),
        ],
        out_specs=[tok, pl.BlockSpec((groups, tb // S5_CHUNK, k), lambda i: (0, i, 0)),
                   tok, tok, tok, tok, tok],
        out_shape=[out_bf, jax.ShapeDtypeStruct((groups, t // S5_CHUNK, k), BF16),
                   out_bf, out_bf, jax.ShapeDtypeStruct((t, w), F32), out_bf, out_bf],
        scratch_shapes=[pltpu.VMEM((w // LANES, tb, LANES), F32)],
        compiler_params=_params(("arbitrary",)),
        name="inproj_even",
    )(x2, mod, norm_g, w_in, lb_logits)


def _inproj_odd_kernel(x_ref, mod_ref, ng_ref, w_ref, wa1_ref, wa2_ref, ba_ref,
                       q_ref, k_ref, g_ref, v_ref, gt_ref, *, q_scale):
    kw = q_ref.shape[1]
    vw = v_ref.shape[1]
    for rows in _sub_tiles(x_ref.shape[0], PROJ_SUB_TILES):
        h = _norm_modulate(x_ref[rows, :], ng_ref[...], mod_ref[0, 0:1, :], mod_ref[0, 1:2, :])
        hb = h.astype(BF16)
        q_ref[rows, :] = (_dot(hb, w_ref[:, 0:kw]) * q_scale).astype(BF16)
        k_ref[rows, :] = _dot(hb, w_ref[:, kw:2 * kw]).astype(BF16)
        v_ref[rows, :] = _dot(hb, w_ref[:, 2 * kw:2 * kw + vw]).astype(BF16)
        gt_ref[rows, :] = _silu(_dot(hb, w_ref[:, 2 * kw + vw:2 * kw + 2 * vw])).astype(BF16)
        a1 = _dot(hb, wa1_ref[...]).astype(BF16)
        za = _dot(a1, wa2_ref[...]) + ba_ref[...]
        log_sig = jnp.minimum(za, 0.0) - jnp.log(1.0 + jnp.exp(-jnp.abs(za)))
        g_ref[rows, :] = log_sig * (1.0 / GLA_GATE_NORM)


def _inproj_odd(x2, mod, norm_g, w_in, w_a1, w_a2, b_a, kw, vw, q_scale, tiles_per_batch):
    t, d = x2.shape
    tb = TOKEN_TILE
    row = lambda i: (i, 0)
    return pl.pallas_call(
        functools.partial(_inproj_odd_kernel, q_scale=q_scale),
        grid=(t // tb,),
        in_specs=[
            pl.BlockSpec((tb, d), row),
            pl.BlockSpec((1, N_MOD, d), lambda i: (i // tiles_per_batch, 0, 0)),
            _const_spec((1, d)),
            _const_spec(w_in.shape),
            _const_spec(w_a1.shape),
            _const_spec(w_a2.shape),
            _const_spec(b_a.shape),
        ],
        out_specs=[pl.BlockSpec((tb, kw), row), pl.BlockSpec((tb, kw), row), pl.BlockSpec((tb, kw), row),
                   pl.BlockSpec((tb, vw), row), pl.BlockSpec((tb, vw), row)],
        out_shape=[jax.ShapeDtypeStruct((t, kw), BF16), jax.ShapeDtypeStruct((t, kw), BF16),
                   jax.ShapeDtypeStruct((t, kw), F32), jax.ShapeDtypeStruct((t, vw), BF16),
                   jax.ShapeDtypeStruct((t, vw), BF16)],
        compiler_params=_params(("arbitrary",)),
        name="inproj_odd",
    )(x2, mod, norm_g, w_in, w_a1, w_a2, b_a)


def _s5_kernel(u_ref, mt_ref, wsr_ref, wsi_ref, wir_ref, wii_ref, sr_ref, si_ref, y_ref,
               *, chunks_per_seq):
    u = u_ref[0]
    v_re = _dot(u, wsr_ref[0])
    v_im = _dot(u, wsi_ref[0])
    n_in_seq = lax.broadcasted_iota(jnp.int32, v_re.shape, 0) & (chunks_per_seq - 1)
    for lv in range(sr_ref.shape[1]):
        step = 1 << lv
        keep = n_in_seq >= step
        s_re = jnp.where(keep, pltpu.roll(v_re, step, 0), 0.0)
        s_im = jnp.where(keep, pltpu.roll(v_im, step, 0), 0.0)
        a_re, a_im = sr_ref[0, lv:lv + 1, :], si_ref[0, lv:lv + 1, :]
        v_re, v_im = v_re + a_re * s_re - a_im * s_im, v_im + a_re * s_im + a_im * s_re
    keep = n_in_seq >= 1
    x_re = jnp.where(keep, pltpu.roll(v_re, 1, 0), 0.0).astype(BF16)
    x_im = jnp.where(keep, pltpu.roll(v_im, 1, 0), 0.0).astype(BF16)
    y = _dot(u, mt_ref[0]) + _dot_nt(x_re, wir_ref[0]) + _dot_nt(x_im, wii_ref[0])
    y_ref[0] = y.astype(BF16)


def _s5_prep_kernel(bt_re_ref, bt_im_ref, c_re_ref, c_im_ref, lp_re_ref, lp_im_ref,
                    mt_ref, wsr_ref, wsi_ref, wir_ref, wii_ref, toep_ref, *, chunk):
    hg, p = bt_re_ref.shape[1:]
    k = chunk * hg
    t = lax.broadcasted_iota(jnp.int32, (k, p), 0) // hg

    def lam_pow(e):
        re = jnp.ones((k, p), F32)
        im = jnp.zeros((k, p), F32)
        for b in range(lp_re_ref.shape[1]):
            b_re, b_im = lp_re_ref[0, b:b + 1, :], lp_im_ref[0, b:b + 1, :]
            bit = ((e >> b) & 1) == 1
            re, im = (jnp.where(bit, re * b_re - im * b_im, re),
                      jnp.where(bit, re * b_im + im * b_re, im))
        return re, im

    tile = lambda x: jnp.concatenate([x] * chunk, axis=0)
    b_re, b_im = tile(bt_re_ref[0]), tile(bt_im_ref[0])
    c_re, c_im = tile(c_re_ref[0]), tile(c_im_ref[0])

    p_re, p_im = lam_pow(t)
    z_re, z_im = c_re * p_re - c_im * p_im, c_re * p_im + c_im * p_re
    hp = lax.Precision.HIGHEST
    nt = (((1,), (1,)), ((), ()))
    kt = (lax.dot_general(bt_re_ref[0], z_re, nt, precision=hp, preferred_element_type=F32)
          - lax.dot_general(bt_im_ref[0], z_im, nt, precision=hp, preferred_element_type=F32))
    col_t = lax.broadcasted_iota(jnp.int32, (chunk, k), 1) // hg
    row_s = lax.broadcasted_iota(jnp.int32, (chunk, k), 0)
    for hi in range(hg):
        rows = jnp.broadcast_to(kt[hi:hi + 1, :], (chunk, k))
        rows = pltpu.roll(rows, 0, 1, stride=hg, stride_axis=0)
        rows = jnp.where(col_t >= row_s, rows, 0.0)
        for j in range(k // LANES):
            toep_ref[j, pl.ds(hi, chunk, stride=hg), :] = rows[:, j * LANES:(j + 1) * LANES]
    for j in range(k // LANES):
        mt_ref[0, :, j * LANES:(j + 1) * LANES] = toep_ref[j].astype(BF16)
    p_re, p_im = lam_pow(chunk - 1 - t)
    wsr_ref[0] = (b_re * p_re - b_im * p_im).astype(BF16)
    wsi_ref[0] = (b_re * p_im + b_im * p_re).astype(BF16)
    l_re, l_im = lp_re_ref[0, 0:1, :], lp_im_ref[0, 0:1, :]
    wir_ref[0] = (z_re * l_re - z_im * l_im).astype(BF16)
    wii_ref[0] = (-(z_re * l_im + z_im * l_re)).astype(BF16)


def _s5_discretise(lam_re, lam_im, b_re, b_im, log_step, chunk, chunks_per_seq):
    delta = jnp.exp(log_step)[:, None]
    ld_re, ld_im = lam_re * delta, lam_im * delta

    def powers(exponents):
        e = jnp.asarray(exponents, F32)[None, :, None]
        mag = jnp.exp(ld_re[:, None, :] * e)
        ang = ld_im[:, None, :] * e
        return mag * jnp.cos(ang), mag * jnp.sin(ang)

    lp_re, lp_im = powers([1 << b for b in range(chunk.bit_length())])
    nr, ni = lp_re[:, 0] - 1.0, lp_im[:, 0]
    den = lam_re * lam_re + lam_im * lam_im
    s_re = (nr * lam_re + ni * lam_im) / den
    s_im = (ni * lam_re - nr * lam_im) / den
    bt_re = (s_re[:, :, None] * b_re - s_im[:, :, None] * b_im).transpose(0, 2, 1)
    bt_im = (s_re[:, :, None] * b_im + s_im[:, :, None] * b_re).transpose(0, 2, 1)
    n_lev = int(math.log2(chunks_per_seq))
    sc_re, sc_im = powers([chunk * (1 << lv) for lv in range(n_lev)])
    return bt_re, bt_im, lp_re, lp_im, sc_re, sc_im


def _s5_mix(u2, lam_re, lam_im, b_re, b_im, c_re, c_im, log_step, chunks_per_seq):
    g, r, k = u2.shape
    chunk = k // GRANULE
    p = lam_re.shape[1]
    bt_re, bt_im, lp_re, lp_im, sc_re, sc_im = _s5_discretise(
        lam_re, lam_im, b_re, b_im, log_step, chunk, chunks_per_seq)
    grp = lambda i: (i, 0, 0)
    small = lambda a: pl.BlockSpec((1,) + a.shape[1:], grp)
    kp = jax.ShapeDtypeStruct((g, k, p), BF16)
    ops = pl.pallas_call(
        functools.partial(_s5_prep_kernel, chunk=chunk),
        grid=(g,),
        in_specs=[small(bt_re), small(bt_im), small(c_re), small(c_im), small(lp_re), small(lp_im)],
        out_specs=[pl.BlockSpec((1, k, k), grp)] + [pl.BlockSpec((1, k, p), grp)] * 4,
        out_shape=[jax.ShapeDtypeStruct((g, k, k), BF16), kp, kp, kp, kp],
        scratch_shapes=[pltpu.VMEM((k // LANES, k, LANES), F32)],
        compiler_params=_params(("arbitrary",)),
        name="s5_prep",
    )(bt_re, bt_im, c_re, c_im, lp_re, lp_im)
    return pl.pallas_call(
        functools.partial(_s5_kernel, chunks_per_seq=chunks_per_seq),
        grid=(g,),
        in_specs=[pl.BlockSpec((1, r, k), grp), pl.BlockSpec((1, k, k), grp)]
                 + [pl.BlockSpec((1, k, p), grp)] * 4 + [small(sc_re), small(sc_im)],
        out_specs=pl.BlockSpec((1, r, k), grp),
        out_shape=jax.ShapeDtypeStruct((g, r, k), BF16),
        compiler_params=_params(("arbitrary",)),
        name="s5_mix",
    )(u2, *ops, sc_re, sc_im)


def _level_index(chunk):
    i = np.arange(chunk)[:, None]
    j = np.arange(chunk)[None, :]
    x = np.bitwise_xor(i, j)
    lvl = np.floor(np.log2(np.maximum(x, 1))).astype(np.int32)
    return np.where(i > j, lvl, np.where(i == j, -1, -2)).astype(np.int32)


def _recur_kernel(lvl_ref, q_ref, k_ref, g_ref, v_ref, gt_ref, ng_ref, y_ref, st_ref,
                  *, chunk, heads, dk, dv):
    @pl.when(pl.program_id(1) == 0)
    def _():
        st_ref[...] = jnp.zeros(st_ref.shape, F32)

    n_chunks = q_ref.shape[0] // chunk
    n_levels = int(math.log2(chunk))
    lvl = lvl_ref[...]
    row = lax.broadcasted_iota(jnp.int32, (chunk, dk), 0)
    ng = ng_ref[...]

    def one_chunk(c, carry):
        r0 = pl.multiple_of(c * chunk, chunk)
        rows = pl.ds(r0, chunk)
        in_second = [(row & (1 << lv)) != 0 for lv in range(n_levels)]
        at_level = [lvl == lv for lv in range(n_levels)]
        for hd in range(heads):
            ks = slice(hd * dk, (hd + 1) * dk)
            vs = slice(hd * dv, (hd + 1) * dv)
            qb = q_ref[rows, ks]
            kb = k_ref[rows, ks]
            vb = v_ref[rows, vs]
            q = qb.astype(F32)
            k = kb.astype(F32)
            pre = g_ref[rows, ks]
            tot = pre
            scores = jnp.where(lvl == -1, _dot_nt(qb, kb), 0.0)
            for lv in range(n_levels):
                h = 1 << lv
                second = in_second[lv]
                decay = jnp.exp(jnp.where(second, pre, tot - pre))
                w = (jnp.where(second, q, k) * decay).astype(BF16)
                scores = jnp.where(at_level[lv], _dot_nt(w, w), scores)
                below = pltpu.roll(tot, h, 0)
                above = pltpu.roll(tot, chunk - h, 0)
                pre = pre + jnp.where(second, below, 0.0)
                tot = tot + jnp.where(second, below, above)
            st = st_ref[hd]
            o = _dot(scores.astype(BF16), vb)
            o = o + _dot_nt((q * jnp.exp(pre)).astype(BF16), st.astype(BF16))
            kd = (k * jnp.exp(tot - pre)).astype(BF16)
            st_ref[hd] = st * jnp.exp(tot[0:1, :]) + _dot_tn(vb, kd)
            ms = jnp.mean(o * o, axis=-1, keepdims=True)
            o = o * lax.rsqrt(ms + EPS) * ng * gt_ref[rows, vs].astype(F32)
            y_ref[rows, vs] = o.astype(BF16)
        return carry

    lax.fori_loop(0, n_chunks, one_chunk, 0)


def _gated_recurrence(q, k, g, v, gate, norm_g, bsz, heads):
    t, kw = q.shape
    vw = v.shape[1]
    dk, dv = kw // heads, vw // heads
    tt = RECUR_TILE
    per_seq = (t // bsz) // tt
    row = lambda b, i: (b * per_seq + i, 0)
    lvl = jnp.asarray(_level_index(RECUR_CHUNK))
    return pl.pallas_call(
        functools.partial(_recur_kernel, chunk=RECUR_CHUNK, heads=heads, dk=dk, dv=dv),
        grid=(bsz, per_seq),
        in_specs=[
            pl.BlockSpec(lvl.shape, lambda b, i: (0, 0)),
            pl.BlockSpec((tt, kw), row), pl.BlockSpec((tt, kw), row), pl.BlockSpec((tt, kw), row),
            pl.BlockSpec((tt, vw), row), pl.BlockSpec((tt, vw), row),
            pl.BlockSpec((1, dv), lambda b, i: (0, 0)),
        ],
        out_specs=pl.BlockSpec((tt, vw), row),
        out_shape=jax.ShapeDtypeStruct((t, vw), BF16),
        scratch_shapes=[pltpu.VMEM((heads, dv, dk), F32)],
        compiler_params=_params(("arbitrary", "arbitrary")),
        name="gated_recurrence",
    )(lvl, q, k, g, v, gate, norm_g)


def _gelu_tanh(x):
    return 0.5 * x * (1.0 + jnp.tanh(math.sqrt(2.0 / math.pi) * (x + 0.044715 * (x * x * x))))


def _ffn_tail(x, mod_ref, nf_ref, w1_ref, w3_ref, w2_ref):
    h = _norm_modulate(x, nf_ref[...], mod_ref[0, 3:4, :], mod_ref[0, 4:5, :]).astype(BF16)
    acc = None
    for j in range(w1_ref.shape[0]):
        a = _dot(h, w1_ref[j])
        b = _dot(h, w3_ref[j])
        part = _dot((_silu(a) * b).astype(BF16), w2_ref[j])
        acc = part if acc is None else acc + part
    return x + mod_ref[0, 5:6, :] * acc


def _block_even_kernel(x_ref, y2_ref, u_ref, yb_ref, mod_ref, sd_ref, wg_ref, bg_ref, wo_ref,
                       nf_ref, w1_ref, w3_ref, w2_ref, o_ref, tok_ref):
    for rows in _sub_tiles(x_ref.shape[0], BLOCK_SUB_TILES):
        _to_token_rows(y2_ref, tok_ref, rows)
        y = jnp.concatenate([tok_ref[j, rows, :] for j in range(tok_ref.shape[0])], axis=1)
        y = y + sd_ref[...] * u_ref[rows, :].astype(F32)
        y = _gelu_tanh(y)
        ya = y * jax.nn.sigmoid(_dot(y.astype(BF16), wg_ref[...]) + bg_ref[...])
        sw = ya.shape[1]
        mixed = _dot(ya.astype(BF16), wo_ref[0:sw, :]) + _dot(yb_ref[rows, :], wo_ref[sw:, :])
        x = x_ref[rows, :] + mod_ref[0, 2:3, :] * mixed
        o_ref[rows, :] = _ffn_tail(x, mod_ref, nf_ref, w1_ref, w3_ref, w2_ref)


def _block_odd_kernel(x_ref, yc_ref, mod_ref, wo_ref, nf_ref, w1_ref, w3_ref, w2_ref, fg_ref,
                      o_ref, *, final_norm):
    for rows in _sub_tiles(x_ref.shape[0], BLOCK_SUB_TILES):
        x = x_ref[rows, :] + mod_ref[0, 2:3, :] * _dot(yc_ref[rows, :], wo_ref[...])
        x = _ffn_tail(x, mod_ref, nf_ref, w1_ref, w3_ref, w2_ref)
        if final_norm:
            ms = jnp.mean(x * x, axis=-1, keepdims=True)
            x = x * lax.rsqrt(ms + EPS) * fg_ref[...]
        o_ref[rows, :] = x


def _ffn_weights(w1, w3, w2):
    d, hid = w1.shape
    nj = hid // FFN_TILE
    w1t = w1.astype(BF16).reshape(d, nj, FFN_TILE).transpose(1, 0, 2)
    w3t = w3.astype(BF16).reshape(d, nj, FFN_TILE).transpose(1, 0, 2)
    w2t = w2.astype(BF16).reshape(nj, FFN_TILE, d)
    return w1t, w3t, w2t


def _block_even(x2, y2, u, yb, mod, s5_d, w_glu, b_glu, w_out, nf, w1t, w3t, w2t, tiles_per_batch):
    t, d = x2.shape
    tb = TOKEN_TILE
    row = lambda i: (i, 0)
    sw = u.shape[1]
    groups, _, k = y2.shape
    return pl.pallas_call(
        _block_even_kernel,
        grid=(t // tb,),
        in_specs=[
            pl.BlockSpec((tb, d), row),
            pl.BlockSpec((groups, tb * GRANULE // k, k), lambda i: (0, i, 0)),
            pl.BlockSpec((tb, sw), row),
            pl.BlockSpec((tb, yb.shape[1]), row),
            pl.BlockSpec((1, N_MOD, d), lambda i: (i // tiles_per_batch, 0, 0)),
            _const_spec(s5_d.shape), _const_spec(w_glu.shape), _const_spec(b_glu.shape),
            _const_spec(w_out.shape), _const_spec(nf.shape),
            _const_spec(w1t.shape), _const_spec(w3t.shape), _const_spec(w2t.shape),
        ],
        out_specs=pl.BlockSpec((tb, d), row),
        out_shape=jax.ShapeDtypeStruct((t, d), F32),
        scratch_shapes=[pltpu.VMEM((sw // LANES, tb, LANES), F32)],
        compiler_params=_params(("arbitrary",)),
        name="block_even",
    )(x2, y2, u, yb, mod, s5_d, w_glu, b_glu, w_out, nf, w1t, w3t, w2t)


def _block_odd(x2, yc, mod, w_out, nf, w1t, w3t, w2t, final_g, final_norm, tiles_per_batch):
    t, d = x2.shape
    tb = TOKEN_TILE
    row = lambda i: (i, 0)
    return pl.pallas_call(
        functools.partial(_block_odd_kernel, final_norm=final_norm),
        grid=(t // tb,),
        in_specs=[
            pl.BlockSpec((tb, d), row), pl.BlockSpec((tb, yc.shape[1]), row),
            pl.BlockSpec((1, N_MOD, d), lambda i: (i // tiles_per_batch, 0, 0)),
            _const_spec(w_out.shape), _const_spec(nf.shape),
            _const_spec(w1t.shape), _const_spec(w3t.shape), _const_spec(w2t.shape),
            _const_spec(final_g.shape),
        ],
        out_specs=pl.BlockSpec((tb, d), row),
        out_shape=jax.ShapeDtypeStruct((t, d), F32),
        compiler_params=_params(("arbitrary",)),
        name="block_odd",
    )(x2, yc, mod, w_out, nf, w1t, w3t, w2t, final_g)


def kernel(x, c, ada_w, ada_b, norm_mix_g, norm_ffn_g, ev_w_in, ev_w_out, s5_lam_re, s5_lam_im, s5_b_re, s5_b_im, s5_c_re, s5_c_im, s5_d, s5_log_step, s5_w_glu, s5_b_glu, hg_lb_logits, hg_norm_g, od_w_in, od_w_a1, od_w_a2, od_b_a, gla_norm_g, od_w_out, ffn_w1, ffn_w3, ffn_w2, final_norm_g):
    bsz, seq, d = x.shape
    depth = ada_w.shape[0]
    assert depth % 2 == 0, "the final norm is fused into the last (odd) layer"
    t = bsz * seq
    tiles_per_batch = seq // TOKEN_TILE
    x2 = x.reshape(t, d)
    mod = _adaln(c, ada_w, ada_b).reshape(depth, bsz, N_MOD, d)

    for layer in range(depth):
        nm = norm_mix_g[layer].reshape(1, d)
        nf = norm_ffn_g[layer].reshape(1, d)
        w1t, w3t, w2t = _ffn_weights(ffn_w1[layer], ffn_w3[layer], ffn_w2[layer])
        if layer % 2 == 0:
            e = layer // 2
            u, u2, hq, hk, hgl, hv, hgate = _inproj_even(
                x2, mod[layer], nm, ev_w_in[e].astype(BF16), hg_lb_logits, layer, tiles_per_batch)
            sw = u.shape[1]
            y2 = _s5_mix(u2, s5_lam_re[e], s5_lam_im[e], s5_b_re[e], s5_b_im[e], s5_c_re[e],
                         s5_c_im[e], s5_log_step[e], seq // S5_CHUNK)
            yb = _gated_recurrence(hq, hk, hgl, hv, hgate, hg_norm_g[e].reshape(1, -1), bsz, HG_HEADS)
            x2 = _block_even(x2, y2, u, yb, mod[layer], s5_d[e].reshape(1, sw),
                             s5_w_glu[e].astype(BF16), s5_b_glu[e].reshape(1, sw),
                             ev_w_out[e].astype(BF16), nf, w1t, w3t, w2t, tiles_per_batch)
        else:
            o = layer // 2
            kw = od_w_a2.shape[-1]
            vw = od_w_out.shape[1]
            rank = od_w_a1.shape[-1]
            w_a1 = jnp.pad(od_w_a1[o], ((0, 0), (0, LANES - rank))).astype(BF16)
            w_a2 = jnp.pad(od_w_a2[o], ((0, LANES - rank), (0, 0))).astype(BF16)
            q, k, g, v, gate = _inproj_odd(
                x2, mod[layer], nm, od_w_in[o].astype(BF16), w_a1, w_a2, od_b_a[o].reshape(1, kw),
                kw, vw, (kw // GLA_HEADS) ** -0.5, tiles_per_batch)
            yc = _gated_recurrence(q, k, g, v, gate, gla_norm_g[o].reshape(1, -1), bsz, GLA_HEADS)
            x2 = _block_odd(x2, yc, mod[layer], od_w_out[o].astype(BF16), nf, w1t, w3t, w2t,
                            final_norm_g.reshape(1, d), layer == depth - 1, tiles_per_batch)
    return x2.reshape(bsz, seq, d)
```

```python
import functools
import math

import numpy as np
import jax
import jax.numpy as jnp
from jax import lax
from jax.experimental import pallas as pl
from jax.experimental.pallas import tpu as pltpu

F32 = jnp.float32
BF16 = jnp.bfloat16
EPS = 1e-6
LOG2E = math.log2(math.e)

HG_HEADS = 4
GLA_HEADS = 4
GLA_GATE_NORM = 16.0
N_MOD = 6

VMEM_LIMIT_BYTES = 56 * 1024 * 1024
LANES = 128
SUBLANES = 8
GRANULE = 16
GRANULES = LANES // GRANULE
RELAYOUT_ROWS = 32

TOKEN_TILE = 1024
RECUR_TILE = 512
RECUR_CHUNK = 128
S5_CHUNK = 32
FFN_TILE = 256
BLOCK_SUB_TILES = 2


def _dot(a, b):
    return jnp.dot(a, b, preferred_element_type=F32)


def _dot_nt(a, b):
    return lax.dot_general(a, b, (((1,), (1,)), ((), ())), preferred_element_type=F32)


def _dot_tn(a, b):
    return lax.dot_general(a, b, (((0,), (0,)), ((), ())), preferred_element_type=F32)


def _silu(x):
    return x * jax.nn.sigmoid(x)


def _params(semantics):
    return pltpu.CompilerParams(dimension_semantics=semantics, vmem_limit_bytes=VMEM_LIMIT_BYTES)


def _const_spec(shape):
    nd = len(shape)
    return pl.BlockSpec(shape, lambda *_: (0,) * nd, pipeline_mode=pl.Buffered(1))


def _sub_tiles(n_rows, n_sub):
    step = n_rows // n_sub
    return [slice(i * step, (i + 1) * step) for i in range(n_sub)]


def _prefix_matrix(chunk):
    tri = np.tril(np.ones((chunk, chunk), np.float32))
    return np.concatenate([tri, tri, tri], axis=1)


def _chunk_cumsum(g, tri3):
    chunk = tri3.shape[0]
    hi = g.astype(BF16)
    r1 = g - hi.astype(F32)
    mid = r1.astype(BF16)
    lo = (r1 - mid.astype(F32)).astype(BF16)
    out = []
    for c in range(g.shape[0] // chunk):
        rows = slice(c * chunk, (c + 1) * chunk)
        out.append(_dot(tri3, jnp.concatenate([hi[rows], mid[rows], lo[rows]], axis=0)))
    return jnp.concatenate(out, axis=0)


def _norm_modulate(x, norm_g, shift, scale):
    ms = jnp.mean(x * x, axis=-1, keepdims=True)
    h = x * lax.rsqrt(ms + EPS) * norm_g
    return h * (1.0 + scale) + shift


def _adaln_kernel(ct_ref, w_ref, b_ref, o_ref):
    cond = _silu(ct_ref[...])
    w = w_ref[0]
    for b in range(cond.shape[1]):
        o_ref[0, b:b + 1, :] = jnp.sum(cond[:, b:b + 1] * w, axis=0, keepdims=True) + b_ref[0]


def _adaln(c, ada_w, ada_b):
    depth, d, n = ada_w.shape
    bsz = c.shape[0]
    nt = 1024
    return pl.pallas_call(
        _adaln_kernel,
        grid=(depth, n // nt),
        in_specs=[
            pl.BlockSpec((d, bsz), lambda l, j: (0, 0)),
            pl.BlockSpec((1, d, nt), lambda l, j: (l, 0, j)),
            pl.BlockSpec((1, 1, nt), lambda l, j: (l, 0, j)),
        ],
        out_specs=pl.BlockSpec((1, bsz, nt), lambda l, j: (l, 0, j)),
        out_shape=jax.ShapeDtypeStruct((depth, bsz, n), F32),
        compiler_params=_params(("arbitrary", "arbitrary")),
        name="adaln_mod",
    )(c.T, ada_w, ada_b.reshape(depth, 1, n))


def _granule_transpose(blocks, slot):
    blocks = list(blocks)
    for d in (4, 2, 1):
        upper = (slot & d) != 0
        for a in range(GRANULES):
            if a & d:
                continue
            lo, hi = blocks[a], blocks[a + d]
            blocks[a] = jnp.where(upper, pltpu.roll(hi, GRANULE * d, 1), lo)
            blocks[a + d] = jnp.where(upper, hi, pltpu.roll(lo, LANES - GRANULE * d, 1))
    return blocks


def _to_chunk_rows(tok_ref, out_ref, tok_rows):
    n_grp, _, k = out_ref.shape
    chunk = k // GRANULE
    n_rows = min(RELAYOUT_ROWS, (tok_rows.stop - tok_rows.start) // chunk)
    slot = lax.broadcasted_iota(jnp.int32, (n_rows, LANES), 1) // GRANULE
    for r0 in range(tok_rows.start // chunk, tok_rows.stop // chunk, n_rows):
        for tb in range(chunk // GRANULES):
            for j in range(n_grp // GRANULES):
                rows = [tok_ref[j, pl.ds(r0 * chunk + GRANULES * tb + tp, n_rows, stride=chunk), :]
                        for tp in range(GRANULES)]
                for gp, blk in enumerate(_granule_transpose(rows, slot)):
                    out_ref[GRANULES * j + gp, r0:r0 + n_rows, tb * LANES:(tb + 1) * LANES] = (
                        blk.astype(out_ref.dtype))


def _to_token_rows(grp_ref, tok_ref, tok_rows):
    n_grp, _, k = grp_ref.shape
    chunk = k // GRANULE
    n_rows = min(RELAYOUT_ROWS, (tok_rows.stop - tok_rows.start) // chunk)
    slot = lax.broadcasted_iota(jnp.int32, (n_rows, LANES), 1) // GRANULE
    for r0 in range(tok_rows.start // chunk, tok_rows.stop // chunk, n_rows):
        for tb in range(chunk // GRANULES):
            for j in range(n_grp // GRANULES):
                cols = [grp_ref[GRANULES * j + gp, r0:r0 + n_rows, tb * LANES:(tb + 1) * LANES].astype(F32)
                        for gp in range(GRANULES)]
                for tp, blk in enumerate(_granule_transpose(cols, slot)):
                    tok_ref[j, pl.ds(r0 * chunk + GRANULES * tb + tp, n_rows, stride=chunk), :] = blk


def _inproj_even_kernel(x_ref, mod_ref, ng_ref, w_ref, lbl_ref, tri_ref,
                        u_ref, u2_ref, q_ref, k_ref, g_ref, v_ref, gt_ref, tok_ref, *, lb_row):
    w = u_ref.shape[1]
    lg = lbl_ref[...]
    e = jnp.exp(lg - jnp.max(lg, axis=0, keepdims=True))
    lb = jnp.sum(e[:lb_row + 1], axis=0, keepdims=True) / jnp.sum(e, axis=0, keepdims=True)
    h = _norm_modulate(x_ref[...], ng_ref[...], mod_ref[0, 0:1, :], mod_ref[0, 1:2, :])
    hb = h.astype(BF16)
    part = lambda p: _dot(hb, w_ref[:, p * w:(p + 1) * w])
    rows = slice(0, x_ref.shape[0])
    z = part(0)
    for j in range(tok_ref.shape[0]):
        tok_ref[j] = z[:, j * LANES:(j + 1) * LANES]
    _to_chunk_rows(tok_ref, u2_ref, rows)
    u_ref[...] = z.astype(BF16)
    q_ref[...] = _silu(part(1)).astype(BF16)
    f = part(2)
    k_ref[...] = ((1.0 - lb) * jax.nn.sigmoid(-f)).astype(BF16)
    g_ref[...] = _chunk_cumsum(jnp.log(lb + (1.0 - lb) * jax.nn.sigmoid(f)) * LOG2E, tri_ref[...])
    v_ref[...] = part(3).astype(BF16)
    gt_ref[...] = _silu(part(4)).astype(BF16)


def _inproj_even(x2, mod, norm_g, w_in, lb_logits, lb_row, tiles_per_batch):
    t, d = x2.shape
    w = w_in.shape[1] // 5
    tb = TOKEN_TILE
    row = lambda i: (i, 0)
    out_bf = jax.ShapeDtypeStruct((t, w), BF16)
    groups = w // GRANULE
    k = S5_CHUNK * GRANULE
    tok = pl.BlockSpec((tb, w), row)
    tri3 = jnp.asarray(_prefix_matrix(RECUR_CHUNK), BF16)
    return pl.pallas_call(
        functools.partial(_inproj_even_kernel, lb_row=lb_row),
        grid=(t // tb,),
        in_specs=[
            pl.BlockSpec((tb, d), row),
            pl.BlockSpec((1, N_MOD, d), lambda i: (i // tiles_per_batch, 0, 0)),
            _const_spec((1, d)),
            _const_spec(w_in.shape),
            _const_spec(lb_logits.shape),
            _const_spec(tri3.shape),
        ],
        out_specs=[tok, pl.BlockSpec((groups, tb // S5_CHUNK, k), lambda i: (0, i, 0)),
                   tok, tok, tok, tok, tok],
        out_shape=[out_bf, jax.ShapeDtypeStruct((groups, t // S5_CHUNK, k), BF16),
                   out_bf, out_bf, jax.ShapeDtypeStruct((t, w), F32), out_bf, out_bf],
        scratch_shapes=[pltpu.VMEM((w // LANES, tb, LANES), F32)],
        compiler_params=_params(("arbitrary",)),
        name="inproj_even",
    )(x2, mod, norm_g, w_in, lb_logits, tri3)


def _inproj_odd_kernel(x_ref, mod_ref, ng_ref, w_ref, wa1_ref, wa2_ref, ba_ref, tri_ref,
                       q_ref, k_ref, g_ref, v_ref, gt_ref, *, q_scale):
    kw = q_ref.shape[1]
    vw = v_ref.shape[1]
    h = _norm_modulate(x_ref[...], ng_ref[...], mod_ref[0, 0:1, :], mod_ref[0, 1:2, :])
    hb = h.astype(BF16)
    q_ref[...] = (_dot(hb, w_ref[:, 0:kw]) * q_scale).astype(BF16)
    k_ref[...] = _dot(hb, w_ref[:, kw:2 * kw]).astype(BF16)
    v_ref[...] = _dot(hb, w_ref[:, 2 * kw:2 * kw + vw]).astype(BF16)
    gt_ref[...] = _silu(_dot(hb, w_ref[:, 2 * kw + vw:2 * kw + 2 * vw])).astype(BF16)
    a1 = _dot(hb, wa1_ref[...]).astype(BF16)
    za = _dot(a1, wa2_ref[...]) + ba_ref[...]
    log_sig = jnp.minimum(za, 0.0) - jnp.log(1.0 + jnp.exp(-jnp.abs(za)))
    g_ref[...] = _chunk_cumsum(log_sig * (LOG2E / GLA_GATE_NORM), tri_ref[...])


def _inproj_odd(x2, mod, norm_g, w_in, w_a1, w_a2, b_a, kw, vw, q_scale, tiles_per_batch):
    t, d = x2.shape
    tb = TOKEN_TILE
    row = lambda i: (i, 0)
    tri3 = jnp.asarray(_prefix_matrix(RECUR_CHUNK), BF16)
    return pl.pallas_call(
        functools.partial(_inproj_odd_kernel, q_scale=q_scale),
        grid=(t // tb,),
        in_specs=[
            pl.BlockSpec((tb, d), row),
            pl.BlockSpec((1, N_MOD, d), lambda i: (i // tiles_per_batch, 0, 0)),
            _const_spec((1, d)),
            _const_spec(w_in.shape),
            _const_spec(w_a1.shape),
            _const_spec(w_a2.shape),
            _const_spec(b_a.shape),
            _const_spec(tri3.shape),
        ],
        out_specs=[pl.BlockSpec((tb, kw), row), pl.BlockSpec((tb, kw), row), pl.BlockSpec((tb, kw), row),
                   pl.BlockSpec((tb, vw), row), pl.BlockSpec((tb, vw), row)],
        out_shape=[jax.ShapeDtypeStruct((t, kw), BF16), jax.ShapeDtypeStruct((t, kw), BF16),
                   jax.ShapeDtypeStruct((t, kw), F32), jax.ShapeDtypeStruct((t, vw), BF16),
                   jax.ShapeDtypeStruct((t, vw), BF16)],
        compiler_params=_params(("arbitrary",)),
        name="inproj_odd",
    )(x2, mod, norm_g, w_in, w_a1, w_a2, b_a, tri3)


def _s5_kernel(u_ref, mt_ref, wsr_ref, wsi_ref, wir_ref, wii_ref, sr_ref, si_ref, y_ref,
               *, chunks_per_seq):
    u = u_ref[0]
    v_re = _dot(u, wsr_ref[0])
    v_im = _dot(u, wsi_ref[0])
    n_in_seq = lax.broadcasted_iota(jnp.int32, v_re.shape, 0) & (chunks_per_seq - 1)
    for lv in range(sr_ref.shape[1]):
        step = 1 << lv
        keep = n_in_seq >= step
        s_re = jnp.where(keep, pltpu.roll(v_re, step, 0), 0.0)
        s_im = jnp.where(keep, pltpu.roll(v_im, step, 0), 0.0)
        a_re, a_im = sr_ref[0, lv:lv + 1, :], si_ref[0, lv:lv + 1, :]
        v_re, v_im = v_re + a_re * s_re - a_im * s_im, v_im + a_re * s_im + a_im * s_re
    keep = n_in_seq >= 1
    x_re = jnp.where(keep, pltpu.roll(v_re, 1, 0), 0.0).astype(BF16)
    x_im = jnp.where(keep, pltpu.roll(v_im, 1, 0), 0.0).astype(BF16)
    y = _dot(u, mt_ref[0]) + _dot_nt(x_re, wir_ref[0]) + _dot_nt(x_im, wii_ref[0])
    y_ref[0] = y.astype(BF16)


def _s5_prep_kernel(bt_re_ref, bt_im_ref, c_re_ref, c_im_ref, lp_re_ref, lp_im_ref,
                    mt_ref, wsr_ref, wsi_ref, wir_ref, wii_ref, toep_ref, *, chunk):
    hg, p = bt_re_ref.shape[1:]
    k = chunk * hg
    t = lax.broadcasted_iota(jnp.int32, (k, p), 0) // hg

    def lam_pow(e):
        re = jnp.ones((k, p), F32)
        im = jnp.zeros((k, p), F32)
        for b in range(lp_re_ref.shape[1]):
            b_re, b_im = lp_re_ref[0, b:b + 1, :], lp_im_ref[0, b:b + 1, :]
            bit = ((e >> b) & 1) == 1
            re, im = (jnp.where(bit, re * b_re - im * b_im, re),
                      jnp.where(bit, re * b_im + im * b_re, im))
        return re, im

    tile = lambda x: jnp.concatenate([x] * chunk, axis=0)
    b_re, b_im = tile(bt_re_ref[0]), tile(bt_im_ref[0])
    c_re, c_im = tile(c_re_ref[0]), tile(c_im_ref[0])

    p_re, p_im = lam_pow(t)
    z_re, z_im = c_re * p_re - c_im * p_im, c_re * p_im + c_im * p_re
    hp = lax.Precision.HIGHEST
    nt = (((1,), (1,)), ((), ()))
    kt = (lax.dot_general(bt_re_ref[0], z_re, nt, precision=hp, preferred_element_type=F32)
          - lax.dot_general(bt_im_ref[0], z_im, nt, precision=hp, preferred_element_type=F32))
    col_t = lax.broadcasted_iota(jnp.int32, (chunk, k), 1) // hg
    row_s = lax.broadcasted_iota(jnp.int32, (chunk, k), 0)
    for hi in range(hg):
        rows = jnp.broadcast_to(kt[hi:hi + 1, :], (chunk, k))
        rows = pltpu.roll(rows, 0, 1, stride=hg, stride_axis=0)
        rows = jnp.where(col_t >= row_s, rows, 0.0)
        for j in range(k // LANES):
            toep_ref[j, pl.ds(hi, chunk, stride=hg), :] = rows[:, j * LANES:(j + 1) * LANES]
    for j in range(k // LANES):
        mt_ref[0, :, j * LANES:(j + 1) * LANES] = toep_ref[j].astype(BF16)
    p_re, p_im = lam_pow(chunk - 1 - t)
    wsr_ref[0] = (b_re * p_re - b_im * p_im).astype(BF16)
    wsi_ref[0] = (b_re * p_im + b_im * p_re).astype(BF16)
    l_re, l_im = lp_re_ref[0, 0:1, :], lp_im_ref[0, 0:1, :]
    wir_ref[0] = (z_re * l_re - z_im * l_im).astype(BF16)
    wii_ref[0] = (-(z_re * l_im + z_im * l_re)).astype(BF16)


def _s5_discretise(lam_re, lam_im, b_re, b_im, log_step, chunk, chunks_per_seq):
    delta = jnp.exp(log_step)[:, None]
    ld_re, ld_im = lam_re * delta, lam_im * delta

    def powers(exponents):
        e = jnp.asarray(exponents, F32)[None, :, None]
        mag = jnp.exp(ld_re[:, None, :] * e)
        ang = ld_im[:, None, :] * e
        return mag * jnp.cos(ang), mag * jnp.sin(ang)

    lp_re, lp_im = powers([1 << b for b in range(chunk.bit_length())])
    nr, ni = lp_re[:, 0] - 1.0, lp_im[:, 0]
    den = lam_re * lam_re + lam_im * lam_im
    s_re = (nr * lam_re + ni * lam_im) / den
    s_im = (ni * lam_re - nr * lam_im) / den
    bt_re = (s_re[:, :, None] * b_re - s_im[:, :, None] * b_im).transpose(0, 2, 1)
    bt_im = (s_re[:, :, None] * b_im + s_im[:, :, None] * b_re).transpose(0, 2, 1)
    n_lev = int(math.log2(chunks_per_seq))
    sc_re, sc_im = powers([chunk * (1 << lv) for lv in range(n_lev)])
    return bt_re, bt_im, lp_re, lp_im, sc_re, sc_im


def _s5_mix(u2, lam_re, lam_im, b_re, b_im, c_re, c_im, log_step, chunks_per_seq):
    g, r, k = u2.shape
    chunk = k // GRANULE
    p = lam_re.shape[1]
    bt_re, bt_im, lp_re, lp_im, sc_re, sc_im = _s5_discretise(
        lam_re, lam_im, b_re, b_im, log_step, chunk, chunks_per_seq)
    grp = lambda i: (i, 0, 0)
    small = lambda a: pl.BlockSpec((1,) + a.shape[1:], grp)
    kp = jax.ShapeDtypeStruct((g, k, p), BF16)
    ops = pl.pallas_call(
        functools.partial(_s5_prep_kernel, chunk=chunk),
        grid=(g,),
        in_specs=[small(bt_re), small(bt_im), small(c_re), small(c_im), small(lp_re), small(lp_im)],
        out_specs=[pl.BlockSpec((1, k, k), grp)] + [pl.BlockSpec((1, k, p), grp)] * 4,
        out_shape=[jax.ShapeDtypeStruct((g, k, k), BF16), kp, kp, kp, kp],
        scratch_shapes=[pltpu.VMEM((k // LANES, k, LANES), F32)],
        compiler_params=_params(("arbitrary",)),
        name="s5_prep",
    )(bt_re, bt_im, c_re, c_im, lp_re, lp_im)
    return pl.pallas_call(
        functools.partial(_s5_kernel, chunks_per_seq=chunks_per_seq),
        grid=(g,),
        in_specs=[pl.BlockSpec((1, r, k), grp), pl.BlockSpec((1, k, k), grp)]
                 + [pl.BlockSpec((1, k, p), grp)] * 4 + [small(sc_re), small(sc_im)],
        out_specs=pl.BlockSpec((1, r, k), grp),
        out_shape=jax.ShapeDtypeStruct((g, r, k), BF16),
        compiler_params=_params(("arbitrary",)),
        name="s5_mix",
    )(u2, *ops, sc_re, sc_im)


def _level_index(chunk):
    i = np.arange(chunk)[:, None]
    j = np.arange(chunk)[None, :]
    x = np.bitwise_xor(i, j)
    lvl = np.floor(np.log2(np.maximum(x, 1))).astype(np.int32)
    return np.where(i > j, lvl, np.where(i == j, -1, -2)).astype(np.int32)


def _recur_kernel(lvl_ref, q_ref, k_ref, c_ref, v_ref, gt_ref, ng_ref, y_ref, st_ref,
                  *, chunk, heads, dk, dv):
    @pl.when(pl.program_id(1) == 0)
    def _():
        st_ref[...] = jnp.zeros(st_ref.shape, F32)

    n_chunks = q_ref.shape[0] // chunk
    n_levels = int(math.log2(chunk))
    lvl = lvl_ref[...]
    row = lax.broadcasted_iota(jnp.int32, (chunk, dk), 0)
    ng = ng_ref[...]

    sub = lax.broadcasted_iota(jnp.int32, (SUBLANES, dk), 0)

    for c in range(n_chunks):
        r0 = c * chunk
        rows = slice(r0, r0 + chunk)
        for hd in range(heads):
            ks = slice(hd * dk, (hd + 1) * dk)
            vs = slice(hd * dv, (hd + 1) * dv)
            qb = q_ref[rows, ks]
            kb = k_ref[rows, ks]
            vb = v_ref[rows, vs]
            q = qb.astype(F32)
            k = kb.astype(F32)
            cum = c_ref[rows, ks]

            def row_on_sublanes(i):
                return jnp.broadcast_to(c_ref[pl.ds(r0 + i, 1), ks], (SUBLANES, dk))

            scores = jnp.where(lvl == -1, _dot_nt(qb, kb), 0.0)
            for lv in range(n_levels):
                h = 1 << lv
                second = (row & h) != 0
                if h == 1:
                    boundary = jnp.where(second, pltpu.roll(cum, 1, 0), cum)
                else:
                    pieces = []
                    for v in range(chunk // SUBLANES):
                        first_row = v * SUBLANES
                        if 2 * h <= SUBLANES:
                            piece = row_on_sublanes(first_row + h - 1)
                            for b in range(1, SUBLANES // (2 * h)):
                                piece = jnp.where(sub >= 2 * h * b,
                                                  row_on_sublanes(first_row + 2 * h * b + h - 1), piece)
                        elif first_row % (2 * h) == 0:
                            piece = row_on_sublanes(first_row + h - 1)
                        pieces.append(piece)
                    boundary = jnp.concatenate(pieces, axis=0)
                decay = jnp.exp2(-jnp.abs(cum - boundary))
                w = (jnp.where(second, q, k) * decay).astype(BF16)
                scores = jnp.where(lvl == lv, _dot_nt(w, w), scores)
            last = row_on_sublanes(chunk - 1)
            st = st_ref[hd]
            o = _dot(scores.astype(BF16), vb)
            o = o + _dot_nt((q * jnp.exp2(cum)).astype(BF16), st.astype(BF16))
            kd = (k * jnp.exp2(jnp.concatenate([last] * (chunk // SUBLANES), axis=0) - cum)).astype(BF16)
            st_ref[hd] = st * jnp.exp2(last[0:1, :]) + _dot_tn(vb, kd)
            ms = jnp.mean(o * o, axis=-1, keepdims=True)
            o = o * lax.rsqrt(ms + EPS) * ng * gt_ref[rows, vs].astype(F32)
            y_ref[rows, vs] = o.astype(BF16)


def _gated_recurrence(q, k, g, v, gate, norm_g, bsz, heads):
    t, kw = q.shape
    vw = v.shape[1]
    dk, dv = kw // heads, vw // heads
    tt = RECUR_TILE
    per_seq = (t // bsz) // tt
    row = lambda b, i: (b * per_seq + i, 0)
    lvl = jnp.asarray(_level_index(RECUR_CHUNK))
    return pl.pallas_call(
        functools.partial(_recur_kernel, chunk=RECUR_CHUNK, heads=heads, dk=dk, dv=dv),
        grid=(bsz, per_seq),
        in_specs=[
            pl.BlockSpec(lvl.shape, lambda b, i: (0, 0)),
            pl.BlockSpec((tt, kw), row), pl.BlockSpec((tt, kw), row), pl.BlockSpec((tt, kw), row),
            pl.BlockSpec((tt, vw), row), pl.BlockSpec((tt, vw), row),
            pl.BlockSpec((1, dv), lambda b, i: (0, 0)),
        ],
        out_specs=pl.BlockSpec((tt, vw), row),
        out_shape=jax.ShapeDtypeStruct((t, vw), BF16),
        scratch_shapes=[pltpu.VMEM((heads, dv, dk), F32)],
        compiler_params=_params(("arbitrary", "arbitrary")),
        name="gated_recurrence",
    )(lvl, q, k, g, v, gate, norm_g)


def _gelu_tanh(x):
    return 0.5 * x * (1.0 + jnp.tanh(math.sqrt(2.0 / math.pi) * (x + 0.044715 * (x * x * x))))


def _ffn_tail(x, mod_ref, nf_ref, w1_ref, w3_ref, w2_ref):
    h = _norm_modulate(x, nf_ref[...], mod_ref[0, 3:4, :], mod_ref[0, 4:5, :]).astype(BF16)
    acc = None
    for j in range(w1_ref.shape[0]):
        a = _dot(h, w1_ref[j])
        b = _dot(h, w3_ref[j])
        part = _dot((_silu(a) * b).astype(BF16), w2_ref[j])
        acc = part if acc is None else acc + part
    return x + mod_ref[0, 5:6, :] * acc


def _block_even_kernel(x_ref, y2_ref, u_ref, yb_ref, mod_ref, sd_ref, wg_ref, bg_ref, wo_ref,
                       nf_ref, w1_ref, w3_ref, w2_ref, o_ref, tok_ref):
    for rows in _sub_tiles(x_ref.shape[0], BLOCK_SUB_TILES):
        _to_token_rows(y2_ref, tok_ref, rows)
        y = jnp.concatenate([tok_ref[j, rows, :] for j in range(tok_ref.shape[0])], axis=1)
        y = y + sd_ref[...] * u_ref[rows, :].astype(F32)
        y = _gelu_tanh(y)
        ya = y * jax.nn.sigmoid(_dot(y.astype(BF16), wg_ref[...]) + bg_ref[...])
        sw = ya.shape[1]
        mixed = _dot(ya.astype(BF16), wo_ref[0:sw, :]) + _dot(yb_ref[rows, :], wo_ref[sw:, :])
        x = x_ref[rows, :] + mod_ref[0, 2:3, :] * mixed
        o_ref[rows, :] = _ffn_tail(x, mod_ref, nf_ref, w1_ref, w3_ref, w2_ref)


def _block_odd_kernel(x_ref, yc_ref, mod_ref, wo_ref, nf_ref, w1_ref, w3_ref, w2_ref, fg_ref,
                      o_ref, *, final_norm):
    for rows in _sub_tiles(x_ref.shape[0], BLOCK_SUB_TILES):
        x = x_ref[rows, :] + mod_ref[0, 2:3, :] * _dot(yc_ref[rows, :], wo_ref[...])
        x = _ffn_tail(x, mod_ref, nf_ref, w1_ref, w3_ref, w2_ref)
        if final_norm:
            ms = jnp.mean(x * x, axis=-1, keepdims=True)
            x = x * lax.rsqrt(ms + EPS) * fg_ref[...]
        o_ref[rows, :] = x


def _ffn_weights(w1, w3, w2):
    d, hid = w1.shape
    nj = hid // FFN_TILE
    w1t = w1.astype(BF16).reshape(d, nj, FFN_TILE).transpose(1, 0, 2)
    w3t = w3.astype(BF16).reshape(d, nj, FFN_TILE).transpose(1, 0, 2)
    w2t = w2.astype(BF16).reshape(nj, FFN_TILE, d)
    return w1t, w3t, w2t


def _block_even(x2, y2, u, yb, mod, s5_d, w_glu, b_glu, w_out, nf, w1t, w3t, w2t, tiles_per_batch):
    t, d = x2.shape
    tb = TOKEN_TILE
    row = lambda i: (i, 0)
    sw = u.shape[1]
    groups, _, k = y2.shape
    return pl.pallas_call(
        _block_even_kernel,
        grid=(t // tb,),
        in_specs=[
            pl.BlockSpec((tb, d), row),
            pl.BlockSpec((groups, tb * GRANULE // k, k), lambda i: (0, i, 0)),
            pl.BlockSpec((tb, sw), row),
            pl.BlockSpec((tb, yb.shape[1]), row),
            pl.BlockSpec((1, N_MOD, d), lambda i: (i // tiles_per_batch, 0, 0)),
            _const_spec(s5_d.shape), _const_spec(w_glu.shape), _const_spec(b_glu.shape),
            _const_spec(w_out.shape), _const_spec(nf.shape),
            _const_spec(w1t.shape), _const_spec(w3t.shape), _const_spec(w2t.shape),
        ],
        out_specs=pl.BlockSpec((tb, d), row),
        out_shape=jax.ShapeDtypeStruct((t, d), F32),
        scratch_shapes=[pltpu.VMEM((sw // LANES, tb, LANES), F32)],
        compiler_params=_params(("arbitrary",)),
        name="block_even",
    )(x2, y2, u, yb, mod, s5_d, w_glu, b_glu, w_out, nf, w1t, w3t, w2t)


def _block_odd(x2, yc, mod, w_out, nf, w1t, w3t, w2t, final_g, final_norm, tiles_per_batch):
    t, d = x2.shape
    tb = TOKEN_TILE
    row = lambda i: (i, 0)
    return pl.pallas_call(
        functools.partial(_block_odd_kernel, final_norm=final_norm),
        grid=(t // tb,),
        in_specs=[
            pl.BlockSpec((tb, d), row), pl.BlockSpec((tb, yc.shape[1]), row),
            pl.BlockSpec((1, N_MOD, d), lambda i: (i // tiles_per_batch, 0, 0)),
            _const_spec(w_out.shape), _const_spec(nf.shape),
            _const_spec(w1t.shape), _const_spec(w3t.shape), _const_spec(w2t.shape),
            _const_spec(final_g.shape),
        ],
        out_specs=pl.BlockSpec((tb, d), row),
        out_shape=jax.ShapeDtypeStruct((t, d), F32),
        compiler_params=_params(("arbitrary",)),
        name="block_odd",
    )(x2, yc, mod, w_out, nf, w1t, w3t, w2t, final_g)


def kernel(x, c, ada_w, ada_b, norm_mix_g, norm_ffn_g, ev_w_in, ev_w_out, s5_lam_re, s5_lam_im, s5_b_re, s5_b_im, s5_c_re, s5_c_im, s5_d, s5_log_step, s5_w_glu, s5_b_glu, hg_lb_logits, hg_norm_g, od_w_in, od_w_a1, od_w_a2, od_b_a, gla_norm_g, od_w_out, ffn_w1, ffn_w3, ffn_w2, final_norm_g):
    bsz, seq, d = x.shape
    depth = ada_w.shape[0]
    assert depth % 2 == 0, "the final norm is fused into the last (odd) layer"
    t = bsz * seq
    tiles_per_batch = seq // TOKEN_TILE
    x2 = x.reshape(t, d)
    mod = _adaln(c, ada_w, ada_b).reshape(depth, bsz, N_MOD, d)

    for layer in range(depth):
        nm = norm_mix_g[layer].reshape(1, d)
        nf = norm_ffn_g[layer].reshape(1, d)
        w1t, w3t, w2t = _ffn_weights(ffn_w1[layer], ffn_w3[layer], ffn_w2[layer])
        if layer % 2 == 0:
            e = layer // 2
            u, u2, hq, hk, hgl, hv, hgate = _inproj_even(
                x2, mod[layer], nm, ev_w_in[e].astype(BF16), hg_lb_logits, layer, tiles_per_batch)
            sw = u.shape[1]
            y2 = _s5_mix(u2, s5_lam_re[e], s5_lam_im[e], s5_b_re[e], s5_b_im[e], s5_c_re[e],
                         s5_c_im[e], s5_log_step[e], seq // S5_CHUNK)
            yb = _gated_recurrence(hq, hk, hgl, hv, hgate, hg_norm_g[e].reshape(1, -1), bsz, HG_HEADS)
            x2 = _block_even(x2, y2, u, yb, mod[layer], s5_d[e].reshape(1, sw),
                             s5_w_glu[e].astype(BF16), s5_b_glu[e].reshape(1, sw),
                             ev_w_out[e].astype(BF16), nf, w1t, w3t, w2t, tiles_per_batch)
        else:
            o = layer // 2
            kw = od_w_a2.shape[-1]
            vw = od_w_out.shape[1]
            rank = od_w_a1.shape[-1]
            w_a1 = jnp.pad(od_w_a1[o], ((0, 0), (0, LANES - rank))).astype(BF16)
            w_a2 = jnp.pad(od_w_a2[o], ((0, LANES - rank), (0, 0))).astype(BF16)
            q, k, g, v, gate = _inproj_odd(
                x2, mod[layer], nm, od_w_in[o].astype(BF16), w_a1, w_a2, od_b_a[o].reshape(1, kw),
                kw, vw, (kw // GLA_HEADS) ** -0.5, tiles_per_batch)
            yc = _gated_recurrence(q, k, g, v, gate, gla_norm_g[o].reshape(1, -1), bsz, GLA_HEADS)
            x2 = _block_odd(x2, yc, mod[layer], od_w_out[o].astype(BF16), nf, w1t, w3t, w2t,
                            final_norm_g.reshape(1, d), layer == depth - 1, tiles_per_batch)
    return x2.reshape(bsz, seq, d)
```

```python
import functools
import math

import numpy as np
import jax
import jax.numpy as jnp
from jax import lax
from jax.experimental import pallas as pl
from jax.experimental.pallas import tpu as pltpu

F32 = jnp.float32
BF16 = jnp.bfloat16
EPS = 1e-6
LOG2E = math.log2(math.e)

HG_HEADS = 4
GLA_HEADS = 4
GLA_GATE_NORM = 16.0
N_MOD = 6

VMEM_LIMIT_BYTES = 56 * 1024 * 1024
LANES = 128
SUBLANES = 8
BF16_ROWS = 16
GRANULE = 16
GRANULES = LANES // GRANULE
RELAYOUT_ROWS = 32

TOKEN_TILE = 1024
RECUR_TILE = 512
RECUR_CHUNK = 128
S5_CHUNK = 32
FFN_TILE = 256
BLOCK_SUB_TILES = 2


def _dot(a, b):
    return jnp.dot(a, b, preferred_element_type=F32)


def _dot_nt(a, b):
    return lax.dot_general(a, b, (((1,), (1,)), ((), ())), preferred_element_type=F32)


def _dot_tn(a, b):
    return lax.dot_general(a, b, (((0,), (0,)), ((), ())), preferred_element_type=F32)


def _silu(x):
    return x * jax.nn.sigmoid(x)


def _params(semantics):
    return pltpu.CompilerParams(dimension_semantics=semantics, vmem_limit_bytes=VMEM_LIMIT_BYTES)


def _const_spec(shape):
    nd = len(shape)
    return pl.BlockSpec(shape, lambda *_: (0,) * nd, pipeline_mode=pl.Buffered(1))


def _sub_tiles(n_rows, n_sub):
    step = n_rows // n_sub
    return [slice(i * step, (i + 1) * step) for i in range(n_sub)]


def _cast_specs(weights, n_steps):
    specs = []
    for w in weights:
        n_rows = w.shape[0]
        rows = next(r for r in range(BF16_ROWS, n_rows + 1, BF16_ROWS)
                    if n_rows % r == 0 and r * n_steps >= n_rows)
        last = n_rows // rows - 1
        specs.append(pl.BlockSpec((rows, w.shape[1]), lambda i, last=last: (jnp.minimum(i, last), 0)))
    return specs


def _cast_weights(f32_refs, bf16_refs):
    for src, dst in zip(f32_refs, bf16_refs):
        dst[...] = src[...].astype(BF16)


def _prefix_matrix(chunk):
    tri = np.tril(np.ones((chunk, chunk), np.float32))
    return np.concatenate([tri, tri, tri], axis=1)


def _chunk_cumsum(g, tri3):
    chunk = tri3.shape[0]
    hi = g.astype(BF16)
    r1 = g - hi.astype(F32)
    mid = r1.astype(BF16)
    lo = (r1 - mid.astype(F32)).astype(BF16)
    out = []
    for c in range(g.shape[0] // chunk):
        rows = slice(c * chunk, (c + 1) * chunk)
        out.append(_dot(tri3, jnp.concatenate([hi[rows], mid[rows], lo[rows]], axis=0)))
    return jnp.concatenate(out, axis=0)


def _norm_modulate(x, norm_g, shift, scale):
    ms = jnp.mean(x * x, axis=-1, keepdims=True)
    h = x * lax.rsqrt(ms + EPS) * norm_g
    return h * (1.0 + scale) + shift


def _adaln_kernel(ct_ref, w_ref, b_ref, o_ref):
    cond = _silu(ct_ref[...])
    w = w_ref[0]
    for b in range(cond.shape[1]):
        o_ref[0, b:b + 1, :] = jnp.sum(cond[:, b:b + 1] * w, axis=0, keepdims=True) + b_ref[0]


def _adaln(c, ada_w, ada_b):
    depth, d, n = ada_w.shape
    bsz = c.shape[0]
    nt = 1024
    return pl.pallas_call(
        _adaln_kernel,
        grid=(depth, n // nt),
        in_specs=[
            pl.BlockSpec((d, bsz), lambda l, j: (0, 0)),
            pl.BlockSpec((1, d, nt), lambda l, j: (l, 0, j)),
            pl.BlockSpec((1, 1, nt), lambda l, j: (l, 0, j)),
        ],
        out_specs=pl.BlockSpec((1, bsz, nt), lambda l, j: (l, 0, j)),
        out_shape=jax.ShapeDtypeStruct((depth, bsz, n), F32),
        compiler_params=_params(("arbitrary", "arbitrary")),
        name="adaln_mod",
    )(c.T, ada_w, ada_b.reshape(depth, 1, n))


def _granule_transpose(blocks, slot):
    blocks = list(blocks)
    for d in (4, 2, 1):
        upper = (slot & d) != 0
        for a in range(GRANULES):
            if a & d:
                continue
            lo, hi = blocks[a], blocks[a + d]
            blocks[a] = jnp.where(upper, pltpu.roll(hi, GRANULE * d, 1), lo)
            blocks[a + d] = jnp.where(upper, hi, pltpu.roll(lo, LANES - GRANULE * d, 1))
    return blocks


def _to_chunk_rows(tok_ref, out_ref, tok_rows):
    n_grp, _, k = out_ref.shape
    chunk = k // GRANULE
    n_rows = min(RELAYOUT_ROWS, (tok_rows.stop - tok_rows.start) // chunk)
    slot = lax.broadcasted_iota(jnp.int32, (n_rows, LANES), 1) // GRANULE
    for r0 in range(tok_rows.start // chunk, tok_rows.stop // chunk, n_rows):
        for tb in range(chunk // GRANULES):
            for j in range(n_grp // GRANULES):
                rows = [tok_ref[j, pl.ds(r0 * chunk + GRANULES * tb + tp, n_rows, stride=chunk), :]
                        for tp in range(GRANULES)]
                for gp, blk in enumerate(_granule_transpose(rows, slot)):
                    out_ref[GRANULES * j + gp, r0:r0 + n_rows, tb * LANES:(tb + 1) * LANES] = (
                        blk.astype(out_ref.dtype))


def _to_token_rows(grp_ref, tok_ref, tok_rows):
    n_grp, _, k = grp_ref.shape
    chunk = k // GRANULE
    n_rows = min(RELAYOUT_ROWS, (tok_rows.stop - tok_rows.start) // chunk)
    slot = lax.broadcasted_iota(jnp.int32, (n_rows, LANES), 1) // GRANULE
    for r0 in range(tok_rows.start // chunk, tok_rows.stop // chunk, n_rows):
        for tb in range(chunk // GRANULES):
            for j in range(n_grp // GRANULES):
                cols = [grp_ref[GRANULES * j + gp, r0:r0 + n_rows, tb * LANES:(tb + 1) * LANES].astype(F32)
                        for gp in range(GRANULES)]
                for tp, blk in enumerate(_granule_transpose(cols, slot)):
                    tok_ref[j, pl.ds(r0 * chunk + GRANULES * tb + tp, n_rows, stride=chunk), :] = blk


def _inproj_even_kernel(x_ref, mod_ref, ng_ref, w_ref, lbl_ref, tri_ref, f1_ref, f3_ref, f2_ref,
                        u_ref, u2_ref, q_ref, k_ref, g_ref, v_ref, gt_ref, b1_ref, b3_ref, b2_ref,
                        tok_ref, *, lb_row):
    _cast_weights((f1_ref, f3_ref, f2_ref), (b1_ref, b3_ref, b2_ref))
    w = u_ref.shape[1]
    lg = lbl_ref[...]
    e = jnp.exp(lg - jnp.max(lg, axis=0, keepdims=True))
    lb = jnp.sum(e[:lb_row + 1], axis=0, keepdims=True) / jnp.sum(e, axis=0, keepdims=True)
    h = _norm_modulate(x_ref[...], ng_ref[...], mod_ref[0, 0:1, :], mod_ref[0, 1:2, :])
    hb = h.astype(BF16)
    part = lambda p: _dot(hb, w_ref[:, p * w:(p + 1) * w])
    rows = slice(0, x_ref.shape[0])
    z = part(0)
    for j in range(tok_ref.shape[0]):
        tok_ref[j] = z[:, j * LANES:(j + 1) * LANES]
    _to_chunk_rows(tok_ref, u2_ref, rows)
    u_ref[...] = z.astype(BF16)
    q_ref[...] = _silu(part(1)).astype(BF16)
    f = part(2)
    k_ref[...] = ((1.0 - lb) * jax.nn.sigmoid(-f)).astype(BF16)
    g_ref[...] = _chunk_cumsum(jnp.log(lb + (1.0 - lb) * jax.nn.sigmoid(f)) * LOG2E, tri_ref[...])
    v_ref[...] = part(3).astype(BF16)
    gt_ref[...] = _silu(part(4)).astype(BF16)


def _inproj_even(x2, mod, norm_g, w_in, lb_logits, lb_row, ffn_f32, tiles_per_batch):
    t, d = x2.shape
    w = w_in.shape[1] // 5
    tb = TOKEN_TILE
    row = lambda i: (i, 0)
    out_bf = jax.ShapeDtypeStruct((t, w), BF16)
    groups = w // GRANULE
    k = S5_CHUNK * GRANULE
    tok = pl.BlockSpec((tb, w), row)
    tri3 = jnp.asarray(_prefix_matrix(RECUR_CHUNK), BF16)
    cast = _cast_specs(ffn_f32, t // tb)
    return pl.pallas_call(
        functools.partial(_inproj_even_kernel, lb_row=lb_row),
        grid=(t // tb,),
        in_specs=[
            pl.BlockSpec((tb, d), row),
            pl.BlockSpec((1, N_MOD, d), lambda i: (i // tiles_per_batch, 0, 0)),
            _const_spec((1, d)),
            _const_spec(w_in.shape),
            _const_spec(lb_logits.shape),
            _const_spec(tri3.shape),
        ] + cast,
        out_specs=[tok, pl.BlockSpec((groups, tb // S5_CHUNK, k), lambda i: (0, i, 0)),
                   tok, tok, tok, tok, tok] + cast,
        out_shape=[out_bf, jax.ShapeDtypeStruct((groups, t // S5_CHUNK, k), BF16),
                   out_bf, out_bf, jax.ShapeDtypeStruct((t, w), F32), out_bf, out_bf]
                  + [jax.ShapeDtypeStruct(a.shape, BF16) for a in ffn_f32],
        scratch_shapes=[pltpu.VMEM((w // LANES, tb, LANES), F32)],
        compiler_params=_params(("arbitrary",)),
        name="inproj_even",
    )(x2, mod, norm_g, w_in, lb_logits, tri3, *ffn_f32)


def _inproj_odd_kernel(x_ref, mod_ref, ng_ref, w_ref, wa1_ref, wa2_ref, ba_ref, tri_ref,
                       f1_ref, f3_ref, f2_ref,
                       q_ref, k_ref, g_ref, v_ref, gt_ref, b1_ref, b3_ref, b2_ref, *, q_scale):
    _cast_weights((f1_ref, f3_ref, f2_ref), (b1_ref, b3_ref, b2_ref))
    kw = q_ref.shape[1]
    vw = v_ref.shape[1]
    h = _norm_modulate(x_ref[...], ng_ref[...], mod_ref[0, 0:1, :], mod_ref[0, 1:2, :])
    hb = h.astype(BF16)
    q_ref[...] = (_dot(hb, w_ref[:, 0:kw]) * q_scale).astype(BF16)
    k_ref[...] = _dot(hb, w_ref[:, kw:2 * kw]).astype(BF16)
    v_ref[...] = _dot(hb, w_ref[:, 2 * kw:2 * kw + vw]).astype(BF16)
    gt_ref[...] = _silu(_dot(hb, w_ref[:, 2 * kw + vw:2 * kw + 2 * vw])).astype(BF16)
    a1 = _dot(hb, wa1_ref[...]).astype(BF16)
    za = _dot(a1, wa2_ref[...]) + ba_ref[...]
    log_sig = jnp.minimum(za, 0.0) - jnp.log(1.0 + jnp.exp(-jnp.abs(za)))
    g_ref[...] = _chunk_cumsum(log_sig * (LOG2E / GLA_GATE_NORM), tri_ref[...])


def _inproj_odd(x2, mod, norm_g, w_in, w_a1, w_a2, b_a, kw, vw, q_scale, ffn_f32, tiles_per_batch):
    t, d = x2.shape
    tb = TOKEN_TILE
    row = lambda i: (i, 0)
    tri3 = jnp.asarray(_prefix_matrix(RECUR_CHUNK), BF16)
    cast = _cast_specs(ffn_f32, t // tb)
    return pl.pallas_call(
        functools.partial(_inproj_odd_kernel, q_scale=q_scale),
        grid=(t // tb,),
        in_specs=[
            pl.BlockSpec((tb, d), row),
            pl.BlockSpec((1, N_MOD, d), lambda i: (i // tiles_per_batch, 0, 0)),
            _const_spec((1, d)),
            _const_spec(w_in.shape),
            _const_spec(w_a1.shape),
            _const_spec(w_a2.shape),
            _const_spec(b_a.shape),
            _const_spec(tri3.shape),
        ] + cast,
        out_specs=[pl.BlockSpec((tb, kw), row), pl.BlockSpec((tb, kw), row), pl.BlockSpec((tb, kw), row),
                   pl.BlockSpec((tb, vw), row), pl.BlockSpec((tb, vw), row)] + cast,
        out_shape=[jax.ShapeDtypeStruct((t, kw), BF16), jax.ShapeDtypeStruct((t, kw), BF16),
                   jax.ShapeDtypeStruct((t, kw), F32), jax.ShapeDtypeStruct((t, vw), BF16),
                   jax.ShapeDtypeStruct((t, vw), BF16)]
                  + [jax.ShapeDtypeStruct(a.shape, BF16) for a in ffn_f32],
        compiler_params=_params(("arbitrary",)),
        name="inproj_odd",
    )(x2, mod, norm_g, w_in, w_a1, w_a2, b_a, tri3, *ffn_f32)


def _s5_kernel(u_ref, mt_ref, wsr_ref, wsi_ref, wir_ref, wii_ref, sr_ref, si_ref, y_ref,
               *, chunks_per_seq):
    u = u_ref[0]
    v_re = _dot(u, wsr_ref[0])
    v_im = _dot(u, wsi_ref[0])
    n_in_seq = lax.broadcasted_iota(jnp.int32, v_re.shape, 0) & (chunks_per_seq - 1)
    for lv in range(sr_ref.shape[1]):
        step = 1 << lv
        keep = n_in_seq >= step
        s_re = jnp.where(keep, pltpu.roll(v_re, step, 0), 0.0)
        s_im = jnp.where(keep, pltpu.roll(v_im, step, 0), 0.0)
        a_re, a_im = sr_ref[0, lv:lv + 1, :], si_ref[0, lv:lv + 1, :]
        v_re, v_im = v_re + a_re * s_re - a_im * s_im, v_im + a_re * s_im + a_im * s_re
    keep = n_in_seq >= 1
    x_re = jnp.where(keep, pltpu.roll(v_re, 1, 0), 0.0).astype(BF16)
    x_im = jnp.where(keep, pltpu.roll(v_im, 1, 0), 0.0).astype(BF16)
    y = _dot(u, mt_ref[0]) + _dot_nt(x_re, wir_ref[0]) + _dot_nt(x_im, wii_ref[0])
    y_ref[0] = y.astype(BF16)


def _s5_prep_kernel(bt_re_ref, bt_im_ref, c_re_ref, c_im_ref, lp_re_ref, lp_im_ref,
                    mt_ref, wsr_ref, wsi_ref, wir_ref, wii_ref, toep_ref, *, chunk):
    hg, p = bt_re_ref.shape[1:]
    k = chunk * hg
    hp = lax.Precision.HIGHEST

    tau = lax.broadcasted_iota(jnp.int32, (chunk, p), 0)
    pw_re = jnp.ones((chunk, p), F32)
    pw_im = jnp.zeros((chunk, p), F32)
    for b in range(lp_re_ref.shape[1]):
        l_re, l_im = lp_re_ref[0, b:b + 1, :], lp_im_ref[0, b:b + 1, :]
        bit = ((tau >> b) & 1) == 1
        pw_re, pw_im = (jnp.where(bit, pw_re * l_re - pw_im * l_im, pw_re),
                        jnp.where(bit, pw_re * l_im + pw_im * l_re, pw_im))

    def rows_of_powers(exponent):
        pick = (exponent == lax.broadcasted_iota(jnp.int32, (k, chunk), 1)).astype(F32)
        return (jnp.dot(pick, pw_re, precision=hp, preferred_element_type=F32),
                jnp.dot(pick, pw_im, precision=hp, preferred_element_type=F32))

    t = lax.broadcasted_iota(jnp.int32, (k, chunk), 0) // hg
    tile = lambda x: jnp.concatenate([x] * chunk, axis=0)
    b_re, b_im = tile(bt_re_ref[0]), tile(bt_im_ref[0])
    c_re, c_im = tile(c_re_ref[0]), tile(c_im_ref[0])

    p_re, p_im = rows_of_powers(t)
    z_re, z_im = c_re * p_re - c_im * p_im, c_re * p_im + c_im * p_re
    nt = (((1,), (1,)), ((), ()))
    kt = (lax.dot_general(bt_re_ref[0], z_re, nt, precision=hp, preferred_element_type=F32)
          - lax.dot_general(bt_im_ref[0], z_im, nt, precision=hp, preferred_element_type=F32))
    col_t = lax.broadcasted_iota(jnp.int32, (chunk, k), 1) // hg
    row_s = lax.broadcasted_iota(jnp.int32, (chunk, k), 0)
    for hi in range(hg):
        rows = jnp.broadcast_to(kt[hi:hi + 1, :], (chunk, k))
        rows = pltpu.roll(rows, 0, 1, stride=hg, stride_axis=0)
        rows = jnp.where(col_t >= row_s, rows, 0.0)
        for j in range(k // LANES):
            toep_ref[j, pl.ds(hi, chunk, stride=hg), :] = rows[:, j * LANES:(j + 1) * LANES]
    for j in range(k // LANES):
        mt_ref[0, :, j * LANES:(j + 1) * LANES] = toep_ref[j].astype(BF16)
    p_re, p_im = rows_of_powers(chunk - 1 - t)
    wsr_ref[0] = (b_re * p_re - b_im * p_im).astype(BF16)
    wsi_ref[0] = (b_re * p_im + b_im * p_re).astype(BF16)
    l_re, l_im = lp_re_ref[0, 0:1, :], lp_im_ref[0, 0:1, :]
    wir_ref[0] = (z_re * l_re - z_im * l_im).astype(BF16)
    wii_ref[0] = (-(z_re * l_im + z_im * l_re)).astype(BF16)


def _s5_discretise(lam_re, lam_im, b_re, b_im, log_step, chunk, chunks_per_seq):
    delta = jnp.exp(log_step)[:, None]
    ld_re, ld_im = lam_re * delta, lam_im * delta

    def powers(exponents):
        e = jnp.asarray(exponents, F32)[None, :, None]
        mag = jnp.exp(ld_re[:, None, :] * e)
        ang = ld_im[:, None, :] * e
        return mag * jnp.cos(ang), mag * jnp.sin(ang)

    lp_re, lp_im = powers([1 << b for b in range(chunk.bit_length())])
    nr, ni = lp_re[:, 0] - 1.0, lp_im[:, 0]
    den = lam_re * lam_re + lam_im * lam_im
    s_re = (nr * lam_re + ni * lam_im) / den
    s_im = (ni * lam_re - nr * lam_im) / den
    bt_re = (s_re[:, :, None] * b_re - s_im[:, :, None] * b_im).transpose(0, 2, 1)
    bt_im = (s_re[:, :, None] * b_im + s_im[:, :, None] * b_re).transpose(0, 2, 1)
    n_lev = int(math.log2(chunks_per_seq))
    sc_re, sc_im = powers([chunk * (1 << lv) for lv in range(n_lev)])
    return bt_re, bt_im, lp_re, lp_im, sc_re, sc_im


def _s5_mix(u2, lam_re, lam_im, b_re, b_im, c_re, c_im, log_step, chunks_per_seq):
    g, r, k = u2.shape
    chunk = k // GRANULE
    p = lam_re.shape[1]
    bt_re, bt_im, lp_re, lp_im, sc_re, sc_im = _s5_discretise(
        lam_re, lam_im, b_re, b_im, log_step, chunk, chunks_per_seq)
    grp = lambda i: (i, 0, 0)
    small = lambda a: pl.BlockSpec((1,) + a.shape[1:], grp)
    kp = jax.ShapeDtypeStruct((g, k, p), BF16)
    ops = pl.pallas_call(
        functools.partial(_s5_prep_kernel, chunk=chunk),
        grid=(g,),
        in_specs=[small(bt_re), small(bt_im), small(c_re), small(c_im), small(lp_re), small(lp_im)],
        out_specs=[pl.BlockSpec((1, k, k), grp)] + [pl.BlockSpec((1, k, p), grp)] * 4,
        out_shape=[jax.ShapeDtypeStruct((g, k, k), BF16), kp, kp, kp, kp],
        scratch_shapes=[pltpu.VMEM((k // LANES, k, LANES), F32)],
        compiler_params=_params(("arbitrary",)),
        name="s5_prep",
    )(bt_re, bt_im, c_re, c_im, lp_re, lp_im)
    return pl.pallas_call(
        functools.partial(_s5_kernel, chunks_per_seq=chunks_per_seq),
        grid=(g,),
        in_specs=[pl.BlockSpec((1, r, k), grp), pl.BlockSpec((1, k, k), grp)]
                 + [pl.BlockSpec((1, k, p), grp)] * 4 + [small(sc_re), small(sc_im)],
        out_specs=pl.BlockSpec((1, r, k), grp),
        out_shape=jax.ShapeDtypeStruct((g, r, k), BF16),
        compiler_params=_params(("arbitrary",)),
        name="s5_mix",
    )(u2, *ops, sc_re, sc_im)


def _level_index(chunk):
    i = np.arange(chunk)[:, None]
    j = np.arange(chunk)[None, :]
    x = np.bitwise_xor(i, j)
    lvl = np.floor(np.log2(np.maximum(x, 1))).astype(np.int32)
    return np.where(i > j, lvl, np.where(i == j, -1, -2)).astype(np.int32)


def _recur_kernel(lvl_ref, q_ref, k_ref, c_ref, v_ref, gt_ref, ng_ref, y_ref, st_ref,
                  *, chunk, heads, dk, dv):
    @pl.when(pl.program_id(1) == 0)
    def _():
        st_ref[...] = jnp.zeros(st_ref.shape, F32)

    n_chunks = q_ref.shape[0] // chunk
    n_levels = int(math.log2(chunk))
    lvl = lvl_ref[...]
    row = lax.broadcasted_iota(jnp.int32, (chunk, dk), 0)
    ng = ng_ref[...]

    sub = lax.broadcasted_iota(jnp.int32, (SUBLANES, dk), 0)

    for c in range(n_chunks):
        r0 = c * chunk
        rows = slice(r0, r0 + chunk)
        for hd in range(heads):
            ks = slice(hd * dk, (hd + 1) * dk)
            vs = slice(hd * dv, (hd + 1) * dv)
            qb = q_ref[rows, ks]
            kb = k_ref[rows, ks]
            vb = v_ref[rows, vs]
            q = qb.astype(F32)
            k = kb.astype(F32)
            cum = c_ref[rows, ks]

            def row_on_sublanes(i):
                return jnp.broadcast_to(c_ref[pl.ds(r0 + i, 1), ks], (SUBLANES, dk))

            scores = jnp.where(lvl == -1, _dot_nt(qb, kb), 0.0)
            for lv in range(n_levels):
                h = 1 << lv
                second = (row & h) != 0
                if h == 1:
                    boundary = jnp.where(second, pltpu.roll(cum, 1, 0), cum)
                else:
                    pieces = []
                    for v in range(chunk // SUBLANES):
                        first_row = v * SUBLANES
                        if 2 * h <= SUBLANES:
                            piece = row_on_sublanes(first_row + h - 1)
                            for b in range(1, SUBLANES // (2 * h)):
                                piece = jnp.where(sub >= 2 * h * b,
                                                  row_on_sublanes(first_row + 2 * h * b + h - 1), piece)
                        elif first_row % (2 * h) == 0:
                            piece = row_on_sublanes(first_row + h - 1)
                        pieces.append(piece)
                    boundary = jnp.concatenate(pieces, axis=0)
                decay = jnp.exp2(-jnp.abs(cum - boundary))
                w = (jnp.where(second, q, k) * decay).astype(BF16)
                scores = jnp.where(lvl == lv, _dot_nt(w, w), scores)
            last = row_on_sublanes(chunk - 1)
            st = st_ref[hd]
            o = _dot(scores.astype(BF16), vb)
            o = o + _dot_nt((q * jnp.exp2(cum)).astype(BF16), st.astype(BF16))
            kd = (k * jnp.exp2(jnp.concatenate([last] * (chunk // SUBLANES), axis=0) - cum)).astype(BF16)
            st_ref[hd] = st * jnp.exp2(last[0:1, :]) + _dot_tn(vb, kd)
            ms = jnp.mean(o * o, axis=-1, keepdims=True)
            o = o * lax.rsqrt(ms + EPS) * ng * gt_ref[rows, vs].astype(F32)
            y_ref[rows, vs] = o.astype(BF16)


def _gated_recurrence(q, k, g, v, gate, norm_g, bsz, heads):
    t, kw = q.shape
    vw = v.shape[1]
    dk, dv = kw // heads, vw // heads
    tt = RECUR_TILE
    per_seq = (t // bsz) // tt
    row = lambda b, i: (b * per_seq + i, 0)
    lvl = jnp.asarray(_level_index(RECUR_CHUNK))
    return pl.pallas_call(
        functools.partial(_recur_kernel, chunk=RECUR_CHUNK, heads=heads, dk=dk, dv=dv),
        grid=(bsz, per_seq),
        in_specs=[
            pl.BlockSpec(lvl.shape, lambda b, i: (0, 0)),
            pl.BlockSpec((tt, kw), row), pl.BlockSpec((tt, kw), row), pl.BlockSpec((tt, kw), row),
            pl.BlockSpec((tt, vw), row), pl.BlockSpec((tt, vw), row),
            pl.BlockSpec((1, dv), lambda b, i: (0, 0)),
        ],
        out_specs=pl.BlockSpec((tt, vw), row),
        out_shape=jax.ShapeDtypeStruct((t, vw), BF16),
        scratch_shapes=[pltpu.VMEM((heads, dv, dk), F32)],
        compiler_params=_params(("arbitrary", "arbitrary")),
        name="gated_recurrence",
    )(lvl, q, k, g, v, gate, norm_g)


def _gelu_tanh(x):
    return 0.5 * x * (1.0 + jnp.tanh(math.sqrt(2.0 / math.pi) * (x + 0.044715 * (x * x * x))))


def _ffn_tail(x, mod_ref, nf_ref, w1_ref, w3_ref, w2_ref):
    h = _norm_modulate(x, nf_ref[...], mod_ref[0, 3:4, :], mod_ref[0, 4:5, :]).astype(BF16)
    acc = None
    for j in range(w1_ref.shape[1] // FFN_TILE):
        cols = slice(j * FFN_TILE, (j + 1) * FFN_TILE)
        a = _dot(h, w1_ref[:, cols])
        b = _dot(h, w3_ref[:, cols])
        part = _dot((_silu(a) * b).astype(BF16), w2_ref[cols, :])
        acc = part if acc is None else acc + part
    return x + mod_ref[0, 5:6, :] * acc


def _block_even_kernel(x_ref, y2_ref, u_ref, yb_ref, mod_ref, sd_ref, wg_ref, bg_ref, wo_ref,
                       nf_ref, w1_ref, w3_ref, w2_ref, o_ref, tok_ref):
    for rows in _sub_tiles(x_ref.shape[0], BLOCK_SUB_TILES):
        _to_token_rows(y2_ref, tok_ref, rows)
        y = jnp.concatenate([tok_ref[j, rows, :] for j in range(tok_ref.shape[0])], axis=1)
        y = y + sd_ref[...] * u_ref[rows, :].astype(F32)
        y = _gelu_tanh(y)
        ya = y * jax.nn.sigmoid(_dot(y.astype(BF16), wg_ref[...]) + bg_ref[...])
        sw = ya.shape[1]
        mixed = _dot(ya.astype(BF16), wo_ref[0:sw, :]) + _dot(yb_ref[rows, :], wo_ref[sw:, :])
        x = x_ref[rows, :] + mod_ref[0, 2:3, :] * mixed
        o_ref[rows, :] = _ffn_tail(x, mod_ref, nf_ref, w1_ref, w3_ref, w2_ref)


def _block_odd_kernel(x_ref, yc_ref, mod_ref, wo_ref, nf_ref, w1_ref, w3_ref, w2_ref, fg_ref,
                      o_ref, *, final_norm):
    for rows in _sub_tiles(x_ref.shape[0], BLOCK_SUB_TILES):
        x = x_ref[rows, :] + mod_ref[0, 2:3, :] * _dot(yc_ref[rows, :], wo_ref[...])
        x = _ffn_tail(x, mod_ref, nf_ref, w1_ref, w3_ref, w2_ref)
        if final_norm:
            ms = jnp.mean(x * x, axis=-1, keepdims=True)
            x = x * lax.rsqrt(ms + EPS) * fg_ref[...]
        o_ref[rows, :] = x


def _block_even(x2, y2, u, yb, mod, s5_d, w_glu, b_glu, w_out, nf, w1t, w3t, w2t, tiles_per_batch):
    t, d = x2.shape
    tb = TOKEN_TILE
    row = lambda i: (i, 0)
    sw = u.shape[1]
    groups, _, k = y2.shape
    return pl.pallas_call(
        _block_even_kernel,
        grid=(t // tb,),
        in_specs=[
            pl.BlockSpec((tb, d), row),
            pl.BlockSpec((groups, tb * GRANULE // k, k), lambda i: (0, i, 0)),
            pl.BlockSpec((tb, sw), row),
            pl.BlockSpec((tb, yb.shape[1]), row),
            pl.BlockSpec((1, N_MOD, d), lambda i: (i // tiles_per_batch, 0, 0)),
            _const_spec(s5_d.shape), _const_spec(w_glu.shape), _const_spec(b_glu.shape),
            _const_spec(w_out.shape), _const_spec(nf.shape),
            _const_spec(w1t.shape), _const_spec(w3t.shape), _const_spec(w2t.shape),
        ],
        out_specs=pl.BlockSpec((tb, d), row),
        out_shape=jax.ShapeDtypeStruct((t, d), F32),
        scratch_shapes=[pltpu.VMEM((sw // LANES, tb, LANES), F32)],
        compiler_params=_params(("arbitrary",)),
        name="block_even",
    )(x2, y2, u, yb, mod, s5_d, w_glu, b_glu, w_out, nf, w1t, w3t, w2t)


def _block_odd(x2, yc, mod, w_out, nf, w1t, w3t, w2t, final_g, final_norm, tiles_per_batch):
    t, d = x2.shape
    tb = TOKEN_TILE
    row = lambda i: (i, 0)
    return pl.pallas_call(
        functools.partial(_block_odd_kernel, final_norm=final_norm),
        grid=(t // tb,),
        in_specs=[
            pl.BlockSpec((tb, d), row), pl.BlockSpec((tb, yc.shape[1]), row),
            pl.BlockSpec((1, N_MOD, d), lambda i: (i // tiles_per_batch, 0, 0)),
            _const_spec(w_out.shape), _const_spec(nf.shape),
            _const_spec(w1t.shape), _const_spec(w3t.shape), _const_spec(w2t.shape),
            _const_spec(final_g.shape),
        ],
        out_specs=pl.BlockSpec((tb, d), row),
        out_shape=jax.ShapeDtypeStruct((t, d), F32),
        compiler_params=_params(("arbitrary",)),
        name="block_odd",
    )(x2, yc, mod, w_out, nf, w1t, w3t, w2t, final_g)


def kernel(x, c, ada_w, ada_b, norm_mix_g, norm_ffn_g, ev_w_in, ev_w_out, s5_lam_re, s5_lam_im, s5_b_re, s5_b_im, s5_c_re, s5_c_im, s5_d, s5_log_step, s5_w_glu, s5_b_glu, hg_lb_logits, hg_norm_g, od_w_in, od_w_a1, od_w_a2, od_b_a, gla_norm_g, od_w_out, ffn_w1, ffn_w3, ffn_w2, final_norm_g):
    bsz, seq, d = x.shape
    depth = ada_w.shape[0]
    assert depth % 2 == 0, "the final norm is fused into the last (odd) layer"
    t = bsz * seq
    tiles_per_batch = seq // TOKEN_TILE
    x2 = x.reshape(t, d)
    mod = _adaln(c, ada_w, ada_b).reshape(depth, bsz, N_MOD, d)

    for layer in range(depth):
        nm = norm_mix_g[layer].reshape(1, d)
        nf = norm_ffn_g[layer].reshape(1, d)
        ffn_f32 = (ffn_w1[layer], ffn_w3[layer], ffn_w2[layer])
        if layer % 2 == 0:
            e = layer // 2
            u, u2, hq, hk, hgl, hv, hgate, w1t, w3t, w2t = _inproj_even(
                x2, mod[layer], nm, ev_w_in[e].astype(BF16), hg_lb_logits, layer, ffn_f32,
                tiles_per_batch)
            sw = u.shape[1]
            y2 = _s5_mix(u2, s5_lam_re[e], s5_lam_im[e], s5_b_re[e], s5_b_im[e], s5_c_re[e],
                         s5_c_im[e], s5_log_step[e], seq // S5_CHUNK)
            yb = _gated_recurrence(hq, hk, hgl, hv, hgate, hg_norm_g[e].reshape(1, -1), bsz, HG_HEADS)
            x2 = _block_even(x2, y2, u, yb, mod[layer], s5_d[e].reshape(1, sw),
                             s5_w_glu[e].astype(BF16), s5_b_glu[e].reshape(1, sw),
                             ev_w_out[e].astype(BF16), nf, w1t, w3t, w2t, tiles_per_batch)
        else:
            o = layer // 2
            kw = od_w_a2.shape[-1]
            vw = od_w_out.shape[1]
            rank = od_w_a1.shape[-1]
            w_a1 = jnp.pad(od_w_a1[o], ((0, 0), (0, LANES - rank))).astype(BF16)
            w_a2 = jnp.pad(od_w_a2[o], ((0, LANES - rank), (0, 0))).astype(BF16)
            q, k, g, v, gate, w1t, w3t, w2t = _inproj_odd(
                x2, mod[layer], nm, od_w_in[o].astype(BF16), w_a1, w_a2, od_b_a[o].reshape(1, kw),
                kw, vw, (kw // GLA_HEADS) ** -0.5, ffn_f32, tiles_per_batch)
            yc = _gated_recurrence(q, k, g, v, gate, gla_norm_g[o].reshape(1, -1), bsz, GLA_HEADS)
            x2 = _block_odd(x2, yc, mod[layer], od_w_out[o].astype(BF16), nf, w1t, w3t, w2t,
                            final_norm_g.reshape(1, d), layer == depth - 1, tiles_per_batch)
    return x2.reshape(bsz, seq, d)
```

```python
import functools
import math

import numpy as np
import jax
import jax.numpy as jnp
from jax import lax
from jax.experimental import pallas as pl
from jax.experimental.pallas import tpu as pltpu

F32 = jnp.float32
BF16 = jnp.bfloat16
EPS = 1e-6
LOG2E = math.log2(math.e)

HG_HEADS = 4
GLA_HEADS = 4
GLA_GATE_NORM = 16.0
N_MOD = 6

VMEM_LIMIT_BYTES = 56 * 1024 * 1024
LANES = 128
SUBLANES = 8
BF16_ROWS = 16
GRANULE = 16
GRANULES = LANES // GRANULE
RELAYOUT_ROWS = 32

TOKEN_TILE = 1024
RECUR_TILE = 512
RECUR_CHUNK = 128
S5_CHUNK = 32
FFN_TILE = 256
BLOCK_SUB_TILES = 2


def _dot(a, b):
    return jnp.dot(a, b, preferred_element_type=F32)


def _dot_nt(a, b):
    return lax.dot_general(a, b, (((1,), (1,)), ((), ())), preferred_element_type=F32)


def _dot_tn(a, b):
    return lax.dot_general(a, b, (((0,), (0,)), ((), ())), preferred_element_type=F32)


def _silu(x):
    return x * jax.nn.sigmoid(x)


def _params(semantics):
    return pltpu.CompilerParams(dimension_semantics=semantics, vmem_limit_bytes=VMEM_LIMIT_BYTES)


def _const_spec(shape):
    nd = len(shape)
    return pl.BlockSpec(shape, lambda *_: (0,) * nd, pipeline_mode=pl.Buffered(1))


def _sub_tiles(n_rows, n_sub):
    step = n_rows // n_sub
    return [slice(i * step, (i + 1) * step) for i in range(n_sub)]


def _cast_specs(stacked, layer, n_steps):
    in_specs, out_specs = [], []
    for w in stacked:
        _, n_rows, n_cols = w.shape
        rows = next(r for r in range(BF16_ROWS, n_rows + 1, BF16_ROWS)
                    if n_rows % r == 0 and r * n_steps >= n_rows)
        last = n_rows // rows - 1
        in_specs.append(pl.BlockSpec((None, rows, n_cols),
                                     lambda i, last=last: (layer, jnp.minimum(i, last), 0)))
        out_specs.append(pl.BlockSpec((rows, n_cols), lambda i, last=last: (jnp.minimum(i, last), 0)))
    return in_specs, out_specs


def _cast_weights(f32_refs, bf16_refs):
    for src, dst in zip(f32_refs, bf16_refs):
        dst[...] = src[...].astype(BF16)


def _prefix_matrix(chunk):
    tri = np.tril(np.ones((chunk, chunk), np.float32))
    return np.concatenate([tri, tri, tri], axis=1)


def _chunk_cumsum(g, tri3):
    chunk = tri3.shape[0]
    hi = g.astype(BF16)
    r1 = g - hi.astype(F32)
    mid = r1.astype(BF16)
    lo = (r1 - mid.astype(F32)).astype(BF16)
    out = []
    for c in range(g.shape[0] // chunk):
        rows = slice(c * chunk, (c + 1) * chunk)
        out.append(_dot(tri3, jnp.concatenate([hi[rows], mid[rows], lo[rows]], axis=0)))
    return jnp.concatenate(out, axis=0)


def _norm_modulate(x, norm_g, shift, scale):
    ms = jnp.mean(x * x, axis=-1, keepdims=True)
    h = x * lax.rsqrt(ms + EPS) * norm_g
    return h * (1.0 + scale) + shift


def _adaln_kernel(ct_ref, w_ref, b_ref, o_ref):
    cond = _silu(ct_ref[...])
    w = w_ref[0]
    for b in range(cond.shape[1]):
        o_ref[0, b:b + 1, :] = jnp.sum(cond[:, b:b + 1] * w, axis=0, keepdims=True) + b_ref[0]


def _adaln(c, ada_w, ada_b):
    depth, d, n = ada_w.shape
    bsz = c.shape[0]
    nt = 1024
    return pl.pallas_call(
        _adaln_kernel,
        grid=(depth, n // nt),
        in_specs=[
            pl.BlockSpec((d, bsz), lambda l, j: (0, 0)),
            pl.BlockSpec((1, d, nt), lambda l, j: (l, 0, j)),
            pl.BlockSpec((1, 1, nt), lambda l, j: (l, 0, j)),
        ],
        out_specs=pl.BlockSpec((1, bsz, nt), lambda l, j: (l, 0, j)),
        out_shape=jax.ShapeDtypeStruct((depth, bsz, n), F32),
        compiler_params=_params(("arbitrary", "arbitrary")),
        name="adaln_mod",
    )(c.T, ada_w, ada_b.reshape(depth, 1, n))


def _granule_transpose(blocks, slot):
    blocks = list(blocks)
    for d in (4, 2, 1):
        upper = (slot & d) != 0
        for a in range(GRANULES):
            if a & d:
                continue
            lo, hi = blocks[a], blocks[a + d]
            blocks[a] = jnp.where(upper, pltpu.roll(hi, GRANULE * d, 1), lo)
            blocks[a + d] = jnp.where(upper, hi, pltpu.roll(lo, LANES - GRANULE * d, 1))
    return blocks


def _to_chunk_rows(tok_ref, out_ref, tok_rows):
    n_grp, _, k = out_ref.shape
    chunk = k // GRANULE
    n_rows = min(RELAYOUT_ROWS, (tok_rows.stop - tok_rows.start) // chunk)
    slot = lax.broadcasted_iota(jnp.int32, (n_rows, LANES), 1) // GRANULE
    for r0 in range(tok_rows.start // chunk, tok_rows.stop // chunk, n_rows):
        for tb in range(chunk // GRANULES):
            for j in range(n_grp // GRANULES):
                rows = [tok_ref[j, pl.ds(r0 * chunk + GRANULES * tb + tp, n_rows, stride=chunk), :]
                        for tp in range(GRANULES)]
                for gp, blk in enumerate(_granule_transpose(rows, slot)):
                    out_ref[GRANULES * j + gp, r0:r0 + n_rows, tb * LANES:(tb + 1) * LANES] = (
                        blk.astype(out_ref.dtype))


def _to_token_rows(grp_ref, tok_ref, tok_rows):
    n_grp, _, k = grp_ref.shape
    chunk = k // GRANULE
    n_rows = min(RELAYOUT_ROWS, (tok_rows.stop - tok_rows.start) // chunk)
    slot = lax.broadcasted_iota(jnp.int32, (n_rows, LANES), 1) // GRANULE
    for r0 in range(tok_rows.start // chunk, tok_rows.stop // chunk, n_rows):
        for tb in range(chunk // GRANULES):
            for j in range(n_grp // GRANULES):
                cols = [grp_ref[GRANULES * j + gp, r0:r0 + n_rows, tb * LANES:(tb + 1) * LANES].astype(F32)
                        for gp in range(GRANULES)]
                for tp, blk in enumerate(_granule_transpose(cols, slot)):
                    tok_ref[j, pl.ds(r0 * chunk + GRANULES * tb + tp, n_rows, stride=chunk), :] = blk


def _inproj_even_kernel(x_ref, mod_ref, ng_ref, w_ref, lbl_ref, tri_ref, f1_ref, f3_ref, f2_ref,
                        u_ref, u2_ref, q_ref, k_ref, g_ref, v_ref, gt_ref, b1_ref, b3_ref, b2_ref,
                        tok_ref, *, lb_row):
    _cast_weights((f1_ref, f3_ref, f2_ref), (b1_ref, b3_ref, b2_ref))
    w = u_ref.shape[1]
    lg = lbl_ref[...]
    e = jnp.exp(lg - jnp.max(lg, axis=0, keepdims=True))
    lb = jnp.sum(e[:lb_row + 1], axis=0, keepdims=True) / jnp.sum(e, axis=0, keepdims=True)
    h = _norm_modulate(x_ref[...], ng_ref[...], mod_ref[0, 0:1, :], mod_ref[0, 1:2, :])
    hb = h.astype(BF16)
    part = lambda p: _dot(hb, w_ref[:, p * w:(p + 1) * w])
    rows = slice(0, x_ref.shape[0])
    z = part(0)
    for j in range(tok_ref.shape[0]):
        tok_ref[j] = z[:, j * LANES:(j + 1) * LANES]
    _to_chunk_rows(tok_ref, u2_ref, rows)
    u_ref[...] = z.astype(BF16)
    q_ref[...] = _silu(part(1)).astype(BF16)
    f = part(2)
    k_ref[...] = ((1.0 - lb) * jax.nn.sigmoid(-f)).astype(BF16)
    g_ref[...] = _chunk_cumsum(jnp.log(lb + (1.0 - lb) * jax.nn.sigmoid(f)) * LOG2E, tri_ref[...])
    v_ref[...] = part(3).astype(BF16)
    gt_ref[...] = _silu(part(4)).astype(BF16)


def _inproj_even(x2, mod, norm_g, w_in, lb_logits, layer, ffn_f32, tiles_per_batch):
    lb_row = layer
    t, d = x2.shape
    w = w_in.shape[1] // 5
    tb = TOKEN_TILE
    row = lambda i: (i, 0)
    out_bf = jax.ShapeDtypeStruct((t, w), BF16)
    groups = w // GRANULE
    k = S5_CHUNK * GRANULE
    tok = pl.BlockSpec((tb, w), row)
    tri3 = jnp.asarray(_prefix_matrix(RECUR_CHUNK), BF16)
    cast_in, cast_out = _cast_specs(ffn_f32, layer, t // tb)
    return pl.pallas_call(
        functools.partial(_inproj_even_kernel, lb_row=lb_row),
        grid=(t // tb,),
        in_specs=[
            pl.BlockSpec((tb, d), row),
            pl.BlockSpec((1, N_MOD, d), lambda i: (i // tiles_per_batch, 0, 0)),
            _const_spec((1, d)),
            _const_spec(w_in.shape),
            _const_spec(lb_logits.shape),
            _const_spec(tri3.shape),
        ] + cast_in,
        out_specs=[tok, pl.BlockSpec((groups, tb // S5_CHUNK, k), lambda i: (0, i, 0)),
                   tok, tok, tok, tok, tok] + cast_out,
        out_shape=[out_bf, jax.ShapeDtypeStruct((groups, t // S5_CHUNK, k), BF16),
                   out_bf, out_bf, jax.ShapeDtypeStruct((t, w), F32), out_bf, out_bf]
                  + [jax.ShapeDtypeStruct(a.shape[1:], BF16) for a in ffn_f32],
        scratch_shapes=[pltpu.VMEM((w // LANES, tb, LANES), F32)],
        compiler_params=_params(("arbitrary",)),
        name="inproj_even",
    )(x2, mod, norm_g, w_in, lb_logits, tri3, *ffn_f32)


def _inproj_odd_kernel(x_ref, mod_ref, ng_ref, w_ref, wa1_ref, wa2_ref, ba_ref, tri_ref,
                       f1_ref, f3_ref, f2_ref,
                       q_ref, k_ref, g_ref, v_ref, gt_ref, b1_ref, b3_ref, b2_ref, *, q_scale):
    _cast_weights((f1_ref, f3_ref, f2_ref), (b1_ref, b3_ref, b2_ref))
    kw = q_ref.shape[1]
    vw = v_ref.shape[1]
    h = _norm_modulate(x_ref[...], ng_ref[...], mod_ref[0, 0:1, :], mod_ref[0, 1:2, :])
    hb = h.astype(BF16)
    q_ref[...] = (_dot(hb, w_ref[:, 0:kw]) * q_scale).astype(BF16)
    k_ref[...] = _dot(hb, w_ref[:, kw:2 * kw]).astype(BF16)
    v_ref[...] = _dot(hb, w_ref[:, 2 * kw:2 * kw + vw]).astype(BF16)
    gt_ref[...] = _silu(_dot(hb, w_ref[:, 2 * kw + vw:2 * kw + 2 * vw])).astype(BF16)
    a1 = _dot(hb, wa1_ref[...]).astype(BF16)
    za = _dot(a1, wa2_ref[...]) + ba_ref[...]
    log_sig = jnp.minimum(za, 0.0) - jnp.log(1.0 + jnp.exp(-jnp.abs(za)))
    g_ref[...] = _chunk_cumsum(log_sig * (LOG2E / GLA_GATE_NORM), tri_ref[...])


def _inproj_odd(x2, mod, norm_g, w_in, w_a1, w_a2, b_a, kw, vw, q_scale, layer, ffn_f32,
                tiles_per_batch):
    t, d = x2.shape
    tb = TOKEN_TILE
    row = lambda i: (i, 0)
    tri3 = jnp.asarray(_prefix_matrix(RECUR_CHUNK), BF16)
    cast_in, cast_out = _cast_specs(ffn_f32, layer, t // tb)
    return pl.pallas_call(
        functools.partial(_inproj_odd_kernel, q_scale=q_scale),
        grid=(t // tb,),
        in_specs=[
            pl.BlockSpec((tb, d), row),
            pl.BlockSpec((1, N_MOD, d), lambda i: (i // tiles_per_batch, 0, 0)),
            _const_spec((1, d)),
            _const_spec(w_in.shape),
            _const_spec(w_a1.shape),
            _const_spec(w_a2.shape),
            _const_spec(b_a.shape),
            _const_spec(tri3.shape),
        ] + cast_in,
        out_specs=[pl.BlockSpec((tb, kw), row), pl.BlockSpec((tb, kw), row), pl.BlockSpec((tb, kw), row),
                   pl.BlockSpec((tb, vw), row), pl.BlockSpec((tb, vw), row)] + cast_out,
        out_shape=[jax.ShapeDtypeStruct((t, kw), BF16), jax.ShapeDtypeStruct((t, kw), BF16),
                   jax.ShapeDtypeStruct((t, kw), F32), jax.ShapeDtypeStruct((t, vw), BF16),
                   jax.ShapeDtypeStruct((t, vw), BF16)]
                  + [jax.ShapeDtypeStruct(a.shape[1:], BF16) for a in ffn_f32],
        compiler_params=_params(("arbitrary",)),
        name="inproj_odd",
    )(x2, mod, norm_g, w_in, w_a1, w_a2, b_a, tri3, *ffn_f32)


def _s5_kernel(u_ref, mt_ref, wsr_ref, wsi_ref, wir_ref, wii_ref, sr_ref, si_ref, y_ref,
               *, chunks_per_seq):
    u = u_ref[0]
    v_re = _dot(u, wsr_ref[0])
    v_im = _dot(u, wsi_ref[0])
    n_in_seq = lax.broadcasted_iota(jnp.int32, v_re.shape, 0) & (chunks_per_seq - 1)
    for lv in range(sr_ref.shape[1]):
        step = 1 << lv
        keep = n_in_seq >= step
        s_re = jnp.where(keep, pltpu.roll(v_re, step, 0), 0.0)
        s_im = jnp.where(keep, pltpu.roll(v_im, step, 0), 0.0)
        a_re, a_im = sr_ref[0, lv:lv + 1, :], si_ref[0, lv:lv + 1, :]
        v_re, v_im = v_re + a_re * s_re - a_im * s_im, v_im + a_re * s_im + a_im * s_re
    keep = n_in_seq >= 1
    x_re = jnp.where(keep, pltpu.roll(v_re, 1, 0), 0.0).astype(BF16)
    x_im = jnp.where(keep, pltpu.roll(v_im, 1, 0), 0.0).astype(BF16)
    y = _dot(u, mt_ref[0]) + _dot_nt(x_re, wir_ref[0]) + _dot_nt(x_im, wii_ref[0])
    y_ref[0] = y.astype(BF16)


def _s5_prep_kernel(bt_re_ref, bt_im_ref, c_re_ref, c_im_ref, lp_re_ref, lp_im_ref,
                    mt_ref, wsr_ref, wsi_ref, wir_ref, wii_ref, toep_ref, *, chunk):
    hg, p = bt_re_ref.shape[1:]
    k = chunk * hg
    hp = lax.Precision.HIGHEST

    tau = lax.broadcasted_iota(jnp.int32, (chunk, p), 0)
    pw_re = jnp.ones((chunk, p), F32)
    pw_im = jnp.zeros((chunk, p), F32)
    for b in range(lp_re_ref.shape[1]):
        l_re, l_im = lp_re_ref[0, b:b + 1, :], lp_im_ref[0, b:b + 1, :]
        bit = ((tau >> b) & 1) == 1
        pw_re, pw_im = (jnp.where(bit, pw_re * l_re - pw_im * l_im, pw_re),
                        jnp.where(bit, pw_re * l_im + pw_im * l_re, pw_im))

    def rows_of_powers(exponent):
        pick = (exponent == lax.broadcasted_iota(jnp.int32, (k, chunk), 1)).astype(F32)
        return (jnp.dot(pick, pw_re, precision=hp, preferred_element_type=F32),
                jnp.dot(pick, pw_im, precision=hp, preferred_element_type=F32))

    t = lax.broadcasted_iota(jnp.int32, (k, chunk), 0) // hg
    tile = lambda x: jnp.concatenate([x] * chunk, axis=0)
    b_re, b_im = tile(bt_re_ref[0]), tile(bt_im_ref[0])
    c_re, c_im = tile(c_re_ref[0]), tile(c_im_ref[0])

    p_re, p_im = rows_of_powers(t)
    z_re, z_im = c_re * p_re - c_im * p_im, c_re * p_im + c_im * p_re
    nt = (((1,), (1,)), ((), ()))
    kt = (lax.dot_general(bt_re_ref[0], z_re, nt, precision=hp, preferred_element_type=F32)
          - lax.dot_general(bt_im_ref[0], z_im, nt, precision=hp, preferred_element_type=F32))
    col_t = lax.broadcasted_iota(jnp.int32, (chunk, k), 1) // hg
    row_s = lax.broadcasted_iota(jnp.int32, (chunk, k), 0)
    for hi in range(hg):
        rows = jnp.broadcast_to(kt[hi:hi + 1, :], (chunk, k))
        rows = pltpu.roll(rows, 0, 1, stride=hg, stride_axis=0)
        rows = jnp.where(col_t >= row_s, rows, 0.0)
        for j in range(k // LANES):
            toep_ref[j, pl.ds(hi, chunk, stride=hg), :] = rows[:, j * LANES:(j + 1) * LANES]
    for j in range(k // LANES):
        mt_ref[0, :, j * LANES:(j + 1) * LANES] = toep_ref[j].astype(BF16)
    p_re, p_im = rows_of_powers(chunk - 1 - t)
    wsr_ref[0] = (b_re * p_re - b_im * p_im).astype(BF16)
    wsi_ref[0] = (b_re * p_im + b_im * p_re).astype(BF16)
    l_re, l_im = lp_re_ref[0, 0:1, :], lp_im_ref[0, 0:1, :]
    wir_ref[0] = (z_re * l_re - z_im * l_im).astype(BF16)
    wii_ref[0] = (-(z_re * l_im + z_im * l_re)).astype(BF16)


def _s5_discretise(lam_re, lam_im, b_re, b_im, log_step, chunk, chunks_per_seq):
    delta = jnp.exp(log_step)[:, None]
    ld_re, ld_im = lam_re * delta, lam_im * delta

    def powers(exponents):
        e = jnp.asarray(exponents, F32)[None, :, None]
        mag = jnp.exp(ld_re[:, None, :] * e)
        ang = ld_im[:, None, :] * e
        return mag * jnp.cos(ang), mag * jnp.sin(ang)

    lp_re, lp_im = powers([1 << b for b in range(chunk.bit_length())])
    nr, ni = lp_re[:, 0] - 1.0, lp_im[:, 0]
    den = lam_re * lam_re + lam_im * lam_im
    s_re = (nr * lam_re + ni * lam_im) / den
    s_im = (ni * lam_re - nr * lam_im) / den
    bt_re = (s_re[:, :, None] * b_re - s_im[:, :, None] * b_im).transpose(0, 2, 1)
    bt_im = (s_re[:, :, None] * b_im + s_im[:, :, None] * b_re).transpose(0, 2, 1)
    n_lev = int(math.log2(chunks_per_seq))
    sc_re, sc_im = powers([chunk * (1 << lv) for lv in range(n_lev)])
    return bt_re, bt_im, lp_re, lp_im, sc_re, sc_im


def _s5_mix(u2, lam_re, lam_im, b_re, b_im, c_re, c_im, log_step, chunks_per_seq):
    g, r, k = u2.shape
    chunk = k // GRANULE
    p = lam_re.shape[1]
    bt_re, bt_im, lp_re, lp_im, sc_re, sc_im = _s5_discretise(
        lam_re, lam_im, b_re, b_im, log_step, chunk, chunks_per_seq)
    grp = lambda i: (i, 0, 0)
    small = lambda a: pl.BlockSpec((1,) + a.shape[1:], grp)
    kp = jax.ShapeDtypeStruct((g, k, p), BF16)
    ops = pl.pallas_call(
        functools.partial(_s5_prep_kernel, chunk=chunk),
        grid=(g,),
        in_specs=[small(bt_re), small(bt_im), small(c_re), small(c_im), small(lp_re), small(lp_im)],
        out_specs=[pl.BlockSpec((1, k, k), grp)] + [pl.BlockSpec((1, k, p), grp)] * 4,
        out_shape=[jax.ShapeDtypeStruct((g, k, k), BF16), kp, kp, kp, kp],
        scratch_shapes=[pltpu.VMEM((k // LANES, k, LANES), F32)],
        compiler_params=_params(("arbitrary",)),
        name="s5_prep",
    )(bt_re, bt_im, c_re, c_im, lp_re, lp_im)
    return pl.pallas_call(
        functools.partial(_s5_kernel, chunks_per_seq=chunks_per_seq),
        grid=(g,),
        in_specs=[pl.BlockSpec((1, r, k), grp), pl.BlockSpec((1, k, k), grp)]
                 + [pl.BlockSpec((1, k, p), grp)] * 4 + [small(sc_re), small(sc_im)],
        out_specs=pl.BlockSpec((1, r, k), grp),
        out_shape=jax.ShapeDtypeStruct((g, r, k), BF16),
        compiler_params=_params(("arbitrary",)),
        name="s5_mix",
    )(u2, *ops, sc_re, sc_im)


def _level_index(chunk):
    i = np.arange(chunk)[:, None]
    j = np.arange(chunk)[None, :]
    x = np.bitwise_xor(i, j)
    lvl = np.floor(np.log2(np.maximum(x, 1))).astype(np.int32)
    return np.where(i > j, lvl, np.where(i == j, -1, -2)).astype(np.int32)


def _recur_kernel(lvl_ref, q_ref, k_ref, c_ref, v_ref, gt_ref, ng_ref, y_ref, st_ref,
                  *, chunk, heads, dk, dv):
    @pl.when(pl.program_id(1) == 0)
    def _():
        st_ref[...] = jnp.zeros(st_ref.shape, F32)

    n_chunks = q_ref.shape[0] // chunk
    n_levels = int(math.log2(chunk))
    lvl = lvl_ref[...]
    row = lax.broadcasted_iota(jnp.int32, (chunk, dk), 0)
    ng = ng_ref[...]

    sub = lax.broadcasted_iota(jnp.int32, (SUBLANES, dk), 0)

    for c in range(n_chunks):
        r0 = c * chunk
        rows = slice(r0, r0 + chunk)
        for hd in range(heads):
            ks = slice(hd * dk, (hd + 1) * dk)
            vs = slice(hd * dv, (hd + 1) * dv)
            qb = q_ref[rows, ks]
            kb = k_ref[rows, ks]
            vb = v_ref[rows, vs]
            q = qb.astype(F32)
            k = kb.astype(F32)
            cum = c_ref[rows, ks]

            def row_on_sublanes(i):
                return jnp.broadcast_to(c_ref[pl.ds(r0 + i, 1), ks], (SUBLANES, dk))

            scores = jnp.where(lvl == -1, _dot_nt(qb, kb), 0.0)
            for lv in range(n_levels):
                h = 1 << lv
                second = (row & h) != 0
                if h == 1:
                    boundary = jnp.where(second, pltpu.roll(cum, 1, 0), cum)
                else:
                    pieces = []
                    for v in range(chunk // SUBLANES):
                        first_row = v * SUBLANES
                        if 2 * h <= SUBLANES:
                            piece = row_on_sublanes(first_row + h - 1)
                            for b in range(1, SUBLANES // (2 * h)):
                                piece = jnp.where(sub >= 2 * h * b,
                                                  row_on_sublanes(first_row + 2 * h * b + h - 1), piece)
                        elif first_row % (2 * h) == 0:
                            piece = row_on_sublanes(first_row + h - 1)
                        pieces.append(piece)
                    boundary = jnp.concatenate(pieces, axis=0)
                decay = jnp.exp2(-jnp.abs(cum - boundary))
                w = (jnp.where(second, q, k) * decay).astype(BF16)
                scores = jnp.where(lvl == lv, _dot_nt(w, w), scores)
            last = row_on_sublanes(chunk - 1)
            st = st_ref[hd]
            o = _dot(scores.astype(BF16), vb)
            o = o + _dot_nt((q * jnp.exp2(cum)).astype(BF16), st.astype(BF16))
            kd = (k * jnp.exp2(jnp.concatenate([last] * (chunk // SUBLANES), axis=0) - cum)).astype(BF16)
            st_ref[hd] = st * jnp.exp2(last[0:1, :]) + _dot_tn(vb, kd)
            ms = jnp.mean(o * o, axis=-1, keepdims=True)
            o = o * lax.rsqrt(ms + EPS) * ng * gt_ref[rows, vs].astype(F32)
            y_ref[rows, vs] = o.astype(BF16)


def _gated_recurrence(q, k, g, v, gate, norm_g, bsz, heads):
    t, kw = q.shape
    vw = v.shape[1]
    dk, dv = kw // heads, vw // heads
    tt = RECUR_TILE
    per_seq = (t // bsz) // tt
    row = lambda b, i: (b * per_seq + i, 0)
    lvl = jnp.asarray(_level_index(RECUR_CHUNK))
    return pl.pallas_call(
        functools.partial(_recur_kernel, chunk=RECUR_CHUNK, heads=heads, dk=dk, dv=dv),
        grid=(bsz, per_seq),
        in_specs=[
            pl.BlockSpec(lvl.shape, lambda b, i: (0, 0)),
            pl.BlockSpec((tt, kw), row), pl.BlockSpec((tt, kw), row), pl.BlockSpec((tt, kw), row),
            pl.BlockSpec((tt, vw), row), pl.BlockSpec((tt, vw), row),
            pl.BlockSpec((1, dv), lambda b, i: (0, 0)),
        ],
        out_specs=pl.BlockSpec((tt, vw), row),
        out_shape=jax.ShapeDtypeStruct((t, vw), BF16),
        scratch_shapes=[pltpu.VMEM((heads, dv, dk), F32)],
        compiler_params=_params(("arbitrary", "arbitrary")),
        name="gated_recurrence",
    )(lvl, q, k, g, v, gate, norm_g)


def _gelu_tanh(x):
    return 0.5 * x * (1.0 + jnp.tanh(math.sqrt(2.0 / math.pi) * (x + 0.044715 * (x * x * x))))


def _ffn_tail(x, mod_ref, nf_ref, w1_ref, w3_ref, w2_ref):
    h = _norm_modulate(x, nf_ref[...], mod_ref[0, 3:4, :], mod_ref[0, 4:5, :]).astype(BF16)
    acc = None
    for j in range(w1_ref.shape[1] // FFN_TILE):
        cols = slice(j * FFN_TILE, (j + 1) * FFN_TILE)
        a = _dot(h, w1_ref[:, cols])
        b = _dot(h, w3_ref[:, cols])
        part = _dot((_silu(a) * b).astype(BF16), w2_ref[cols, :])
        acc = part if acc is None else acc + part
    return x + mod_ref[0, 5:6, :] * acc


def _block_even_kernel(x_ref, y2_ref, u_ref, yb_ref, mod_ref, sd_ref, wg_ref, bg_ref, wo_ref,
                       nf_ref, w1_ref, w3_ref, w2_ref, o_ref, tok_ref):
    for rows in _sub_tiles(x_ref.shape[0], BLOCK_SUB_TILES):
        _to_token_rows(y2_ref, tok_ref, rows)
        y = jnp.concatenate([tok_ref[j, rows, :] for j in range(tok_ref.shape[0])], axis=1)
        y = y + sd_ref[...] * u_ref[rows, :].astype(F32)
        y = _gelu_tanh(y)
        ya = y * jax.nn.sigmoid(_dot(y.astype(BF16), wg_ref[...]) + bg_ref[...])
        sw = ya.shape[1]
        mixed = _dot(ya.astype(BF16), wo_ref[0:sw, :]) + _dot(yb_ref[rows, :], wo_ref[sw:, :])
        x = x_ref[rows, :] + mod_ref[0, 2:3, :] * mixed
        o_ref[rows, :] = _ffn_tail(x, mod_ref, nf_ref, w1_ref, w3_ref, w2_ref)


def _block_odd_kernel(x_ref, yc_ref, mod_ref, wo_ref, nf_ref, w1_ref, w3_ref, w2_ref, fg_ref,
                      o_ref, *, final_norm):
    for rows in _sub_tiles(x_ref.shape[0], BLOCK_SUB_TILES):
        x = x_ref[rows, :] + mod_ref[0, 2:3, :] * _dot(yc_ref[rows, :], wo_ref[...])
        x = _ffn_tail(x, mod_ref, nf_ref, w1_ref, w3_ref, w2_ref)
        if final_norm:
            ms = jnp.mean(x * x, axis=-1, keepdims=True)
            x = x * lax.rsqrt(ms + EPS) * fg_ref[...]
        o_ref[rows, :] = x


def _block_even(x2, y2, u, yb, mod, s5_d, w_glu, b_glu, w_out, nf, w1t, w3t, w2t, tiles_per_batch):
    t, d = x2.shape
    tb = TOKEN_TILE
    row = lambda i: (i, 0)
    sw = u.shape[1]
    groups, _, k = y2.shape
    return pl.pallas_call(
        _block_even_kernel,
        grid=(t // tb,),
        in_specs=[
            pl.BlockSpec((tb, d), row),
            pl.BlockSpec((groups, tb * GRANULE // k, k), lambda i: (0, i, 0)),
            pl.BlockSpec((tb, sw), row),
            pl.BlockSpec((tb, yb.shape[1]), row),
            pl.BlockSpec((1, N_MOD, d), lambda i: (i // tiles_per_batch, 0, 0)),
            _const_spec(s5_d.shape), _const_spec(w_glu.shape), _const_spec(b_glu.shape),
            _const_spec(w_out.shape), _const_spec(nf.shape),
            _const_spec(w1t.shape), _const_spec(w3t.shape), _const_spec(w2t.shape),
        ],
        out_specs=pl.BlockSpec((tb, d), row),
        out_shape=jax.ShapeDtypeStruct((t, d), F32),
        scratch_shapes=[pltpu.VMEM((sw // LANES, tb, LANES), F32)],
        compiler_params=_params(("arbitrary",)),
        name="block_even",
    )(x2, y2, u, yb, mod, s5_d, w_glu, b_glu, w_out, nf, w1t, w3t, w2t)


def _block_odd(x2, yc, mod, w_out, nf, w1t, w3t, w2t, final_g, final_norm, tiles_per_batch):
    t, d = x2.shape
    tb = TOKEN_TILE
    row = lambda i: (i, 0)
    return pl.pallas_call(
        functools.partial(_block_odd_kernel, final_norm=final_norm),
        grid=(t // tb,),
        in_specs=[
            pl.BlockSpec((tb, d), row), pl.BlockSpec((tb, yc.shape[1]), row),
            pl.BlockSpec((1, N_MOD, d), lambda i: (i // tiles_per_batch, 0, 0)),
            _const_spec(w_out.shape), _const_spec(nf.shape),
            _const_spec(w1t.shape), _const_spec(w3t.shape), _const_spec(w2t.shape),
            _const_spec(final_g.shape),
        ],
        out_specs=pl.BlockSpec((tb, d), row),
        out_shape=jax.ShapeDtypeStruct((t, d), F32),
        compiler_params=_params(("arbitrary",)),
        name="block_odd",
    )(x2, yc, mod, w_out, nf, w1t, w3t, w2t, final_g)


def kernel(x, c, ada_w, ada_b, norm_mix_g, norm_ffn_g, ev_w_in, ev_w_out, s5_lam_re, s5_lam_im, s5_b_re, s5_b_im, s5_c_re, s5_c_im, s5_d, s5_log_step, s5_w_glu, s5_b_glu, hg_lb_logits, hg_norm_g, od_w_in, od_w_a1, od_w_a2, od_b_a, gla_norm_g, od_w_out, ffn_w1, ffn_w3, ffn_w2, final_norm_g):
    bsz, seq, d = x.shape
    depth = ada_w.shape[0]
    assert depth % 2 == 0, "the final norm is fused into the last (odd) layer"
    t = bsz * seq
    tiles_per_batch = seq // TOKEN_TILE
    x2 = x.reshape(t, d)
    mod = _adaln(c, ada_w, ada_b).reshape(depth, bsz, N_MOD, d)

    for layer in range(depth):
        nm = norm_mix_g[layer].reshape(1, d)
        nf = norm_ffn_g[layer].reshape(1, d)
        ffn_f32 = (ffn_w1, ffn_w3, ffn_w2)
        if layer % 2 == 0:
            e = layer // 2
            u, u2, hq, hk, hgl, hv, hgate, w1t, w3t, w2t = _inproj_even(
                x2, mod[layer], nm, ev_w_in[e].astype(BF16), hg_lb_logits, layer, ffn_f32,
                tiles_per_batch)
            sw = u.shape[1]
            y2 = _s5_mix(u2, s5_lam_re[e], s5_lam_im[e], s5_b_re[e], s5_b_im[e], s5_c_re[e],
                         s5_c_im[e], s5_log_step[e], seq // S5_CHUNK)
            yb = _gated_recurrence(hq, hk, hgl, hv, hgate, hg_norm_g[e].reshape(1, -1), bsz, HG_HEADS)
            x2 = _block_even(x2, y2, u, yb, mod[layer], s5_d[e].reshape(1, sw),
                             s5_w_glu[e].astype(BF16), s5_b_glu[e].reshape(1, sw),
                             ev_w_out[e].astype(BF16), nf, w1t, w3t, w2t, tiles_per_batch)
        else:
            o = layer // 2
            kw = od_w_a2.shape[-1]
            vw = od_w_out.shape[1]
            rank = od_w_a1.shape[-1]
            w_a1 = jnp.pad(od_w_a1[o], ((0, 0), (0, LANES - rank))).astype(BF16)
            w_a2 = jnp.pad(od_w_a2[o], ((0, LANES - rank), (0, 0))).astype(BF16)
            q, k, g, v, gate, w1t, w3t, w2t = _inproj_odd(
                x2, mod[layer], nm, od_w_in[o].astype(BF16), w_a1, w_a2, od_b_a[o].reshape(1, kw),
                kw, vw, (kw // GLA_HEADS) ** -0.5, layer, ffn_f32, tiles_per_batch)
            yc = _gated_recurrence(q, k, g, v, gate, gla_norm_g[o].reshape(1, -1), bsz, GLA_HEADS)
            x2 = _block_odd(x2, yc, mod[layer], od_w_out[o].astype(BF16), nf, w1t, w3t, w2t,
                            final_norm_g.reshape(1, d), layer == depth - 1, tiles_per_batch)
    return x2.reshape(bsz, seq, d)
```

```python
import functools
import math

import numpy as np
import jax
import jax.numpy as jnp
from jax import lax
from jax.experimental import pallas as pl
from jax.experimental.pallas import tpu as pltpu

F32 = jnp.float32
BF16 = jnp.bfloat16
EPS = 1e-6
LOG2E = math.log2(math.e)

HG_HEADS = 4
GLA_HEADS = 4
GLA_GATE_NORM = 16.0
N_MOD = 6

VMEM_LIMIT_BYTES = 56 * 1024 * 1024
LANES = 128
SUBLANES = 8
BF16_ROWS = 16
GRANULE = 16
GRANULES = LANES // GRANULE
RELAYOUT_ROWS = 32

TOKEN_TILE = 1024
RECUR_TILE = 512
RECUR_CHUNK = 128
S5_CHUNK = 32
FFN_TILE = 256
BLOCK_SUB_TILES = 2


def _dot(a, b):
    return jnp.dot(a, b, preferred_element_type=F32)


def _dot_nt(a, b):
    return lax.dot_general(a, b, (((1,), (1,)), ((), ())), preferred_element_type=F32)


def _dot_tn(a, b):
    return lax.dot_general(a, b, (((0,), (0,)), ((), ())), preferred_element_type=F32)


def _silu(x):
    return x * jax.nn.sigmoid(x)


def _params(semantics):
    return pltpu.CompilerParams(dimension_semantics=semantics, vmem_limit_bytes=VMEM_LIMIT_BYTES)


def _const_spec(shape):
    nd = len(shape)
    return pl.BlockSpec(shape, lambda *_: (0,) * nd, pipeline_mode=pl.Buffered(1))


def _sub_tiles(n_rows, n_sub):
    step = n_rows // n_sub
    return [slice(i * step, (i + 1) * step) for i in range(n_sub)]


def _cast_specs(stacked, layer, n_steps):
    in_specs, out_specs = [], []
    for w in stacked:
        _, n_rows, n_cols = w.shape
        rows = next(r for r in range(BF16_ROWS, n_rows + 1, BF16_ROWS)
                    if n_rows % r == 0 and r * n_steps >= n_rows)
        last = n_rows // rows - 1
        in_specs.append(pl.BlockSpec((None, rows, n_cols),
                                     lambda i, last=last: (layer, jnp.minimum(i, last), 0)))
        out_specs.append(pl.BlockSpec((rows, n_cols), lambda i, last=last: (jnp.minimum(i, last), 0)))
    return in_specs, out_specs


def _cast_weights(f32_refs, bf16_refs):
    for src, dst in zip(f32_refs, bf16_refs):
        dst[...] = src[...].astype(BF16)


def _prefix_matrix(chunk):
    tri = np.tril(np.ones((chunk, chunk), np.float32))
    return np.concatenate([tri, tri, tri], axis=1)


def _chunk_cumsum(g, tri3):
    chunk = tri3.shape[0]
    hi = g.astype(BF16)
    r1 = g - hi.astype(F32)
    mid = r1.astype(BF16)
    lo = (r1 - mid.astype(F32)).astype(BF16)
    out = []
    for c in range(g.shape[0] // chunk):
        rows = slice(c * chunk, (c + 1) * chunk)
        out.append(_dot(tri3, jnp.concatenate([hi[rows], mid[rows], lo[rows]], axis=0)))
    return jnp.concatenate(out, axis=0)


def _norm_modulate(x, norm_g, shift, scale):
    ms = jnp.mean(x * x, axis=-1, keepdims=True)
    h = x * lax.rsqrt(ms + EPS) * norm_g
    return h * (1.0 + scale) + shift


def _adaln_kernel(ct_ref, w_ref, b_ref, o_ref):
    cond = _silu(ct_ref[...])
    w = w_ref[0]
    for b in range(cond.shape[1]):
        o_ref[0, b:b + 1, :] = jnp.sum(cond[:, b:b + 1] * w, axis=0, keepdims=True) + b_ref[0]


def _adaln(c, ada_w, ada_b):
    depth, d, n = ada_w.shape
    bsz = c.shape[0]
    nt = 1024
    return pl.pallas_call(
        _adaln_kernel,
        grid=(depth, n // nt),
        in_specs=[
            pl.BlockSpec((d, bsz), lambda l, j: (0, 0)),
            pl.BlockSpec((1, d, nt), lambda l, j: (l, 0, j)),
            pl.BlockSpec((1, 1, nt), lambda l, j: (l, 0, j)),
        ],
        out_specs=pl.BlockSpec((1, bsz, nt), lambda l, j: (l, 0, j)),
        out_shape=jax.ShapeDtypeStruct((depth, bsz, n), F32),
        compiler_params=_params(("arbitrary", "arbitrary")),
        name="adaln_mod",
    )(c.T, ada_w, ada_b.reshape(depth, 1, n))


def _granule_transpose(blocks, slot):
    blocks = list(blocks)
    for d in (4, 2, 1):
        upper = (slot & d) != 0
        for a in range(GRANULES):
            if a & d:
                continue
            lo, hi = blocks[a], blocks[a + d]
            blocks[a] = jnp.where(upper, pltpu.roll(hi, GRANULE * d, 1), lo)
            blocks[a + d] = jnp.where(upper, hi, pltpu.roll(lo, LANES - GRANULE * d, 1))
    return blocks


def _to_chunk_rows(tok_ref, out_ref, tok_rows):
    n_grp, _, k = out_ref.shape
    chunk = k // GRANULE
    n_rows = min(RELAYOUT_ROWS, (tok_rows.stop - tok_rows.start) // chunk)
    slot = lax.broadcasted_iota(jnp.int32, (n_rows, LANES), 1) // GRANULE
    for r0 in range(tok_rows.start // chunk, tok_rows.stop // chunk, n_rows):
        for tb in range(chunk // GRANULES):
            for j in range(n_grp // GRANULES):
                rows = [tok_ref[j, pl.ds(r0 * chunk + GRANULES * tb + tp, n_rows, stride=chunk), :]
                        for tp in range(GRANULES)]
                for gp, blk in enumerate(_granule_transpose(rows, slot)):
                    out_ref[GRANULES * j + gp, r0:r0 + n_rows, tb * LANES:(tb + 1) * LANES] = (
                        blk.astype(out_ref.dtype))


def _to_token_rows(grp_ref, tok_ref, tok_rows):
    n_grp, _, k = grp_ref.shape
    chunk = k // GRANULE
    n_rows = min(RELAYOUT_ROWS, (tok_rows.stop - tok_rows.start) // chunk)
    slot = lax.broadcasted_iota(jnp.int32, (n_rows, LANES), 1) // GRANULE
    for r0 in range(tok_rows.start // chunk, tok_rows.stop // chunk, n_rows):
        for tb in range(chunk // GRANULES):
            for j in range(n_grp // GRANULES):
                cols = [grp_ref[GRANULES * j + gp, r0:r0 + n_rows, tb * LANES:(tb + 1) * LANES].astype(F32)
                        for gp in range(GRANULES)]
                for tp, blk in enumerate(_granule_transpose(cols, slot)):
                    tok_ref[j, pl.ds(r0 * chunk + GRANULES * tb + tp, n_rows, stride=chunk), :] = blk


def _inproj_even_kernel(x_ref, mod_ref, ng_ref, w_ref, lbl_ref, tri_ref, f1_ref, f3_ref, f2_ref,
                        u_ref, u2_ref, q_ref, k_ref, g_ref, v_ref, gt_ref, b1_ref, b3_ref, b2_ref,
                        tok_ref, *, lb_row):
    _cast_weights((f1_ref, f3_ref, f2_ref), (b1_ref, b3_ref, b2_ref))
    w = u_ref.shape[1]
    lg = lbl_ref[...]
    e = jnp.exp(lg - jnp.max(lg, axis=0, keepdims=True))
    lb = jnp.sum(e[:lb_row + 1], axis=0, keepdims=True) / jnp.sum(e, axis=0, keepdims=True)
    h = _norm_modulate(x_ref[...], ng_ref[...], mod_ref[0, 0:1, :], mod_ref[0, 1:2, :])
    hb = h.astype(BF16)
    part = lambda p: _dot(hb, w_ref[:, p * w:(p + 1) * w])
    rows = slice(0, x_ref.shape[0])
    f = part(2)
    g_ref[...] = _chunk_cumsum(jnp.log(lb + (1.0 - lb) * jax.nn.sigmoid(f)) * LOG2E, tri_ref[...])
    k_ref[...] = ((1.0 - lb) * jax.nn.sigmoid(-f)).astype(BF16)
    z = part(0)
    for j in range(tok_ref.shape[0]):
        tok_ref[j] = z[:, j * LANES:(j + 1) * LANES]
    _to_chunk_rows(tok_ref, u2_ref, rows)
    u_ref[...] = z.astype(BF16)
    q_ref[...] = _silu(part(1)).astype(BF16)
    v_ref[...] = part(3).astype(BF16)
    gt_ref[...] = _silu(part(4)).astype(BF16)


def _inproj_even(x2, mod, norm_g, w_in, lb_logits, layer, ffn_f32, tiles_per_batch):
    lb_row = layer
    t, d = x2.shape
    w = w_in.shape[1] // 5
    tb = TOKEN_TILE
    row = lambda i: (i, 0)
    out_bf = jax.ShapeDtypeStruct((t, w), BF16)
    groups = w // GRANULE
    k = S5_CHUNK * GRANULE
    tok = pl.BlockSpec((tb, w), row)
    tri3 = jnp.asarray(_prefix_matrix(RECUR_CHUNK), BF16)
    cast_in, cast_out = _cast_specs(ffn_f32, layer, t // tb)
    return pl.pallas_call(
        functools.partial(_inproj_even_kernel, lb_row=lb_row),
        grid=(t // tb,),
        in_specs=[
            pl.BlockSpec((tb, d), row),
            pl.BlockSpec((1, N_MOD, d), lambda i: (i // tiles_per_batch, 0, 0)),
            _const_spec((1, d)),
            _const_spec(w_in.shape),
            _const_spec(lb_logits.shape),
            _const_spec(tri3.shape),
        ] + cast_in,
        out_specs=[tok, pl.BlockSpec((groups, tb // S5_CHUNK, k), lambda i: (0, i, 0)),
                   tok, tok, tok, tok, tok] + cast_out,
        out_shape=[out_bf, jax.ShapeDtypeStruct((groups, t // S5_CHUNK, k), BF16),
                   out_bf, out_bf, jax.ShapeDtypeStruct((t, w), F32), out_bf, out_bf]
                  + [jax.ShapeDtypeStruct(a.shape[1:], BF16) for a in ffn_f32],
        scratch_shapes=[pltpu.VMEM((w // LANES, tb, LANES), F32)],
        compiler_params=_params(("arbitrary",)),
        name="inproj_even",
    )(x2, mod, norm_g, w_in, lb_logits, tri3, *ffn_f32)


def _inproj_odd_kernel(x_ref, mod_ref, ng_ref, w_ref, wa1_ref, wa2_ref, ba_ref, tri_ref,
                       f1_ref, f3_ref, f2_ref,
                       q_ref, k_ref, g_ref, v_ref, gt_ref, b1_ref, b3_ref, b2_ref, *, q_scale):
    _cast_weights((f1_ref, f3_ref, f2_ref), (b1_ref, b3_ref, b2_ref))
    kw = q_ref.shape[1]
    vw = v_ref.shape[1]
    h = _norm_modulate(x_ref[...], ng_ref[...], mod_ref[0, 0:1, :], mod_ref[0, 1:2, :])
    hb = h.astype(BF16)
    a1 = _dot(hb, wa1_ref[...]).astype(BF16)
    za = _dot(a1, wa2_ref[...]) + ba_ref[...]
    log_sig = jnp.minimum(za, 0.0) - jnp.log(1.0 + jnp.exp(-jnp.abs(za)))
    g_ref[...] = _chunk_cumsum(log_sig * (LOG2E / GLA_GATE_NORM), tri_ref[...])
    q_ref[...] = (_dot(hb, w_ref[:, 0:kw]) * q_scale).astype(BF16)
    k_ref[...] = _dot(hb, w_ref[:, kw:2 * kw]).astype(BF16)
    v_ref[...] = _dot(hb, w_ref[:, 2 * kw:2 * kw + vw]).astype(BF16)
    gt_ref[...] = _silu(_dot(hb, w_ref[:, 2 * kw + vw:2 * kw + 2 * vw])).astype(BF16)


def _inproj_odd(x2, mod, norm_g, w_in, w_a1, w_a2, b_a, kw, vw, q_scale, layer, ffn_f32,
                tiles_per_batch):
    t, d = x2.shape
    tb = TOKEN_TILE
    row = lambda i: (i, 0)
    tri3 = jnp.asarray(_prefix_matrix(RECUR_CHUNK), BF16)
    cast_in, cast_out = _cast_specs(ffn_f32, layer, t // tb)
    return pl.pallas_call(
        functools.partial(_inproj_odd_kernel, q_scale=q_scale),
        grid=(t // tb,),
        in_specs=[
            pl.BlockSpec((tb, d), row),
            pl.BlockSpec((1, N_MOD, d), lambda i: (i // tiles_per_batch, 0, 0)),
            _const_spec((1, d)),
            _const_spec(w_in.shape),
            _const_spec(w_a1.shape),
            _const_spec(w_a2.shape),
            _const_spec(b_a.shape),
            _const_spec(tri3.shape),
        ] + cast_in,
        out_specs=[pl.BlockSpec((tb, kw), row), pl.BlockSpec((tb, kw), row), pl.BlockSpec((tb, kw), row),
                   pl.BlockSpec((tb, vw), row), pl.BlockSpec((tb, vw), row)] + cast_out,
        out_shape=[jax.ShapeDtypeStruct((t, kw), BF16), jax.ShapeDtypeStruct((t, kw), BF16),
                   jax.ShapeDtypeStruct((t, kw), F32), jax.ShapeDtypeStruct((t, vw), BF16),
                   jax.ShapeDtypeStruct((t, vw), BF16)]
                  + [jax.ShapeDtypeStruct(a.shape[1:], BF16) for a in ffn_f32],
        compiler_params=_params(("arbitrary",)),
        name="inproj_odd",
    )(x2, mod, norm_g, w_in, w_a1, w_a2, b_a, tri3, *ffn_f32)


def _s5_kernel(u_ref, mt_ref, wsr_ref, wsi_ref, wir_ref, wii_ref, sr_ref, si_ref, y_ref,
               *, chunks_per_seq):
    u = u_ref[0]
    v_re = _dot(u, wsr_ref[0])
    v_im = _dot(u, wsi_ref[0])
    n_in_seq = lax.broadcasted_iota(jnp.int32, v_re.shape, 0) & (chunks_per_seq - 1)
    for lv in range(sr_ref.shape[1]):
        step = 1 << lv
        keep = n_in_seq >= step
        s_re = jnp.where(keep, pltpu.roll(v_re, step, 0), 0.0)
        s_im = jnp.where(keep, pltpu.roll(v_im, step, 0), 0.0)
        a_re, a_im = sr_ref[0, lv:lv + 1, :], si_ref[0, lv:lv + 1, :]
        v_re, v_im = v_re + a_re * s_re - a_im * s_im, v_im + a_re * s_im + a_im * s_re
    keep = n_in_seq >= 1
    x_re = jnp.where(keep, pltpu.roll(v_re, 1, 0), 0.0).astype(BF16)
    x_im = jnp.where(keep, pltpu.roll(v_im, 1, 0), 0.0).astype(BF16)
    y = _dot(u, mt_ref[0]) + _dot_nt(x_re, wir_ref[0]) + _dot_nt(x_im, wii_ref[0])
    y_ref[0] = y.astype(BF16)


def _s5_prep_kernel(bt_re_ref, bt_im_ref, c_re_ref, c_im_ref, lp_re_ref, lp_im_ref,
                    mt_ref, wsr_ref, wsi_ref, wir_ref, wii_ref, toep_ref, *, chunk):
    hg, p = bt_re_ref.shape[1:]
    k = chunk * hg
    hp = lax.Precision.HIGHEST

    tau = lax.broadcasted_iota(jnp.int32, (chunk, p), 0)
    pw_re = jnp.ones((chunk, p), F32)
    pw_im = jnp.zeros((chunk, p), F32)
    for b in range(lp_re_ref.shape[1]):
        l_re, l_im = lp_re_ref[0, b:b + 1, :], lp_im_ref[0, b:b + 1, :]
        bit = ((tau >> b) & 1) == 1
        pw_re, pw_im = (jnp.where(bit, pw_re * l_re - pw_im * l_im, pw_re),
                        jnp.where(bit, pw_re * l_im + pw_im * l_re, pw_im))

    def rows_of_powers(exponent):
        pick = (exponent == lax.broadcasted_iota(jnp.int32, (k, chunk), 1)).astype(F32)
        return (jnp.dot(pick, pw_re, precision=hp, preferred_element_type=F32),
                jnp.dot(pick, pw_im, precision=hp, preferred_element_type=F32))

    t = lax.broadcasted_iota(jnp.int32, (k, chunk), 0) // hg
    tile = lambda x: jnp.concatenate([x] * chunk, axis=0)
    b_re, b_im = tile(bt_re_ref[0]), tile(bt_im_ref[0])
    c_re, c_im = tile(c_re_ref[0]), tile(c_im_ref[0])

    p_re, p_im = rows_of_powers(t)
    z_re, z_im = c_re * p_re - c_im * p_im, c_re * p_im + c_im * p_re
    nt = (((1,), (1,)), ((), ()))
    kt = (lax.dot_general(bt_re_ref[0], z_re, nt, precision=hp, preferred_element_type=F32)
          - lax.dot_general(bt_im_ref[0], z_im, nt, precision=hp, preferred_element_type=F32))
    col_t = lax.broadcasted_iota(jnp.int32, (chunk, k), 1) // hg
    row_s = lax.broadcasted_iota(jnp.int32, (chunk, k), 0)
    for hi in range(hg):
        rows = jnp.broadcast_to(kt[hi:hi + 1, :], (chunk, k))
        rows = pltpu.roll(rows, 0, 1, stride=hg, stride_axis=0)
        rows = jnp.where(col_t >= row_s, rows, 0.0)
        for j in range(k // LANES):
            toep_ref[j, pl.ds(hi, chunk, stride=hg), :] = rows[:, j * LANES:(j + 1) * LANES]
    for j in range(k // LANES):
        mt_ref[0, :, j * LANES:(j + 1) * LANES] = toep_ref[j].astype(BF16)
    p_re, p_im = rows_of_powers(chunk - 1 - t)
    wsr_ref[0] = (b_re * p_re - b_im * p_im).astype(BF16)
    wsi_ref[0] = (b_re * p_im + b_im * p_re).astype(BF16)
    l_re, l_im = lp_re_ref[0, 0:1, :], lp_im_ref[0, 0:1, :]
    wir_ref[0] = (z_re * l_re - z_im * l_im).astype(BF16)
    wii_ref[0] = (-(z_re * l_im + z_im * l_re)).astype(BF16)


def _s5_discretise(lam_re, lam_im, b_re, b_im, log_step, chunk, chunks_per_seq):
    delta = jnp.exp(log_step)[:, None]
    ld_re, ld_im = lam_re * delta, lam_im * delta

    def powers(exponents):
        e = jnp.asarray(exponents, F32)[None, :, None]
        mag = jnp.exp(ld_re[:, None, :] * e)
        ang = ld_im[:, None, :] * e
        return mag * jnp.cos(ang), mag * jnp.sin(ang)

    lp_re, lp_im = powers([1 << b for b in range(chunk.bit_length())])
    nr, ni = lp_re[:, 0] - 1.0, lp_im[:, 0]
    den = lam_re * lam_re + lam_im * lam_im
    s_re = (nr * lam_re + ni * lam_im) / den
    s_im = (ni * lam_re - nr * lam_im) / den
    bt_re = (s_re[:, :, None] * b_re - s_im[:, :, None] * b_im).transpose(0, 2, 1)
    bt_im = (s_re[:, :, None] * b_im + s_im[:, :, None] * b_re).transpose(0, 2, 1)
    n_lev = int(math.log2(chunks_per_seq))
    sc_re, sc_im = powers([chunk * (1 << lv) for lv in range(n_lev)])
    return bt_re, bt_im, lp_re, lp_im, sc_re, sc_im


def _s5_mix(u2, lam_re, lam_im, b_re, b_im, c_re, c_im, log_step, chunks_per_seq):
    g, r, k = u2.shape
    chunk = k // GRANULE
    p = lam_re.shape[1]
    bt_re, bt_im, lp_re, lp_im, sc_re, sc_im = _s5_discretise(
        lam_re, lam_im, b_re, b_im, log_step, chunk, chunks_per_seq)
    grp = lambda i: (i, 0, 0)
    small = lambda a: pl.BlockSpec((1,) + a.shape[1:], grp)
    kp = jax.ShapeDtypeStruct((g, k, p), BF16)
    ops = pl.pallas_call(
        functools.partial(_s5_prep_kernel, chunk=chunk),
        grid=(g,),
        in_specs=[small(bt_re), small(bt_im), small(c_re), small(c_im), small(lp_re), small(lp_im)],
        out_specs=[pl.BlockSpec((1, k, k), grp)] + [pl.BlockSpec((1, k, p), grp)] * 4,
        out_shape=[jax.ShapeDtypeStruct((g, k, k), BF16), kp, kp, kp, kp],
        scratch_shapes=[pltpu.VMEM((k // LANES, k, LANES), F32)],
        compiler_params=_params(("arbitrary",)),
        name="s5_prep",
    )(bt_re, bt_im, c_re, c_im, lp_re, lp_im)
    return pl.pallas_call(
        functools.partial(_s5_kernel, chunks_per_seq=chunks_per_seq),
        grid=(g,),
        in_specs=[pl.BlockSpec((1, r, k), grp), pl.BlockSpec((1, k, k), grp)]
                 + [pl.BlockSpec((1, k, p), grp)] * 4 + [small(sc_re), small(sc_im)],
        out_specs=pl.BlockSpec((1, r, k), grp),
        out_shape=jax.ShapeDtypeStruct((g, r, k), BF16),
        compiler_params=_params(("arbitrary",)),
        name="s5_mix",
    )(u2, *ops, sc_re, sc_im)


def _level_index(chunk):
    i = np.arange(chunk)[:, None]
    j = np.arange(chunk)[None, :]
    x = np.bitwise_xor(i, j)
    lvl = np.floor(np.log2(np.maximum(x, 1))).astype(np.int32)
    return np.where(i > j, lvl, np.where(i == j, -1, -2)).astype(np.int32)


def _recur_kernel(lvl_ref, q_ref, k_ref, c_ref, v_ref, gt_ref, ng_ref, y_ref, st_ref,
                  *, chunk, heads, dk, dv):
    @pl.when(pl.program_id(1) == 0)
    def _():
        st_ref[...] = jnp.zeros(st_ref.shape, F32)

    n_chunks = q_ref.shape[0] // chunk
    n_levels = int(math.log2(chunk))
    lvl = lvl_ref[...]
    row = lax.broadcasted_iota(jnp.int32, (chunk, dk), 0)
    ng = ng_ref[...]

    sub = lax.broadcasted_iota(jnp.int32, (SUBLANES, dk), 0)

    for c in range(n_chunks):
        r0 = c * chunk
        rows = slice(r0, r0 + chunk)
        for hd in range(heads):
            ks = slice(hd * dk, (hd + 1) * dk)
            vs = slice(hd * dv, (hd + 1) * dv)
            qb = q_ref[rows, ks]
            kb = k_ref[rows, ks]
            vb = v_ref[rows, vs]
            q = qb.astype(F32)
            k = kb.astype(F32)
            cum = c_ref[rows, ks]

            def row_on_sublanes(i):
                return jnp.broadcast_to(c_ref[pl.ds(r0 + i, 1), ks], (SUBLANES, dk))

            scores = jnp.where(lvl == -1, _dot_nt(qb, kb), 0.0)
            for lv in range(n_levels):
                h = 1 << lv
                second = (row & h) != 0
                if h == 1:
                    boundary = jnp.where(second, pltpu.roll(cum, 1, 0), cum)
                else:
                    pieces = []
                    for v in range(chunk // SUBLANES):
                        first_row = v * SUBLANES
                        if 2 * h <= SUBLANES:
                            piece = row_on_sublanes(first_row + h - 1)
                            for b in range(1, SUBLANES // (2 * h)):
                                piece = jnp.where(sub >= 2 * h * b,
                                                  row_on_sublanes(first_row + 2 * h * b + h - 1), piece)
                        elif first_row % (2 * h) == 0:
                            piece = row_on_sublanes(first_row + h - 1)
                        pieces.append(piece)
                    boundary = jnp.concatenate(pieces, axis=0)
                decay = jnp.exp2(-jnp.abs(cum - boundary))
                w = (jnp.where(second, q, k) * decay).astype(BF16)
                scores = jnp.where(lvl == lv, _dot_nt(w, w), scores)
            last = row_on_sublanes(chunk - 1)
            st = st_ref[hd]
            o = _dot(scores.astype(BF16), vb)
            o = o + _dot_nt((q * jnp.exp2(cum)).astype(BF16), st.astype(BF16))
            kd = (k * jnp.exp2(jnp.concatenate([last] * (chunk // SUBLANES), axis=0) - cum)).astype(BF16)
            st_ref[hd] = st * jnp.exp2(last[0:1, :]) + _dot_tn(vb, kd)
            ms = jnp.mean(o * o, axis=-1, keepdims=True)
            o = o * lax.rsqrt(ms + EPS) * ng * gt_ref[rows, vs].astype(F32)
            y_ref[rows, vs] = o.astype(BF16)


def _gated_recurrence(q, k, g, v, gate, norm_g, bsz, heads):
    t, kw = q.shape
    vw = v.shape[1]
    dk, dv = kw // heads, vw // heads
    tt = RECUR_TILE
    per_seq = (t // bsz) // tt
    row = lambda b, i: (b * per_seq + i, 0)
    lvl = jnp.asarray(_level_index(RECUR_CHUNK))
    return pl.pallas_call(
        functools.partial(_recur_kernel, chunk=RECUR_CHUNK, heads=heads, dk=dk, dv=dv),
        grid=(bsz, per_seq),
        in_specs=[
            pl.BlockSpec(lvl.shape, lambda b, i: (0, 0)),
            pl.BlockSpec((tt, kw), row), pl.BlockSpec((tt, kw), row), pl.BlockSpec((tt, kw), row),
            pl.BlockSpec((tt, vw), row), pl.BlockSpec((tt, vw), row),
            pl.BlockSpec((1, dv), lambda b, i: (0, 0)),
        ],
        out_specs=pl.BlockSpec((tt, vw), row),
        out_shape=jax.ShapeDtypeStruct((t, vw), BF16),
        scratch_shapes=[pltpu.VMEM((heads, dv, dk), F32)],
        compiler_params=_params(("arbitrary", "arbitrary")),
        name="gated_recurrence",
    )(lvl, q, k, g, v, gate, norm_g)


def _gelu_tanh(x):
    return 0.5 * x * (1.0 + jnp.tanh(math.sqrt(2.0 / math.pi) * (x + 0.044715 * (x * x * x))))


def _ffn_tail(x, mod_ref, nf_ref, w1_ref, w3_ref, w2_ref):
    h = _norm_modulate(x, nf_ref[...], mod_ref[0, 3:4, :], mod_ref[0, 4:5, :]).astype(BF16)
    acc = None
    for j in range(w1_ref.shape[1] // FFN_TILE):
        cols = slice(j * FFN_TILE, (j + 1) * FFN_TILE)
        a = _dot(h, w1_ref[:, cols])
        b = _dot(h, w3_ref[:, cols])
        part = _dot((_silu(a) * b).astype(BF16), w2_ref[cols, :])
        acc = part if acc is None else acc + part
    return x + mod_ref[0, 5:6, :] * acc


def _block_even_kernel(x_ref, y2_ref, u_ref, yb_ref, mod_ref, sd_ref, wg_ref, bg_ref, wo_ref,
                       nf_ref, w1_ref, w3_ref, w2_ref, o_ref, tok_ref):
    for rows in _sub_tiles(x_ref.shape[0], BLOCK_SUB_TILES):
        _to_token_rows(y2_ref, tok_ref, rows)
        y = jnp.concatenate([tok_ref[j, rows, :] for j in range(tok_ref.shape[0])], axis=1)
        y = y + sd_ref[...] * u_ref[rows, :].astype(F32)
        y = _gelu_tanh(y)
        ya = y * jax.nn.sigmoid(_dot(y.astype(BF16), wg_ref[...]) + bg_ref[...])
        sw = ya.shape[1]
        mixed = _dot(ya.astype(BF16), wo_ref[0:sw, :]) + _dot(yb_ref[rows, :], wo_ref[sw:, :])
        x = x_ref[rows, :] + mod_ref[0, 2:3, :] * mixed
        o_ref[rows, :] = _ffn_tail(x, mod_ref, nf_ref, w1_ref, w3_ref, w2_ref)


def _block_odd_kernel(x_ref, yc_ref, mod_ref, wo_ref, nf_ref, w1_ref, w3_ref, w2_ref, fg_ref,
                      o_ref, *, final_norm):
    for rows in _sub_tiles(x_ref.shape[0], BLOCK_SUB_TILES):
        x = x_ref[rows, :] + mod_ref[0, 2:3, :] * _dot(yc_ref[rows, :], wo_ref[...])
        x = _ffn_tail(x, mod_ref, nf_ref, w1_ref, w3_ref, w2_ref)
        if final_norm:
            ms = jnp.mean(x * x, axis=-1, keepdims=True)
            x = x * lax.rsqrt(ms + EPS) * fg_ref[...]
        o_ref[rows, :] = x


def _block_even(x2, y2, u, yb, mod, s5_d, w_glu, b_glu, w_out, nf, w1t, w3t, w2t, tiles_per_batch):
    t, d = x2.shape
    tb = TOKEN_TILE
    row = lambda i: (i, 0)
    sw = u.shape[1]
    groups, _, k = y2.shape
    return pl.pallas_call(
        _block_even_kernel,
        grid=(t // tb,),
        in_specs=[
            pl.BlockSpec((tb, d), row),
            pl.BlockSpec((groups, tb * GRANULE // k, k), lambda i: (0, i, 0)),
            pl.BlockSpec((tb, sw), row),
            pl.BlockSpec((tb, yb.shape[1]), row),
            pl.BlockSpec((1, N_MOD, d), lambda i: (i // tiles_per_batch, 0, 0)),
            _const_spec(s5_d.shape), _const_spec(w_glu.shape), _const_spec(b_glu.shape),
            _const_spec(w_out.shape), _const_spec(nf.shape),
            _const_spec(w1t.shape), _const_spec(w3t.shape), _const_spec(w2t.shape),
        ],
        out_specs=pl.BlockSpec((tb, d), row),
        out_shape=jax.ShapeDtypeStruct((t, d), F32),
        scratch_shapes=[pltpu.VMEM((sw // LANES, tb, LANES), F32)],
        compiler_params=_params(("arbitrary",)),
        name="block_even",
    )(x2, y2, u, yb, mod, s5_d, w_glu, b_glu, w_out, nf, w1t, w3t, w2t)


def _block_odd(x2, yc, mod, w_out, nf, w1t, w3t, w2t, final_g, final_norm, tiles_per_batch):
    t, d = x2.shape
    tb = TOKEN_TILE
    row = lambda i: (i, 0)
    return pl.pallas_call(
        functools.partial(_block_odd_kernel, final_norm=final_norm),
        grid=(t // tb,),
        in_specs=[
            pl.BlockSpec((tb, d), row), pl.BlockSpec((tb, yc.shape[1]), row),
            pl.BlockSpec((1, N_MOD, d), lambda i: (i // tiles_per_batch, 0, 0)),
            _const_spec(w_out.shape), _const_spec(nf.shape),
            _const_spec(w1t.shape), _const_spec(w3t.shape), _const_spec(w2t.shape),
            _const_spec(final_g.shape),
        ],
        out_specs=pl.BlockSpec((tb, d), row),
        out_shape=jax.ShapeDtypeStruct((t, d), F32),
        compiler_params=_params(("arbitrary",)),
        name="block_odd",
    )(x2, yc, mod, w_out, nf, w1t, w3t, w2t, final_g)


def kernel(x, c, ada_w, ada_b, norm_mix_g, norm_ffn_g, ev_w_in, ev_w_out, s5_lam_re, s5_lam_im, s5_b_re, s5_b_im, s5_c_re, s5_c_im, s5_d, s5_log_step, s5_w_glu, s5_b_glu, hg_lb_logits, hg_norm_g, od_w_in, od_w_a1, od_w_a2, od_b_a, gla_norm_g, od_w_out, ffn_w1, ffn_w3, ffn_w2, final_norm_g):
    bsz, seq, d = x.shape
    depth = ada_w.shape[0]
    assert depth % 2 == 0, "the final norm is fused into the last (odd) layer"
    t = bsz * seq
    tiles_per_batch = seq // TOKEN_TILE
    x2 = x.reshape(t, d)
    mod = _adaln(c, ada_w, ada_b).reshape(depth, bsz, N_MOD, d)

    for layer in range(depth):
        nm = norm_mix_g[layer].reshape(1, d)
        nf = norm_ffn_g[layer].reshape(1, d)
        ffn_f32 = (ffn_w1, ffn_w3, ffn_w2)
        if layer % 2 == 0:
            e = layer // 2
            u, u2, hq, hk, hgl, hv, hgate, w1t, w3t, w2t = _inproj_even(
                x2, mod[layer], nm, ev_w_in[e].astype(BF16), hg_lb_logits, layer, ffn_f32,
                tiles_per_batch)
            sw = u.shape[1]
            y2 = _s5_mix(u2, s5_lam_re[e], s5_lam_im[e], s5_b_re[e], s5_b_im[e], s5_c_re[e],
                         s5_c_im[e], s5_log_step[e], seq // S5_CHUNK)
            yb = _gated_recurrence(hq, hk, hgl, hv, hgate, hg_norm_g[e].reshape(1, -1), bsz, HG_HEADS)
            x2 = _block_even(x2, y2, u, yb, mod[layer], s5_d[e].reshape(1, sw),
                             s5_w_glu[e].astype(BF16), s5_b_glu[e].reshape(1, sw),
                             ev_w_out[e].astype(BF16), nf, w1t, w3t, w2t, tiles_per_batch)
        else:
            o = layer // 2
            kw = od_w_a2.shape[-1]
            vw = od_w_out.shape[1]
            rank = od_w_a1.shape[-1]
            w_a1 = jnp.pad(od_w_a1[o], ((0, 0), (0, LANES - rank))).astype(BF16)
            w_a2 = jnp.pad(od_w_a2[o], ((0, LANES - rank), (0, 0))).astype(BF16)
            q, k, g, v, gate, w1t, w3t, w2t = _inproj_odd(
                x2, mod[layer], nm, od_w_in[o].astype(BF16), w_a1, w_a2, od_b_a[o].reshape(1, kw),
                kw, vw, (kw // GLA_HEADS) ** -0.5, layer, ffn_f32, tiles_per_batch)
            yc = _gated_recurrence(q, k, g, v, gate, gla_norm_g[o].reshape(1, -1), bsz, GLA_HEADS)
            x2 = _block_odd(x2, yc, mod[layer], od_w_out[o].astype(BF16), nf, w1t, w3t, w2t,
                            final_norm_g.reshape(1, d), layer == depth - 1, tiles_per_batch)
    return x2.reshape(bsz, seq, d)
```

```python
import functools
import math

import numpy as np
import jax
import jax.numpy as jnp
from jax import lax
from jax.experimental import pallas as pl
from jax.experimental.pallas import tpu as pltpu

F32 = jnp.float32
BF16 = jnp.bfloat16
EPS = 1e-6
LOG2E = math.log2(math.e)

HG_HEADS = 4
GLA_HEADS = 4
GLA_GATE_NORM = 16.0
N_MOD = 6

VMEM_LIMIT_BYTES = 56 * 1024 * 1024
LANES = 128
SUBLANES = 8
BF16_ROWS = 16
GRANULE = 16
GRANULES = LANES // GRANULE
RELAYOUT_ROWS = 32

TOKEN_TILE = 1024
RECUR_TILE = 1024
RECUR_CHUNK = 128
S5_CHUNK = 32
FFN_TILE = 256
BLOCK_SUB_TILES = 2


def _dot(a, b):
    return jnp.dot(a, b, preferred_element_type=F32)


def _dot_nt(a, b):
    return lax.dot_general(a, b, (((1,), (1,)), ((), ())), preferred_element_type=F32)


def _dot_tn(a, b):
    return lax.dot_general(a, b, (((0,), (0,)), ((), ())), preferred_element_type=F32)


def _silu(x):
    return x * jax.nn.sigmoid(x)


def _params(semantics):
    return pltpu.CompilerParams(dimension_semantics=semantics, vmem_limit_bytes=VMEM_LIMIT_BYTES)


def _const_spec(shape):
    nd = len(shape)
    return pl.BlockSpec(shape, lambda *_: (0,) * nd, pipeline_mode=pl.Buffered(1))


def _sub_tiles(n_rows, n_sub):
    step = n_rows // n_sub
    return [slice(i * step, (i + 1) * step) for i in range(n_sub)]


def _cast_specs(stacked, layer, n_steps):
    in_specs, out_specs = [], []
    for w in stacked:
        _, n_rows, n_cols = w.shape
        rows = next(r for r in range(BF16_ROWS, n_rows + 1, BF16_ROWS)
                    if n_rows % r == 0 and r * n_steps >= n_rows)
        last = n_rows // rows - 1
        in_specs.append(pl.BlockSpec((None, rows, n_cols),
                                     lambda i, last=last: (layer, jnp.minimum(i, last), 0)))
        out_specs.append(pl.BlockSpec((rows, n_cols), lambda i, last=last: (jnp.minimum(i, last), 0)))
    return in_specs, out_specs


def _cast_weights(f32_refs, bf16_refs):
    for src, dst in zip(f32_refs, bf16_refs):
        dst[...] = src[...].astype(BF16)


def _prefix_matrix(chunk):
    tri = np.tril(np.ones((chunk, chunk), np.float32))
    return np.concatenate([tri, tri, tri], axis=1)


def _chunk_cumsum(g, tri3):
    chunk = tri3.shape[0]
    hi = g.astype(BF16)
    r1 = g - hi.astype(F32)
    mid = r1.astype(BF16)
    lo = (r1 - mid.astype(F32)).astype(BF16)
    out = []
    for c in range(g.shape[0] // chunk):
        rows = slice(c * chunk, (c + 1) * chunk)
        out.append(_dot(tri3, jnp.concatenate([hi[rows], mid[rows], lo[rows]], axis=0)))
    return jnp.concatenate(out, axis=0)


def _norm_modulate(x, norm_g, shift, scale):
    ms = jnp.mean(x * x, axis=-1, keepdims=True)
    h = x * lax.rsqrt(ms + EPS) * norm_g
    return h * (1.0 + scale) + shift


def _adaln_kernel(ct_ref, w_ref, b_ref, o_ref):
    cond = _silu(ct_ref[...])
    w = w_ref[0]
    for b in range(cond.shape[1]):
        o_ref[0, b:b + 1, :] = jnp.sum(cond[:, b:b + 1] * w, axis=0, keepdims=True) + b_ref[0]


def _adaln(c, ada_w, ada_b):
    depth, d, n = ada_w.shape
    bsz = c.shape[0]
    nt = 1024
    return pl.pallas_call(
        _adaln_kernel,
        grid=(depth, n // nt),
        in_specs=[
            pl.BlockSpec((d, bsz), lambda l, j: (0, 0)),
            pl.BlockSpec((1, d, nt), lambda l, j: (l, 0, j)),
            pl.BlockSpec((1, 1, nt), lambda l, j: (l, 0, j)),
        ],
        out_specs=pl.BlockSpec((1, bsz, nt), lambda l, j: (l, 0, j)),
        out_shape=jax.ShapeDtypeStruct((depth, bsz, n), F32),
        compiler_params=_params(("arbitrary", "arbitrary")),
        name="adaln_mod",
    )(c.T, ada_w, ada_b.reshape(depth, 1, n))


def _granule_transpose(blocks, slot):
    blocks = list(blocks)
    for d in (4, 2, 1):
        upper = (slot & d) != 0
        for a in range(GRANULES):
            if a & d:
                continue
            lo, hi = blocks[a], blocks[a + d]
            blocks[a] = jnp.where(upper, pltpu.roll(hi, GRANULE * d, 1), lo)
            blocks[a + d] = jnp.where(upper, hi, pltpu.roll(lo, LANES - GRANULE * d, 1))
    return blocks


def _to_chunk_rows(tok_ref, out_ref, tok_rows):
    n_grp, _, k = out_ref.shape
    chunk = k // GRANULE
    n_rows = min(RELAYOUT_ROWS, (tok_rows.stop - tok_rows.start) // chunk)
    slot = lax.broadcasted_iota(jnp.int32, (n_rows, LANES), 1) // GRANULE
    for r0 in range(tok_rows.start // chunk, tok_rows.stop // chunk, n_rows):
        for tb in range(chunk // GRANULES):
            for j in range(n_grp // GRANULES):
                rows = [tok_ref[j, pl.ds(r0 * chunk + GRANULES * tb + tp, n_rows, stride=chunk), :]
                        for tp in range(GRANULES)]
                for gp, blk in enumerate(_granule_transpose(rows, slot)):
                    out_ref[GRANULES * j + gp, r0:r0 + n_rows, tb * LANES:(tb + 1) * LANES] = (
                        blk.astype(out_ref.dtype))


def _to_token_rows(grp_ref, tok_ref, tok_rows):
    n_grp, _, k = grp_ref.shape
    chunk = k // GRANULE
    n_rows = min(RELAYOUT_ROWS, (tok_rows.stop - tok_rows.start) // chunk)
    slot = lax.broadcasted_iota(jnp.int32, (n_rows, LANES), 1) // GRANULE
    for r0 in range(tok_rows.start // chunk, tok_rows.stop // chunk, n_rows):
        for tb in range(chunk // GRANULES):
            for j in range(n_grp // GRANULES):
                cols = [grp_ref[GRANULES * j + gp, r0:r0 + n_rows, tb * LANES:(tb + 1) * LANES].astype(F32)
                        for gp in range(GRANULES)]
                for tp, blk in enumerate(_granule_transpose(cols, slot)):
                    tok_ref[j, pl.ds(r0 * chunk + GRANULES * tb + tp, n_rows, stride=chunk), :] = blk


def _inproj_even_kernel(x_ref, mod_ref, ng_ref, w_ref, lbl_ref, tri_ref, f1_ref, f3_ref, f2_ref,
                        u_ref, u2_ref, q_ref, k_ref, g_ref, v_ref, gt_ref, b1_ref, b3_ref, b2_ref,
                        tok_ref, *, lb_row):
    _cast_weights((f1_ref, f3_ref, f2_ref), (b1_ref, b3_ref, b2_ref))
    w = u_ref.shape[1]
    lg = lbl_ref[...]
    e = jnp.exp(lg - jnp.max(lg, axis=0, keepdims=True))
    lb = jnp.sum(e[:lb_row + 1], axis=0, keepdims=True) / jnp.sum(e, axis=0, keepdims=True)
    h = _norm_modulate(x_ref[...], ng_ref[...], mod_ref[0, 0:1, :], mod_ref[0, 1:2, :])
    hb = h.astype(BF16)
    part = lambda p: _dot(hb, w_ref[:, p * w:(p + 1) * w])
    rows = slice(0, x_ref.shape[0])
    f = part(2)
    g_ref[...] = _chunk_cumsum(jnp.log(lb + (1.0 - lb) * jax.nn.sigmoid(f)) * LOG2E, tri_ref[...])
    k_ref[...] = ((1.0 - lb) * jax.nn.sigmoid(-f)).astype(BF16)
    z = part(0)
    for j in range(tok_ref.shape[0]):
        tok_ref[j] = z[:, j * LANES:(j + 1) * LANES]
    _to_chunk_rows(tok_ref, u2_ref, rows)
    u_ref[...] = z.astype(BF16)
    q_ref[...] = _silu(part(1)).astype(BF16)
    v_ref[...] = part(3).astype(BF16)
    gt_ref[...] = _silu(part(4)).astype(BF16)


def _inproj_even(x2, mod, norm_g, w_in, lb_logits, layer, ffn_f32, tiles_per_batch):
    lb_row = layer
    t, d = x2.shape
    w = w_in.shape[1] // 5
    tb = TOKEN_TILE
    row = lambda i: (i, 0)
    out_bf = jax.ShapeDtypeStruct((t, w), BF16)
    groups = w // GRANULE
    k = S5_CHUNK * GRANULE
    tok = pl.BlockSpec((tb, w), row)
    tri3 = jnp.asarray(_prefix_matrix(RECUR_CHUNK), BF16)
    cast_in, cast_out = _cast_specs(ffn_f32, layer, t // tb)
    return pl.pallas_call(
        functools.partial(_inproj_even_kernel, lb_row=lb_row),
        grid=(t // tb,),
        in_specs=[
            pl.BlockSpec((tb, d), row),
            pl.BlockSpec((1, N_MOD, d), lambda i: (i // tiles_per_batch, 0, 0)),
            _const_spec((1, d)),
            _const_spec(w_in.shape),
            _const_spec(lb_logits.shape),
            _const_spec(tri3.shape),
        ] + cast_in,
        out_specs=[tok, pl.BlockSpec((groups, tb // S5_CHUNK, k), lambda i: (0, i, 0)),
                   tok, tok, tok, tok, tok] + cast_out,
        out_shape=[out_bf, jax.ShapeDtypeStruct((groups, t // S5_CHUNK, k), BF16),
                   out_bf, out_bf, jax.ShapeDtypeStruct((t, w), F32), out_bf, out_bf]
                  + [jax.ShapeDtypeStruct(a.shape[1:], BF16) for a in ffn_f32],
        scratch_shapes=[pltpu.VMEM((w // LANES, tb, LANES), F32)],
        compiler_params=_params(("arbitrary",)),
        name="inproj_even",
    )(x2, mod, norm_g, w_in, lb_logits, tri3, *ffn_f32)


def _inproj_odd_kernel(x_ref, mod_ref, ng_ref, w_ref, wa1_ref, wa2_ref, ba_ref, tri_ref,
                       f1_ref, f3_ref, f2_ref,
                       q_ref, k_ref, g_ref, v_ref, gt_ref, b1_ref, b3_ref, b2_ref, *, q_scale):
    _cast_weights((f1_ref, f3_ref, f2_ref), (b1_ref, b3_ref, b2_ref))
    kw = q_ref.shape[1]
    vw = v_ref.shape[1]
    h = _norm_modulate(x_ref[...], ng_ref[...], mod_ref[0, 0:1, :], mod_ref[0, 1:2, :])
    hb = h.astype(BF16)
    a1 = _dot(hb, wa1_ref[...]).astype(BF16)
    za = _dot(a1, wa2_ref[...]) + ba_ref[...]
    log_sig = jnp.minimum(za, 0.0) - jnp.log(1.0 + jnp.exp(-jnp.abs(za)))
    g_ref[...] = _chunk_cumsum(log_sig * (LOG2E / GLA_GATE_NORM), tri_ref[...])
    q_ref[...] = (_dot(hb, w_ref[:, 0:kw]) * q_scale).astype(BF16)
    k_ref[...] = _dot(hb, w_ref[:, kw:2 * kw]).astype(BF16)
    v_ref[...] = _dot(hb, w_ref[:, 2 * kw:2 * kw + vw]).astype(BF16)
    gt_ref[...] = _silu(_dot(hb, w_ref[:, 2 * kw + vw:2 * kw + 2 * vw])).astype(BF16)


def _inproj_odd(x2, mod, norm_g, w_in, w_a1, w_a2, b_a, kw, vw, q_scale, layer, ffn_f32,
                tiles_per_batch):
    t, d = x2.shape
    tb = TOKEN_TILE
    row = lambda i: (i, 0)
    tri3 = jnp.asarray(_prefix_matrix(RECUR_CHUNK), BF16)
    cast_in, cast_out = _cast_specs(ffn_f32, layer, t // tb)
    return pl.pallas_call(
        functools.partial(_inproj_odd_kernel, q_scale=q_scale),
        grid=(t // tb,),
        in_specs=[
            pl.BlockSpec((tb, d), row),
            pl.BlockSpec((1, N_MOD, d), lambda i: (i // tiles_per_batch, 0, 0)),
            _const_spec((1, d)),
            _const_spec(w_in.shape),
            _const_spec(w_a1.shape),
            _const_spec(w_a2.shape),
            _const_spec(b_a.shape),
            _const_spec(tri3.shape),
        ] + cast_in,
        out_specs=[pl.BlockSpec((tb, kw), row), pl.BlockSpec((tb, kw), row), pl.BlockSpec((tb, kw), row),
                   pl.BlockSpec((tb, vw), row), pl.BlockSpec((tb, vw), row)] + cast_out,
        out_shape=[jax.ShapeDtypeStruct((t, kw), BF16), jax.ShapeDtypeStruct((t, kw), BF16),
                   jax.ShapeDtypeStruct((t, kw), F32), jax.ShapeDtypeStruct((t, vw), BF16),
                   jax.ShapeDtypeStruct((t, vw), BF16)]
                  + [jax.ShapeDtypeStruct(a.shape[1:], BF16) for a in ffn_f32],
        compiler_params=_params(("arbitrary",)),
        name="inproj_odd",
    )(x2, mod, norm_g, w_in, w_a1, w_a2, b_a, tri3, *ffn_f32)


def _s5_kernel(u_ref, mt_ref, wsr_ref, wsi_ref, wir_ref, wii_ref, sr_ref, si_ref, y_ref,
               *, chunks_per_seq):
    u = u_ref[0]
    v_re = _dot(u, wsr_ref[0])
    v_im = _dot(u, wsi_ref[0])
    n_in_seq = lax.broadcasted_iota(jnp.int32, v_re.shape, 0) & (chunks_per_seq - 1)
    for lv in range(sr_ref.shape[1]):
        step = 1 << lv
        keep = n_in_seq >= step
        s_re = jnp.where(keep, pltpu.roll(v_re, step, 0), 0.0)
        s_im = jnp.where(keep, pltpu.roll(v_im, step, 0), 0.0)
        a_re, a_im = sr_ref[0, lv:lv + 1, :], si_ref[0, lv:lv + 1, :]
        v_re, v_im = v_re + a_re * s_re - a_im * s_im, v_im + a_re * s_im + a_im * s_re
    keep = n_in_seq >= 1
    x_re = jnp.where(keep, pltpu.roll(v_re, 1, 0), 0.0).astype(BF16)
    x_im = jnp.where(keep, pltpu.roll(v_im, 1, 0), 0.0).astype(BF16)
    y = _dot(u, mt_ref[0]) + _dot_nt(x_re, wir_ref[0]) + _dot_nt(x_im, wii_ref[0])
    y_ref[0] = y.astype(BF16)


def _s5_prep_kernel(bt_re_ref, bt_im_ref, c_re_ref, c_im_ref, lp_re_ref, lp_im_ref,
                    mt_ref, wsr_ref, wsi_ref, wir_ref, wii_ref, toep_ref, *, chunk):
    hg, p = bt_re_ref.shape[1:]
    k = chunk * hg
    hp = lax.Precision.HIGHEST

    tau = lax.broadcasted_iota(jnp.int32, (chunk, p), 0)
    pw_re = jnp.ones((chunk, p), F32)
    pw_im = jnp.zeros((chunk, p), F32)
    for b in range(lp_re_ref.shape[1]):
        l_re, l_im = lp_re_ref[0, b:b + 1, :], lp_im_ref[0, b:b + 1, :]
        bit = ((tau >> b) & 1) == 1
        pw_re, pw_im = (jnp.where(bit, pw_re * l_re - pw_im * l_im, pw_re),
                        jnp.where(bit, pw_re * l_im + pw_im * l_re, pw_im))

    def rows_of_powers(exponent):
        pick = (exponent == lax.broadcasted_iota(jnp.int32, (k, chunk), 1)).astype(F32)
        return (jnp.dot(pick, pw_re, precision=hp, preferred_element_type=F32),
                jnp.dot(pick, pw_im, precision=hp, preferred_element_type=F32))

    t = lax.broadcasted_iota(jnp.int32, (k, chunk), 0) // hg
    tile = lambda x: jnp.concatenate([x] * chunk, axis=0)
    b_re, b_im = tile(bt_re_ref[0]), tile(bt_im_ref[0])
    c_re, c_im = tile(c_re_ref[0]), tile(c_im_ref[0])

    p_re, p_im = rows_of_powers(t)
    z_re, z_im = c_re * p_re - c_im * p_im, c_re * p_im + c_im * p_re
    nt = (((1,), (1,)), ((), ()))
    kt = (lax.dot_general(bt_re_ref[0], z_re, nt, precision=hp, preferred_element_type=F32)
          - lax.dot_general(bt_im_ref[0], z_im, nt, precision=hp, preferred_element_type=F32))
    col_t = lax.broadcasted_iota(jnp.int32, (chunk, k), 1) // hg
    row_s = lax.broadcasted_iota(jnp.int32, (chunk, k), 0)
    for hi in range(hg):
        rows = jnp.broadcast_to(kt[hi:hi + 1, :], (chunk, k))
        rows = pltpu.roll(rows, 0, 1, stride=hg, stride_axis=0)
        rows = jnp.where(col_t >= row_s, rows, 0.0)
        for j in range(k // LANES):
            toep_ref[j, pl.ds(hi, chunk, stride=hg), :] = rows[:, j * LANES:(j + 1) * LANES]
    for j in range(k // LANES):
        mt_ref[0, :, j * LANES:(j + 1) * LANES] = toep_ref[j].astype(BF16)
    p_re, p_im = rows_of_powers(chunk - 1 - t)
    wsr_ref[0] = (b_re * p_re - b_im * p_im).astype(BF16)
    wsi_ref[0] = (b_re * p_im + b_im * p_re).astype(BF16)
    l_re, l_im = lp_re_ref[0, 0:1, :], lp_im_ref[0, 0:1, :]
    wir_ref[0] = (z_re * l_re - z_im * l_im).astype(BF16)
    wii_ref[0] = (-(z_re * l_im + z_im * l_re)).astype(BF16)


def _s5_group_kernel(u_ref, bt_re_ref, bt_im_ref, c_re_ref, c_im_ref, lp_re_ref, lp_im_ref,
                     sr_ref, si_ref, y_ref, mt_ref, wsr_ref, wsi_ref, wir_ref, wii_ref, toep_ref,
                     *, chunk, chunks_per_seq):
    _s5_prep_kernel(bt_re_ref, bt_im_ref, c_re_ref, c_im_ref, lp_re_ref, lp_im_ref,
                    mt_ref, wsr_ref, wsi_ref, wir_ref, wii_ref, toep_ref, chunk=chunk)
    _s5_kernel(u_ref, mt_ref, wsr_ref, wsi_ref, wir_ref, wii_ref, sr_ref, si_ref, y_ref,
               chunks_per_seq=chunks_per_seq)


def _s5_discretise(lam_re, lam_im, b_re, b_im, log_step, chunk, chunks_per_seq):
    delta = jnp.exp(log_step)[:, None]
    ld_re, ld_im = lam_re * delta, lam_im * delta

    def powers(exponents):
        e = jnp.asarray(exponents, F32)[None, :, None]
        mag = jnp.exp(ld_re[:, None, :] * e)
        ang = ld_im[:, None, :] * e
        return mag * jnp.cos(ang), mag * jnp.sin(ang)

    lp_re, lp_im = powers([1 << b for b in range(chunk.bit_length())])
    nr, ni = lp_re[:, 0] - 1.0, lp_im[:, 0]
    den = lam_re * lam_re + lam_im * lam_im
    s_re = (nr * lam_re + ni * lam_im) / den
    s_im = (ni * lam_re - nr * lam_im) / den
    bt_re = (s_re[:, :, None] * b_re - s_im[:, :, None] * b_im).transpose(0, 2, 1)
    bt_im = (s_re[:, :, None] * b_im + s_im[:, :, None] * b_re).transpose(0, 2, 1)
    n_lev = int(math.log2(chunks_per_seq))
    sc_re, sc_im = powers([chunk * (1 << lv) for lv in range(n_lev)])
    return bt_re, bt_im, lp_re, lp_im, sc_re, sc_im


def _s5_mix(u2, lam_re, lam_im, b_re, b_im, c_re, c_im, log_step, chunks_per_seq):
    g, r, k = u2.shape
    chunk = k // GRANULE
    p = lam_re.shape[1]
    bt_re, bt_im, lp_re, lp_im, sc_re, sc_im = _s5_discretise(
        lam_re, lam_im, b_re, b_im, log_step, chunk, chunks_per_seq)
    grp = lambda i: (i, 0, 0)
    small = lambda a: pl.BlockSpec((1,) + a.shape[1:], grp)
    params = (bt_re, bt_im, c_re, c_im, lp_re, lp_im, sc_re, sc_im)
    return pl.pallas_call(
        functools.partial(_s5_group_kernel, chunk=chunk, chunks_per_seq=chunks_per_seq),
        grid=(g,),
        in_specs=[pl.BlockSpec((1, r, k), grp)] + [small(a) for a in params],
        out_specs=pl.BlockSpec((1, r, k), grp),
        out_shape=jax.ShapeDtypeStruct((g, r, k), BF16),
        scratch_shapes=[pltpu.VMEM((1, k, k), BF16)] + [pltpu.VMEM((1, k, p), BF16)] * 4
                       + [pltpu.VMEM((k // LANES, k, LANES), F32)],
        compiler_params=_params(("arbitrary",)),
        name="s5_mix",
    )(u2, *params)


def _level_index(chunk):
    i = np.arange(chunk)[:, None]
    j = np.arange(chunk)[None, :]
    x = np.bitwise_xor(i, j)
    lvl = np.floor(np.log2(np.maximum(x, 1))).astype(np.int32)
    return np.where(i > j, lvl, np.where(i == j, -1, -2)).astype(np.int32)


def _recur_kernel(lvl_ref, q_ref, k_ref, c_ref, v_ref, gt_ref, ng_ref, y_ref, st_ref,
                  *, chunk, heads, dk, dv):
    @pl.when(pl.program_id(1) == 0)
    def _():
        st_ref[...] = jnp.zeros(st_ref.shape, F32)

    n_chunks = q_ref.shape[0] // chunk
    n_levels = int(math.log2(chunk))
    lvl = lvl_ref[...]
    row = lax.broadcasted_iota(jnp.int32, (chunk, dk), 0)
    ng = ng_ref[...]

    sub = lax.broadcasted_iota(jnp.int32, (SUBLANES, dk), 0)

    for c in range(n_chunks):
        r0 = c * chunk
        rows = slice(r0, r0 + chunk)
        for hd in range(heads):
            ks = slice(hd * dk, (hd + 1) * dk)
            vs = slice(hd * dv, (hd + 1) * dv)
            qb = q_ref[rows, ks]
            kb = k_ref[rows, ks]
            vb = v_ref[rows, vs]
            q = qb.astype(F32)
            k = kb.astype(F32)
            cum = c_ref[rows, ks]

            def row_on_sublanes(i):
                return jnp.broadcast_to(c_ref[pl.ds(r0 + i, 1), ks], (SUBLANES, dk))

            scores = jnp.where(lvl == -1, _dot_nt(qb, kb), 0.0)
            for lv in range(n_levels):
                h = 1 << lv
                second = (row & h) != 0
                if h == 1:
                    boundary = jnp.where(second, pltpu.roll(cum, 1, 0), cum)
                else:
                    pieces = []
                    for v in range(chunk // SUBLANES):
                        first_row = v * SUBLANES
                        if 2 * h <= SUBLANES:
                            piece = row_on_sublanes(first_row + h - 1)
                            for b in range(1, SUBLANES // (2 * h)):
                                piece = jnp.where(sub >= 2 * h * b,
                                                  row_on_sublanes(first_row + 2 * h * b + h - 1), piece)
                        elif first_row % (2 * h) == 0:
                            piece = row_on_sublanes(first_row + h - 1)
                        pieces.append(piece)
                    boundary = jnp.concatenate(pieces, axis=0)
                decay = jnp.exp2(-jnp.abs(cum - boundary))
                w = (jnp.where(second, q, k) * decay).astype(BF16)
                scores = jnp.where(lvl == lv, _dot_nt(w, w), scores)
            last = row_on_sublanes(chunk - 1)
            st = st_ref[hd]
            o = _dot(scores.astype(BF16), vb)
            o = o + _dot_nt((q * jnp.exp2(cum)).astype(BF16), st.astype(BF16))
            kd = (k * jnp.exp2(jnp.concatenate([last] * (chunk // SUBLANES), axis=0) - cum)).astype(BF16)
            st_ref[hd] = st * jnp.exp2(last[0:1, :]) + _dot_tn(vb, kd)
            ms = jnp.mean(o * o, axis=-1, keepdims=True)
            o = o * lax.rsqrt(ms + EPS) * ng * gt_ref[rows, vs].astype(F32)
            y_ref[rows, vs] = o.astype(BF16)


def _gated_recurrence(q, k, g, v, gate, norm_g, bsz, heads):
    t, kw = q.shape
    vw = v.shape[1]
    dk, dv = kw // heads, vw // heads
    tt = RECUR_TILE
    per_seq = (t // bsz) // tt
    row = lambda b, i: (b * per_seq + i, 0)
    lvl = jnp.asarray(_level_index(RECUR_CHUNK))
    return pl.pallas_call(
        functools.partial(_recur_kernel, chunk=RECUR_CHUNK, heads=heads, dk=dk, dv=dv),
        grid=(bsz, per_seq),
        in_specs=[
            pl.BlockSpec(lvl.shape, lambda b, i: (0, 0)),
            pl.BlockSpec((tt, kw), row), pl.BlockSpec((tt, kw), row), pl.BlockSpec((tt, kw), row),
            pl.BlockSpec((tt, vw), row), pl.BlockSpec((tt, vw), row),
            pl.BlockSpec((1, dv), lambda b, i: (0, 0)),
        ],
        out_specs=pl.BlockSpec((tt, vw), row),
        out_shape=jax.ShapeDtypeStruct((t, vw), BF16),
        scratch_shapes=[pltpu.VMEM((heads, dv, dk), F32)],
        compiler_params=_params(("arbitrary", "arbitrary")),
        name="gated_recurrence",
    )(lvl, q, k, g, v, gate, norm_g)


def _gelu_tanh(x):
    return 0.5 * x * (1.0 + jnp.tanh(math.sqrt(2.0 / math.pi) * (x + 0.044715 * (x * x * x))))


def _ffn_tail(x, mod_ref, nf_ref, w1_ref, w3_ref, w2_ref):
    h = _norm_modulate(x, nf_ref[...], mod_ref[0, 3:4, :], mod_ref[0, 4:5, :]).astype(BF16)
    acc = None
    for j in range(w1_ref.shape[1] // FFN_TILE):
        cols = slice(j * FFN_TILE, (j + 1) * FFN_TILE)
        a = _dot(h, w1_ref[:, cols])
        b = _dot(h, w3_ref[:, cols])
        part = _dot((_silu(a) * b).astype(BF16), w2_ref[cols, :])
        acc = part if acc is None else acc + part
    return x + mod_ref[0, 5:6, :] * acc


def _block_even_kernel(x_ref, y2_ref, u_ref, yb_ref, mod_ref, sd_ref, wg_ref, bg_ref, wo_ref,
                       nf_ref, w1_ref, w3_ref, w2_ref, o_ref, tok_ref):
    for rows in _sub_tiles(x_ref.shape[0], BLOCK_SUB_TILES):
        _to_token_rows(y2_ref, tok_ref, rows)
        y = jnp.concatenate([tok_ref[j, rows, :] for j in range(tok_ref.shape[0])], axis=1)
        y = y + sd_ref[...] * u_ref[rows, :].astype(F32)
        y = _gelu_tanh(y)
        ya = y * jax.nn.sigmoid(_dot(y.astype(BF16), wg_ref[...]) + bg_ref[...])
        sw = ya.shape[1]
        mixed = _dot(ya.astype(BF16), wo_ref[0:sw, :]) + _dot(yb_ref[rows, :], wo_ref[sw:, :])
        x = x_ref[rows, :] + mod_ref[0, 2:3, :] * mixed
        o_ref[rows, :] = _ffn_tail(x, mod_ref, nf_ref, w1_ref, w3_ref, w2_ref)


def _block_odd_kernel(x_ref, yc_ref, mod_ref, wo_ref, nf_ref, w1_ref, w3_ref, w2_ref, fg_ref,
                      o_ref, *, final_norm):
    for rows in _sub_tiles(x_ref.shape[0], BLOCK_SUB_TILES):
        x = x_ref[rows, :] + mod_ref[0, 2:3, :] * _dot(yc_ref[rows, :], wo_ref[...])
        x = _ffn_tail(x, mod_ref, nf_ref, w1_ref, w3_ref, w2_ref)
        if final_norm:
            ms = jnp.mean(x * x, axis=-1, keepdims=True)
            x = x * lax.rsqrt(ms + EPS) * fg_ref[...]
        o_ref[rows, :] = x


def _block_even(x2, y2, u, yb, mod, s5_d, w_glu, b_glu, w_out, nf, w1t, w3t, w2t, tiles_per_batch):
    t, d = x2.shape
    tb = TOKEN_TILE
    row = lambda i: (i, 0)
    sw = u.shape[1]
    groups, _, k = y2.shape
    return pl.pallas_call(
        _block_even_kernel,
        grid=(t // tb,),
        in_specs=[
            pl.BlockSpec((tb, d), row),
            pl.BlockSpec((groups, tb * GRANULE // k, k), lambda i: (0, i, 0)),
            pl.BlockSpec((tb, sw), row),
            pl.BlockSpec((tb, yb.shape[1]), row),
            pl.BlockSpec((1, N_MOD, d), lambda i: (i // tiles_per_batch, 0, 0)),
            _const_spec(s5_d.shape), _const_spec(w_glu.shape), _const_spec(b_glu.shape),
            _const_spec(w_out.shape), _const_spec(nf.shape),
            _const_spec(w1t.shape), _const_spec(w3t.shape), _const_spec(w2t.shape),
        ],
        out_specs=pl.BlockSpec((tb, d), row),
        out_shape=jax.ShapeDtypeStruct((t, d), F32),
        scratch_shapes=[pltpu.VMEM((sw // LANES, tb, LANES), F32)],
        compiler_params=_params(("arbitrary",)),
        name="block_even",
    )(x2, y2, u, yb, mod, s5_d, w_glu, b_glu, w_out, nf, w1t, w3t, w2t)


def _block_odd(x2, yc, mod, w_out, nf, w1t, w3t, w2t, final_g, final_norm, tiles_per_batch):
    t, d = x2.shape
    tb = TOKEN_TILE
    row = lambda i: (i, 0)
    return pl.pallas_call(
        functools.partial(_block_odd_kernel, final_norm=final_norm),
        grid=(t // tb,),
        in_specs=[
            pl.BlockSpec((tb, d), row), pl.BlockSpec((tb, yc.shape[1]), row),
            pl.BlockSpec((1, N_MOD, d), lambda i: (i // tiles_per_batch, 0, 0)),
            _const_spec(w_out.shape), _const_spec(nf.shape),
            _const_spec(w1t.shape), _const_spec(w3t.shape), _const_spec(w2t.shape),
            _const_spec(final_g.shape),
        ],
        out_specs=pl.BlockSpec((tb, d), row),
        out_shape=jax.ShapeDtypeStruct((t, d), F32),
        compiler_params=_params(("arbitrary",)),
        name="block_odd",
    )(x2, yc, mod, w_out, nf, w1t, w3t, w2t, final_g)


def kernel(x, c, ada_w, ada_b, norm_mix_g, norm_ffn_g, ev_w_in, ev_w_out, s5_lam_re, s5_lam_im, s5_b_re, s5_b_im, s5_c_re, s5_c_im, s5_d, s5_log_step, s5_w_glu, s5_b_glu, hg_lb_logits, hg_norm_g, od_w_in, od_w_a1, od_w_a2, od_b_a, gla_norm_g, od_w_out, ffn_w1, ffn_w3, ffn_w2, final_norm_g):
    bsz, seq, d = x.shape
    depth = ada_w.shape[0]
    assert depth % 2 == 0, "the final norm is fused into the last (odd) layer"
    t = bsz * seq
    tiles_per_batch = seq // TOKEN_TILE
    x2 = x.reshape(t, d)
    mod = _adaln(c, ada_w, ada_b).reshape(depth, bsz, N_MOD, d)

    for layer in range(depth):
        nm = norm_mix_g[layer].reshape(1, d)
        nf = norm_ffn_g[layer].reshape(1, d)
        ffn_f32 = (ffn_w1, ffn_w3, ffn_w2)
        if layer % 2 == 0:
            e = layer // 2
            u, u2, hq, hk, hgl, hv, hgate, w1t, w3t, w2t = _inproj_even(
                x2, mod[layer], nm, ev_w_in[e].astype(BF16), hg_lb_logits, layer, ffn_f32,
                tiles_per_batch)
            sw = u.shape[1]
            y2 = _s5_mix(u2, s5_lam_re[e], s5_lam_im[e], s5_b_re[e], s5_b_im[e], s5_c_re[e],
                         s5_c_im[e], s5_log_step[e], seq // S5_CHUNK)
            yb = _gated_recurrence(hq, hk, hgl, hv, hgate, hg_norm_g[e].reshape(1, -1), bsz, HG_HEADS)
            x2 = _block_even(x2, y2, u, yb, mod[layer], s5_d[e].reshape(1, sw),
                             s5_w_glu[e].astype(BF16), s5_b_glu[e].reshape(1, sw),
                             ev_w_out[e].astype(BF16), nf, w1t, w3t, w2t, tiles_per_batch)
        else:
            o = layer // 2
            kw = od_w_a2.shape[-1]
            vw = od_w_out.shape[1]
            rank = od_w_a1.shape[-1]
            w_a1 = jnp.pad(od_w_a1[o], ((0, 0), (0, LANES - rank))).astype(BF16)
            w_a2 = jnp.pad(od_w_a2[o], ((0, LANES - rank), (0, 0))).astype(BF16)
            q, k, g, v, gate, w1t, w3t, w2t = _inproj_odd(
                x2, mod[layer], nm, od_w_in[o].astype(BF16), w_a1, w_a2, od_b_a[o].reshape(1, kw),
                kw, vw, (kw // GLA_HEADS) ** -0.5, layer, ffn_f32, tiles_per_batch)
            yc = _gated_recurrence(q, k, g, v, gate, gla_norm_g[o].reshape(1, -1), bsz, GLA_HEADS)
            x2 = _block_odd(x2, yc, mod[layer], od_w_out[o].astype(BF16), nf, w1t, w3t, w2t,
                            final_norm_g.reshape(1, d), layer == depth - 1, tiles_per_batch)
    return x2.reshape(bsz, seq, d)
```

```python
import functools
import math

import numpy as np
import jax
import jax.numpy as jnp
from jax import lax
from jax.experimental import pallas as pl
from jax.experimental.pallas import tpu as pltpu

F32 = jnp.float32
BF16 = jnp.bfloat16
EPS = 1e-6
LOG2E = math.log2(math.e)

HG_HEADS = 4
GLA_HEADS = 4
GLA_GATE_NORM = 16.0
N_MOD = 6

VMEM_LIMIT_BYTES = 56 * 1024 * 1024
LANES = 128
SUBLANES = 8
BF16_ROWS = 16
GRANULE = 16
GRANULES = LANES // GRANULE
RELAYOUT_ROWS = 32

TOKEN_TILE = 1024
RECUR_TILE = 1024
RECUR_CHUNK = 128
S5_CHUNK = 32
CHUNK_PITCH = 40
FFN_TILE = 256
BLOCK_SUB_TILES = 2


def _dot(a, b):
    return jnp.dot(a, b, preferred_element_type=F32)


def _dot_nt(a, b):
    return lax.dot_general(a, b, (((1,), (1,)), ((), ())), preferred_element_type=F32)


def _dot_tn(a, b):
    return lax.dot_general(a, b, (((0,), (0,)), ((), ())), preferred_element_type=F32)


def _silu(x):
    return x * jax.nn.sigmoid(x)


def _params(semantics):
    return pltpu.CompilerParams(dimension_semantics=semantics, vmem_limit_bytes=VMEM_LIMIT_BYTES)


def _const_spec(shape):
    nd = len(shape)
    return pl.BlockSpec(shape, lambda *_: (0,) * nd, pipeline_mode=pl.Buffered(1))


def _sub_tiles(n_rows, n_sub):
    step = n_rows // n_sub
    return [slice(i * step, (i + 1) * step) for i in range(n_sub)]


def _cast_specs(stacked, layer, n_steps):
    in_specs, out_specs = [], []
    for w in stacked:
        _, n_rows, n_cols = w.shape
        rows = next(r for r in range(BF16_ROWS, n_rows + 1, BF16_ROWS)
                    if n_rows % r == 0 and r * n_steps >= n_rows)
        last = n_rows // rows - 1
        in_specs.append(pl.BlockSpec((None, rows, n_cols),
                                     lambda i, last=last: (layer, jnp.minimum(i, last), 0)))
        out_specs.append(pl.BlockSpec((rows, n_cols), lambda i, last=last: (jnp.minimum(i, last), 0)))
    return in_specs, out_specs


def _cast_weights(f32_refs, bf16_refs):
    for src, dst in zip(f32_refs, bf16_refs):
        dst[...] = src[...].astype(BF16)


def _prefix_matrix(chunk):
    tri = np.tril(np.ones((chunk, chunk), np.float32))
    return np.concatenate([tri, tri, tri], axis=1)


def _chunk_cumsum(g, tri3):
    chunk = tri3.shape[0]
    hi = g.astype(BF16)
    r1 = g - hi.astype(F32)
    mid = r1.astype(BF16)
    lo = (r1 - mid.astype(F32)).astype(BF16)
    out = []
    for c in range(g.shape[0] // chunk):
        rows = slice(c * chunk, (c + 1) * chunk)
        out.append(_dot(tri3, jnp.concatenate([hi[rows], mid[rows], lo[rows]], axis=0)))
    return jnp.concatenate(out, axis=0)


def _norm_modulate(x, norm_g, shift, scale):
    ms = jnp.mean(x * x, axis=-1, keepdims=True)
    h = x * lax.rsqrt(ms + EPS) * norm_g
    return h * (1.0 + scale) + shift


def _adaln_kernel(ct_ref, w_ref, b_ref, o_ref):
    cond = _silu(ct_ref[...])
    w = w_ref[0]
    for b in range(cond.shape[1]):
        o_ref[0, b:b + 1, :] = jnp.sum(cond[:, b:b + 1] * w, axis=0, keepdims=True) + b_ref[0]


def _adaln(c, ada_w, ada_b):
    depth, d, n = ada_w.shape
    bsz = c.shape[0]
    nt = 1024
    return pl.pallas_call(
        _adaln_kernel,
        grid=(depth, n // nt),
        in_specs=[
            pl.BlockSpec((d, bsz), lambda l, j: (0, 0)),
            pl.BlockSpec((1, d, nt), lambda l, j: (l, 0, j)),
            pl.BlockSpec((1, 1, nt), lambda l, j: (l, 0, j)),
        ],
        out_specs=pl.BlockSpec((1, bsz, nt), lambda l, j: (l, 0, j)),
        out_shape=jax.ShapeDtypeStruct((depth, bsz, n), F32),
        compiler_params=_params(("arbitrary", "arbitrary")),
        name="adaln_mod",
    )(c.T, ada_w, ada_b.reshape(depth, 1, n))


def _granule_transpose(blocks, slot):
    blocks = list(blocks)
    for d in (4, 2, 1):
        upper = (slot & d) != 0
        for a in range(GRANULES):
            if a & d:
                continue
            lo, hi = blocks[a], blocks[a + d]
            blocks[a] = jnp.where(upper, pltpu.roll(hi, GRANULE * d, 1), lo)
            blocks[a + d] = jnp.where(upper, hi, pltpu.roll(lo, LANES - GRANULE * d, 1))
    return blocks


def _tok_rows_of_chunk(n):
    return slice(n * CHUNK_PITCH, n * CHUNK_PITCH + S5_CHUNK)


def _to_chunk_rows(tok_ref, out_ref, tok_rows):
    n_grp, _, k = out_ref.shape
    chunk = k // GRANULE
    n_rows = min(RELAYOUT_ROWS, (tok_rows.stop - tok_rows.start) // chunk)
    slot = lax.broadcasted_iota(jnp.int32, (n_rows, LANES), 1) // GRANULE
    for r0 in range(tok_rows.start // chunk, tok_rows.stop // chunk, n_rows):
        for tb in range(chunk // GRANULES):
            for j in range(n_grp // GRANULES):
                rows = [tok_ref[j, pl.ds(r0 * CHUNK_PITCH + GRANULES * tb + tp, n_rows,
                                         stride=CHUNK_PITCH), :]
                        for tp in range(GRANULES)]
                for gp, blk in enumerate(_granule_transpose(rows, slot)):
                    out_ref[GRANULES * j + gp, r0:r0 + n_rows, tb * LANES:(tb + 1) * LANES] = (
                        blk.astype(out_ref.dtype))


def _to_token_rows(grp_ref, tok_ref, tok_rows):
    n_grp, _, k = grp_ref.shape
    chunk = k // GRANULE
    n_rows = min(RELAYOUT_ROWS, (tok_rows.stop - tok_rows.start) // chunk)
    slot = lax.broadcasted_iota(jnp.int32, (n_rows, LANES), 1) // GRANULE
    for r0 in range(tok_rows.start // chunk, tok_rows.stop // chunk, n_rows):
        for tb in range(chunk // GRANULES):
            for j in range(n_grp // GRANULES):
                cols = [grp_ref[GRANULES * j + gp, r0:r0 + n_rows, tb * LANES:(tb + 1) * LANES].astype(F32)
                        for gp in range(GRANULES)]
                for tp, blk in enumerate(_granule_transpose(cols, slot)):
                    tok_ref[j, pl.ds(r0 * CHUNK_PITCH + GRANULES * tb + tp, n_rows,
                                     stride=CHUNK_PITCH), :] = blk


def _inproj_even_kernel(x_ref, mod_ref, ng_ref, w_ref, lbl_ref, tri_ref, f1_ref, f3_ref, f2_ref,
                        u_ref, u2_ref, q_ref, k_ref, g_ref, v_ref, gt_ref, b1_ref, b3_ref, b2_ref,
                        tok_ref, *, lb_row):
    _cast_weights((f1_ref, f3_ref, f2_ref), (b1_ref, b3_ref, b2_ref))
    w = u_ref.shape[1]
    lg = lbl_ref[...]
    e = jnp.exp(lg - jnp.max(lg, axis=0, keepdims=True))
    lb = jnp.sum(e[:lb_row + 1], axis=0, keepdims=True) / jnp.sum(e, axis=0, keepdims=True)
    h = _norm_modulate(x_ref[...], ng_ref[...], mod_ref[0, 0:1, :], mod_ref[0, 1:2, :])
    hb = h.astype(BF16)
    part = lambda p: _dot(hb, w_ref[:, p * w:(p + 1) * w])
    rows = slice(0, x_ref.shape[0])
    f = part(2)
    g_ref[...] = _chunk_cumsum(jnp.log(lb + (1.0 - lb) * jax.nn.sigmoid(f)) * LOG2E, tri_ref[...])
    k_ref[...] = ((1.0 - lb) * jax.nn.sigmoid(-f)).astype(BF16)
    z = part(0)
    for j in range(tok_ref.shape[0]):
        for n in range(x_ref.shape[0] // S5_CHUNK):
            tok_ref[j, _tok_rows_of_chunk(n), :] = z[n * S5_CHUNK:(n + 1) * S5_CHUNK,
                                                     j * LANES:(j + 1) * LANES]
    _to_chunk_rows(tok_ref, u2_ref, rows)
    u_ref[...] = z.astype(BF16)
    q_ref[...] = _silu(part(1)).astype(BF16)
    v_ref[...] = part(3).astype(BF16)
    gt_ref[...] = _silu(part(4)).astype(BF16)


def _inproj_even(x2, mod, norm_g, w_in, lb_logits, layer, ffn_f32, tiles_per_batch):
    lb_row = layer
    t, d = x2.shape
    w = w_in.shape[1] // 5
    tb = TOKEN_TILE
    row = lambda i: (i, 0)
    out_bf = jax.ShapeDtypeStruct((t, w), BF16)
    groups = w // GRANULE
    k = S5_CHUNK * GRANULE
    tok = pl.BlockSpec((tb, w), row)
    tri3 = jnp.asarray(_prefix_matrix(RECUR_CHUNK), BF16)
    cast_in, cast_out = _cast_specs(ffn_f32, layer, t // tb)
    return pl.pallas_call(
        functools.partial(_inproj_even_kernel, lb_row=lb_row),
        grid=(t // tb,),
        in_specs=[
            pl.BlockSpec((tb, d), row),
            pl.BlockSpec((1, N_MOD, d), lambda i: (i // tiles_per_batch, 0, 0)),
            _const_spec((1, d)),
            _const_spec(w_in.shape),
            _const_spec(lb_logits.shape),
            _const_spec(tri3.shape),
        ] + cast_in,
        out_specs=[tok, pl.BlockSpec((groups, tb // S5_CHUNK, k), lambda i: (0, i, 0)),
                   tok, tok, tok, tok, tok] + cast_out,
        out_shape=[out_bf, jax.ShapeDtypeStruct((groups, t // S5_CHUNK, k), BF16),
                   out_bf, out_bf, jax.ShapeDtypeStruct((t, w), F32), out_bf, out_bf]
                  + [jax.ShapeDtypeStruct(a.shape[1:], BF16) for a in ffn_f32],
        scratch_shapes=[pltpu.VMEM((w // LANES, tb // S5_CHUNK * CHUNK_PITCH, LANES), F32)],
        compiler_params=_params(("arbitrary",)),
        name="inproj_even",
    )(x2, mod, norm_g, w_in, lb_logits, tri3, *ffn_f32)


def _inproj_odd_kernel(x_ref, mod_ref, ng_ref, w_ref, wa1_ref, wa2_ref, ba_ref, tri_ref,
                       f1_ref, f3_ref, f2_ref,
                       q_ref, k_ref, g_ref, v_ref, gt_ref, b1_ref, b3_ref, b2_ref, *, q_scale):
    _cast_weights((f1_ref, f3_ref, f2_ref), (b1_ref, b3_ref, b2_ref))
    kw = q_ref.shape[1]
    vw = v_ref.shape[1]
    h = _norm_modulate(x_ref[...], ng_ref[...], mod_ref[0, 0:1, :], mod_ref[0, 1:2, :])
    hb = h.astype(BF16)
    a1 = _dot(hb, wa1_ref[...]).astype(BF16)
    za = _dot(a1, wa2_ref[...]) + ba_ref[...]
    log_sig = jnp.minimum(za, 0.0) - jnp.log(1.0 + jnp.exp(-jnp.abs(za)))
    g_ref[...] = _chunk_cumsum(log_sig * (LOG2E / GLA_GATE_NORM), tri_ref[...])
    q_ref[...] = (_dot(hb, w_ref[:, 0:kw]) * q_scale).astype(BF16)
    k_ref[...] = _dot(hb, w_ref[:, kw:2 * kw]).astype(BF16)
    v_ref[...] = _dot(hb, w_ref[:, 2 * kw:2 * kw + vw]).astype(BF16)
    gt_ref[...] = _silu(_dot(hb, w_ref[:, 2 * kw + vw:2 * kw + 2 * vw])).astype(BF16)


def _inproj_odd(x2, mod, norm_g, w_in, w_a1, w_a2, b_a, kw, vw, q_scale, layer, ffn_f32,
                tiles_per_batch):
    t, d = x2.shape
    tb = TOKEN_TILE
    row = lambda i: (i, 0)
    tri3 = jnp.asarray(_prefix_matrix(RECUR_CHUNK), BF16)
    cast_in, cast_out = _cast_specs(ffn_f32, layer, t // tb)
    return pl.pallas_call(
        functools.partial(_inproj_odd_kernel, q_scale=q_scale),
        grid=(t // tb,),
        in_specs=[
            pl.BlockSpec((tb, d), row),
            pl.BlockSpec((1, N_MOD, d), lambda i: (i // tiles_per_batch, 0, 0)),
            _const_spec((1, d)),
            _const_spec(w_in.shape),
            _const_spec(w_a1.shape),
            _const_spec(w_a2.shape),
            _const_spec(b_a.shape),
            _const_spec(tri3.shape),
        ] + cast_in,
        out_specs=[pl.BlockSpec((tb, kw), row), pl.BlockSpec((tb, kw), row), pl.BlockSpec((tb, kw), row),
                   pl.BlockSpec((tb, vw), row), pl.BlockSpec((tb, vw), row)] + cast_out,
        out_shape=[jax.ShapeDtypeStruct((t, kw), BF16), jax.ShapeDtypeStruct((t, kw), BF16),
                   jax.ShapeDtypeStruct((t, kw), F32), jax.ShapeDtypeStruct((t, vw), BF16),
                   jax.ShapeDtypeStruct((t, vw), BF16)]
                  + [jax.ShapeDtypeStruct(a.shape[1:], BF16) for a in ffn_f32],
        compiler_params=_params(("arbitrary",)),
        name="inproj_odd",
    )(x2, mod, norm_g, w_in, w_a1, w_a2, b_a, tri3, *ffn_f32)


def _s5_kernel(u_ref, mt_ref, wsr_ref, wsi_ref, wir_ref, wii_ref, sr_ref, si_ref, y_ref,
               *, chunks_per_seq):
    u = u_ref[0]
    v_re = _dot(u, wsr_ref[0])
    v_im = _dot(u, wsi_ref[0])
    n_in_seq = lax.broadcasted_iota(jnp.int32, v_re.shape, 0) & (chunks_per_seq - 1)
    for lv in range(sr_ref.shape[1]):
        step = 1 << lv
        keep = n_in_seq >= step
        s_re = jnp.where(keep, pltpu.roll(v_re, step, 0), 0.0)
        s_im = jnp.where(keep, pltpu.roll(v_im, step, 0), 0.0)
        a_re, a_im = sr_ref[0, lv:lv + 1, :], si_ref[0, lv:lv + 1, :]
        v_re, v_im = v_re + a_re * s_re - a_im * s_im, v_im + a_re * s_im + a_im * s_re
    keep = n_in_seq >= 1
    x_re = jnp.where(keep, pltpu.roll(v_re, 1, 0), 0.0).astype(BF16)
    x_im = jnp.where(keep, pltpu.roll(v_im, 1, 0), 0.0).astype(BF16)
    y = _dot(u, mt_ref[0]) + _dot_nt(x_re, wir_ref[0]) + _dot_nt(x_im, wii_ref[0])
    y_ref[0] = y.astype(BF16)


def _s5_prep_kernel(bt_re_ref, bt_im_ref, c_re_ref, c_im_ref, lp_re_ref, lp_im_ref,
                    mt_ref, wsr_ref, wsi_ref, wir_ref, wii_ref, toep_ref, *, chunk):
    hg, p = bt_re_ref.shape[1:]
    k = chunk * hg
    hp = lax.Precision.HIGHEST

    tau = lax.broadcasted_iota(jnp.int32, (chunk, p), 0)
    pw_re = jnp.ones((chunk, p), F32)
    pw_im = jnp.zeros((chunk, p), F32)
    for b in range(lp_re_ref.shape[1]):
        l_re, l_im = lp_re_ref[0, b:b + 1, :], lp_im_ref[0, b:b + 1, :]
        bit = ((tau >> b) & 1) == 1
        pw_re, pw_im = (jnp.where(bit, pw_re * l_re - pw_im * l_im, pw_re),
                        jnp.where(bit, pw_re * l_im + pw_im * l_re, pw_im))

    def rows_of_powers(exponent):
        pick = (exponent == lax.broadcasted_iota(jnp.int32, (k, chunk), 1)).astype(F32)
        return (jnp.dot(pick, pw_re, precision=hp, preferred_element_type=F32),
                jnp.dot(pick, pw_im, precision=hp, preferred_element_type=F32))

    t = lax.broadcasted_iota(jnp.int32, (k, chunk), 0) // hg
    tile = lambda x: jnp.concatenate([x] * chunk, axis=0)
    b_re, b_im = tile(bt_re_ref[0]), tile(bt_im_ref[0])
    c_re, c_im = tile(c_re_ref[0]), tile(c_im_ref[0])

    p_re, p_im = rows_of_powers(t)
    z_re, z_im = c_re * p_re - c_im * p_im, c_re * p_im + c_im * p_re
    nt = (((1,), (1,)), ((), ()))
    kt = (lax.dot_general(bt_re_ref[0], z_re, nt, precision=hp, preferred_element_type=F32)
          - lax.dot_general(bt_im_ref[0], z_im, nt, precision=hp, preferred_element_type=F32))
    col_t = lax.broadcasted_iota(jnp.int32, (chunk, k), 1) // hg
    row_s = lax.broadcasted_iota(jnp.int32, (chunk, k), 0)
    for hi in range(hg):
        rows = jnp.broadcast_to(kt[hi:hi + 1, :], (chunk, k))
        rows = pltpu.roll(rows, 0, 1, stride=hg, stride_axis=0)
        rows = jnp.where(col_t >= row_s, rows, 0.0)
        for j in range(k // LANES):
            toep_ref[j, pl.ds(hi, chunk, stride=hg), :] = rows[:, j * LANES:(j + 1) * LANES]
    for j in range(k // LANES):
        mt_ref[0, :, j * LANES:(j + 1) * LANES] = toep_ref[j].astype(BF16)
    p_re, p_im = rows_of_powers(chunk - 1 - t)
    wsr_ref[0] = (b_re * p_re - b_im * p_im).astype(BF16)
    wsi_ref[0] = (b_re * p_im + b_im * p_re).astype(BF16)
    l_re, l_im = lp_re_ref[0, 0:1, :], lp_im_ref[0, 0:1, :]
    wir_ref[0] = (z_re * l_re - z_im * l_im).astype(BF16)
    wii_ref[0] = (-(z_re * l_im + z_im * l_re)).astype(BF16)


def _s5_group_kernel(u_ref, bt_re_ref, bt_im_ref, c_re_ref, c_im_ref, lp_re_ref, lp_im_ref,
                     sr_ref, si_ref, y_ref, mt_ref, wsr_ref, wsi_ref, wir_ref, wii_ref, toep_ref,
                     *, chunk, chunks_per_seq):
    _s5_prep_kernel(bt_re_ref, bt_im_ref, c_re_ref, c_im_ref, lp_re_ref, lp_im_ref,
                    mt_ref, wsr_ref, wsi_ref, wir_ref, wii_ref, toep_ref, chunk=chunk)
    _s5_kernel(u_ref, mt_ref, wsr_ref, wsi_ref, wir_ref, wii_ref, sr_ref, si_ref, y_ref,
               chunks_per_seq=chunks_per_seq)


def _s5_discretise(lam_re, lam_im, b_re, b_im, log_step, chunk, chunks_per_seq):
    delta = jnp.exp(log_step)[:, None]
    ld_re, ld_im = lam_re * delta, lam_im * delta

    def powers(exponents):
        e = jnp.asarray(exponents, F32)[None, :, None]
        mag = jnp.exp(ld_re[:, None, :] * e)
        ang = ld_im[:, None, :] * e
        return mag * jnp.cos(ang), mag * jnp.sin(ang)

    lp_re, lp_im = powers([1 << b for b in range(chunk.bit_length())])
    nr, ni = lp_re[:, 0] - 1.0, lp_im[:, 0]
    den = lam_re * lam_re + lam_im * lam_im
    s_re = (nr * lam_re + ni * lam_im) / den
    s_im = (ni * lam_re - nr * lam_im) / den
    bt_re = (s_re[:, :, None] * b_re - s_im[:, :, None] * b_im).transpose(0, 2, 1)
    bt_im = (s_re[:, :, None] * b_im + s_im[:, :, None] * b_re).transpose(0, 2, 1)
    n_lev = int(math.log2(chunks_per_seq))
    sc_re, sc_im = powers([chunk * (1 << lv) for lv in range(n_lev)])
    return bt_re, bt_im, lp_re, lp_im, sc_re, sc_im


def _s5_mix(u2, lam_re, lam_im, b_re, b_im, c_re, c_im, log_step, chunks_per_seq):
    g, r, k = u2.shape
    chunk = k // GRANULE
    p = lam_re.shape[1]
    bt_re, bt_im, lp_re, lp_im, sc_re, sc_im = _s5_discretise(
        lam_re, lam_im, b_re, b_im, log_step, chunk, chunks_per_seq)
    grp = lambda i: (i, 0, 0)
    small = lambda a: pl.BlockSpec((1,) + a.shape[1:], grp)
    params = (bt_re, bt_im, c_re, c_im, lp_re, lp_im, sc_re, sc_im)
    return pl.pallas_call(
        functools.partial(_s5_group_kernel, chunk=chunk, chunks_per_seq=chunks_per_seq),
        grid=(g,),
        in_specs=[pl.BlockSpec((1, r, k), grp)] + [small(a) for a in params],
        out_specs=pl.BlockSpec((1, r, k), grp),
        out_shape=jax.ShapeDtypeStruct((g, r, k), BF16),
        scratch_shapes=[pltpu.VMEM((1, k, k), BF16)] + [pltpu.VMEM((1, k, p), BF16)] * 4
                       + [pltpu.VMEM((k // LANES, k, LANES), F32)],
        compiler_params=_params(("arbitrary",)),
        name="s5_mix",
    )(u2, *params)


def _level_index(chunk):
    i = np.arange(chunk)[:, None]
    j = np.arange(chunk)[None, :]
    x = np.bitwise_xor(i, j)
    lvl = np.floor(np.log2(np.maximum(x, 1))).astype(np.int32)
    return np.where(i > j, lvl, np.where(i == j, -1, -2)).astype(np.int32)


def _recur_kernel(lvl_ref, q_ref, k_ref, c_ref, v_ref, gt_ref, ng_ref, y_ref, st_ref,
                  *, chunk, heads, dk, dv):
    @pl.when(pl.program_id(1) == 0)
    def _():
        st_ref[...] = jnp.zeros(st_ref.shape, F32)

    n_chunks = q_ref.shape[0] // chunk
    n_levels = int(math.log2(chunk))
    lvl = lvl_ref[...]
    row = lax.broadcasted_iota(jnp.int32, (chunk, dk), 0)
    ng = ng_ref[...]

    sub = lax.broadcasted_iota(jnp.int32, (SUBLANES, dk), 0)

    for c in range(n_chunks):
        r0 = c * chunk
        rows = slice(r0, r0 + chunk)
        for hd in range(heads):
            ks = slice(hd * dk, (hd + 1) * dk)
            vs = slice(hd * dv, (hd + 1) * dv)
            qb = q_ref[rows, ks]
            kb = k_ref[rows, ks]
            vb = v_ref[rows, vs]
            q = qb.astype(F32)
            k = kb.astype(F32)
            cum = c_ref[rows, ks]

            def row_on_sublanes(i):
                return jnp.broadcast_to(c_ref[pl.ds(r0 + i, 1), ks], (SUBLANES, dk))

            scores = jnp.where(lvl == -1, _dot_nt(qb, kb), 0.0)
            for lv in range(n_levels):
                h = 1 << lv
                second = (row & h) != 0
                if h == 1:
                    boundary = jnp.where(second, pltpu.roll(cum, 1, 0), cum)
                else:
                    pieces = []
                    for v in range(chunk // SUBLANES):
                        first_row = v * SUBLANES
                        if 2 * h <= SUBLANES:
                            piece = row_on_sublanes(first_row + h - 1)
                            for b in range(1, SUBLANES // (2 * h)):
                                piece = jnp.where(sub >= 2 * h * b,
                                                  row_on_sublanes(first_row + 2 * h * b + h - 1), piece)
                        elif first_row % (2 * h) == 0:
                            piece = row_on_sublanes(first_row + h - 1)
                        pieces.append(piece)
                    boundary = jnp.concatenate(pieces, axis=0)
                decay = jnp.exp2(-jnp.abs(cum - boundary))
                w = (jnp.where(second, q, k) * decay).astype(BF16)
                scores = jnp.where(lvl == lv, _dot_nt(w, w), scores)
            last = row_on_sublanes(chunk - 1)
            st = st_ref[hd]
            o = _dot(scores.astype(BF16), vb)
            o = o + _dot_nt((q * jnp.exp2(cum)).astype(BF16), st.astype(BF16))
            kd = (k * jnp.exp2(jnp.concatenate([last] * (chunk // SUBLANES), axis=0) - cum)).astype(BF16)
            st_ref[hd] = st * jnp.exp2(last[0:1, :]) + _dot_tn(vb, kd)
            ms = jnp.mean(o * o, axis=-1, keepdims=True)
            o = o * lax.rsqrt(ms + EPS) * ng * gt_ref[rows, vs].astype(F32)
            y_ref[rows, vs] = o.astype(BF16)


def _gated_recurrence(q, k, g, v, gate, norm_g, bsz, heads):
    t, kw = q.shape
    vw = v.shape[1]
    dk, dv = kw // heads, vw // heads
    tt = RECUR_TILE
    per_seq = (t // bsz) // tt
    row = lambda b, i: (b * per_seq + i, 0)
    lvl = jnp.asarray(_level_index(RECUR_CHUNK))
    return pl.pallas_call(
        functools.partial(_recur_kernel, chunk=RECUR_CHUNK, heads=heads, dk=dk, dv=dv),
        grid=(bsz, per_seq),
        in_specs=[
            pl.BlockSpec(lvl.shape, lambda b, i: (0, 0)),
            pl.BlockSpec((tt, kw), row), pl.BlockSpec((tt, kw), row), pl.BlockSpec((tt, kw), row),
            pl.BlockSpec((tt, vw), row), pl.BlockSpec((tt, vw), row),
            pl.BlockSpec((1, dv), lambda b, i: (0, 0)),
        ],
        out_specs=pl.BlockSpec((tt, vw), row),
        out_shape=jax.ShapeDtypeStruct((t, vw), BF16),
        scratch_shapes=[pltpu.VMEM((heads, dv, dk), F32)],
        compiler_params=_params(("arbitrary", "arbitrary")),
        name="gated_recurrence",
    )(lvl, q, k, g, v, gate, norm_g)


def _gelu_tanh(x):
    return 0.5 * x * (1.0 + jnp.tanh(math.sqrt(2.0 / math.pi) * (x + 0.044715 * (x * x * x))))


def _ffn_tail(x, mod_ref, nf_ref, w1_ref, w3_ref, w2_ref):
    h = _norm_modulate(x, nf_ref[...], mod_ref[0, 3:4, :], mod_ref[0, 4:5, :]).astype(BF16)
    acc = None
    for j in range(w1_ref.shape[1] // FFN_TILE):
        cols = slice(j * FFN_TILE, (j + 1) * FFN_TILE)
        a = _dot(h, w1_ref[:, cols])
        b = _dot(h, w3_ref[:, cols])
        part = _dot((_silu(a) * b).astype(BF16), w2_ref[cols, :])
        acc = part if acc is None else acc + part
    return x + mod_ref[0, 5:6, :] * acc


def _block_even_kernel(x_ref, y2_ref, u_ref, yb_ref, mod_ref, sd_ref, wg_ref, bg_ref, wo_ref,
                       nf_ref, w1_ref, w3_ref, w2_ref, o_ref, tok_ref):
    for rows in _sub_tiles(x_ref.shape[0], BLOCK_SUB_TILES):
        _to_token_rows(y2_ref, tok_ref, rows)
        chunks = range(rows.start // S5_CHUNK, rows.stop // S5_CHUNK)
        y = jnp.concatenate(
            [jnp.concatenate([tok_ref[j, _tok_rows_of_chunk(n), :] for n in chunks], axis=0)
             for j in range(tok_ref.shape[0])], axis=1)
        y = y + sd_ref[...] * u_ref[rows, :].astype(F32)
        y = _gelu_tanh(y)
        ya = y * jax.nn.sigmoid(_dot(y.astype(BF16), wg_ref[...]) + bg_ref[...])
        sw = ya.shape[1]
        mixed = _dot(ya.astype(BF16), wo_ref[0:sw, :]) + _dot(yb_ref[rows, :], wo_ref[sw:, :])
        x = x_ref[rows, :] + mod_ref[0, 2:3, :] * mixed
        o_ref[rows, :] = _ffn_tail(x, mod_ref, nf_ref, w1_ref, w3_ref, w2_ref)


def _block_odd_kernel(x_ref, yc_ref, mod_ref, wo_ref, nf_ref, w1_ref, w3_ref, w2_ref, fg_ref,
                      o_ref, *, final_norm):
    for rows in _sub_tiles(x_ref.shape[0], BLOCK_SUB_TILES):
        x = x_ref[rows, :] + mod_ref[0, 2:3, :] * _dot(yc_ref[rows, :], wo_ref[...])
        x = _ffn_tail(x, mod_ref, nf_ref, w1_ref, w3_ref, w2_ref)
        if final_norm:
            ms = jnp.mean(x * x, axis=-1, keepdims=True)
            x = x * lax.rsqrt(ms + EPS) * fg_ref[...]
        o_ref[rows, :] = x


def _block_even(x2, y2, u, yb, mod, s5_d, w_glu, b_glu, w_out, nf, w1t, w3t, w2t, tiles_per_batch):
    t, d = x2.shape
    tb = TOKEN_TILE
    row = lambda i: (i, 0)
    sw = u.shape[1]
    groups, _, k = y2.shape
    return pl.pallas_call(
        _block_even_kernel,
        grid=(t // tb,),
        in_specs=[
            pl.BlockSpec((tb, d), row),
            pl.BlockSpec((groups, tb * GRANULE // k, k), lambda i: (0, i, 0)),
            pl.BlockSpec((tb, sw), row),
            pl.BlockSpec((tb, yb.shape[1]), row),
            pl.BlockSpec((1, N_MOD, d), lambda i: (i // tiles_per_batch, 0, 0)),
            _const_spec(s5_d.shape), _const_spec(w_glu.shape), _const_spec(b_glu.shape),
            _const_spec(w_out.shape), _const_spec(nf.shape),
            _const_spec(w1t.shape), _const_spec(w3t.shape), _const_spec(w2t.shape),
        ],
        out_specs=pl.BlockSpec((tb, d), row),
        out_shape=jax.ShapeDtypeStruct((t, d), F32),
        scratch_shapes=[pltpu.VMEM((sw // LANES, tb // S5_CHUNK * CHUNK_PITCH, LANES), F32)],
        compiler_params=_params(("arbitrary",)),
        name="block_even",
    )(x2, y2, u, yb, mod, s5_d, w_glu, b_glu, w_out, nf, w1t, w3t, w2t)


def _block_odd(x2, yc, mod, w_out, nf, w1t, w3t, w2t, final_g, final_norm, tiles_per_batch):
    t, d = x2.shape
    tb = TOKEN_TILE
    row = lambda i: (i, 0)
    return pl.pallas_call(
        functools.partial(_block_odd_kernel, final_norm=final_norm),
        grid=(t // tb,),
        in_specs=[
            pl.BlockSpec((tb, d), row), pl.BlockSpec((tb, yc.shape[1]), row),
            pl.BlockSpec((1, N_MOD, d), lambda i: (i // tiles_per_batch, 0, 0)),
            _const_spec(w_out.shape), _const_spec(nf.shape),
            _const_spec(w1t.shape), _const_spec(w3t.shape), _const_spec(w2t.shape),
            _const_spec(final_g.shape),
        ],
        out_specs=pl.BlockSpec((tb, d), row),
        out_shape=jax.ShapeDtypeStruct((t, d), F32),
        compiler_params=_params(("arbitrary",)),
        name="block_odd",
    )(x2, yc, mod, w_out, nf, w1t, w3t, w2t, final_g)


def kernel(x, c, ada_w, ada_b, norm_mix_g, norm_ffn_g, ev_w_in, ev_w_out, s5_lam_re, s5_lam_im, s5_b_re, s5_b_im, s5_c_re, s5_c_im, s5_d, s5_log_step, s5_w_glu, s5_b_glu, hg_lb_logits, hg_norm_g, od_w_in, od_w_a1, od_w_a2, od_b_a, gla_norm_g, od_w_out, ffn_w1, ffn_w3, ffn_w2, final_norm_g):
    bsz, seq, d = x.shape
    depth = ada_w.shape[0]
    assert depth % 2 == 0, "the final norm is fused into the last (odd) layer"
    t = bsz * seq
    tiles_per_batch = seq // TOKEN_TILE
    x2 = x.reshape(t, d)
    mod = _adaln(c, ada_w, ada_b).reshape(depth, bsz, N_MOD, d)

    for layer in range(depth):
        nm = norm_mix_g[layer].reshape(1, d)
        nf = norm_ffn_g[layer].reshape(1, d)
        ffn_f32 = (ffn_w1, ffn_w3, ffn_w2)
        if layer % 2 == 0:
            e = layer // 2
            u, u2, hq, hk, hgl, hv, hgate, w1t, w3t, w2t = _inproj_even(
                x2, mod[layer], nm, ev_w_in[e].astype(BF16), hg_lb_logits, layer, ffn_f32,
                tiles_per_batch)
            sw = u.shape[1]
            y2 = _s5_mix(u2, s5_lam_re[e], s5_lam_im[e], s5_b_re[e], s5_b_im[e], s5_c_re[e],
                         s5_c_im[e], s5_log_step[e], seq // S5_CHUNK)
            yb = _gated_recurrence(hq, hk, hgl, hv, hgate, hg_norm_g[e].reshape(1, -1), bsz, HG_HEADS)
            x2 = _block_even(x2, y2, u, yb, mod[layer], s5_d[e].reshape(1, sw),
                             s5_w_glu[e].astype(BF16), s5_b_glu[e].reshape(1, sw),
                             ev_w_out[e].astype(BF16), nf, w1t, w3t, w2t, tiles_per_batch)
        else:
            o = layer // 2
            kw = od_w_a2.shape[-1]
            vw = od_w_out.shape[1]
            rank = od_w_a1.shape[-1]
            w_a1 = jnp.pad(od_w_a1[o], ((0, 0), (0, LANES - rank))).astype(BF16)
            w_a2 = jnp.pad(od_w_a2[o], ((0, LANES - rank), (0, 0))).astype(BF16)
            q, k, g, v, gate, w1t, w3t, w2t = _inproj_odd(
                x2, mod[layer], nm, od_w_in[o].astype(BF16), w_a1, w_a2, od_b_a[o].reshape(1, kw),
                kw, vw, (kw // GLA_HEADS) ** -0.5, layer, ffn_f32, tiles_per_batch)
            yc = _gated_recurrence(q, k, g, v, gate, gla_norm_g[o].reshape(1, -1), bsz, GLA_HEADS)
            x2 = _block_odd(x2, yc, mod[layer], od_w_out[o].astype(BF16), nf, w1t, w3t, w2t,
                            final_norm_g.reshape(1, d), layer == depth - 1, tiles_per_batch)
    return x2.reshape(bsz, seq, d)
```

```python
import functools
import math

import numpy as np
import jax
import jax.numpy as jnp
from jax import lax
from jax.experimental import pallas as pl
from jax.experimental.pallas import tpu as pltpu

F32 = jnp.float32
BF16 = jnp.bfloat16
EPS = 1e-6
LOG2E = math.log2(math.e)

HG_HEADS = 4
GLA_HEADS = 4
GLA_GATE_NORM = 16.0
N_MOD = 6

VMEM_LIMIT_BYTES = 56 * 1024 * 1024
LANES = 128
SUBLANES = 8
BF16_ROWS = 16
GRANULE = 16
GRANULES = LANES // GRANULE
RELAYOUT_ROWS = 32

TOKEN_TILE = 1024
RECUR_TILE = 1024
RECUR_CHUNK = 128
S5_CHUNK = 32
CHUNK_PITCH = 40
FFN_TILE = 256
BLOCK_SUB_TILES = 2


def _dot(a, b):
    return jnp.dot(a, b, preferred_element_type=F32)


def _dot_nt(a, b):
    return lax.dot_general(a, b, (((1,), (1,)), ((), ())), preferred_element_type=F32)


def _dot_tn(a, b):
    return lax.dot_general(a, b, (((0,), (0,)), ((), ())), preferred_element_type=F32)


def _silu(x):
    return x * jax.nn.sigmoid(x)


def _params(semantics):
    return pltpu.CompilerParams(dimension_semantics=semantics, vmem_limit_bytes=VMEM_LIMIT_BYTES)


def _const_spec(shape):
    nd = len(shape)
    return pl.BlockSpec(shape, lambda *_: (0,) * nd, pipeline_mode=pl.Buffered(1))


def _sub_tiles(n_rows, n_sub):
    step = n_rows // n_sub
    return [slice(i * step, (i + 1) * step) for i in range(n_sub)]


def _cast_specs(items, n_steps):
    in_specs, out_specs = [], []
    for w, layer in items:
        _, n_rows, n_cols = w.shape
        rows = next(r for r in range(BF16_ROWS, n_rows + 1, BF16_ROWS)
                    if n_rows % r == 0 and r * n_steps >= n_rows)
        last = n_rows // rows - 1
        in_specs.append(pl.BlockSpec((None, rows, n_cols),
                                     lambda i, last=last, layer=layer: (layer, jnp.minimum(i, last), 0)))
        out_specs.append(pl.BlockSpec((rows, n_cols), lambda i, last=last: (jnp.minimum(i, last), 0)))
    return in_specs, out_specs


def _cast_weights(f32_refs, bf16_refs):
    for src, dst in zip(f32_refs, bf16_refs):
        dst[...] = src[...].astype(BF16)


def _prefix_matrix(chunk):
    tri = np.tril(np.ones((chunk, chunk), np.float32))
    return np.concatenate([tri, tri, tri], axis=1)


def _chunk_cumsum(g, tri3):
    chunk = tri3.shape[0]
    hi = g.astype(BF16)
    r1 = g - hi.astype(F32)
    mid = r1.astype(BF16)
    lo = (r1 - mid.astype(F32)).astype(BF16)
    out = []
    for c in range(g.shape[0] // chunk):
        rows = slice(c * chunk, (c + 1) * chunk)
        out.append(_dot(tri3, jnp.concatenate([hi[rows], mid[rows], lo[rows]], axis=0)))
    return jnp.concatenate(out, axis=0)


def _norm_modulate(x, norm_g, shift, scale):
    ms = jnp.mean(x * x, axis=-1, keepdims=True)
    h = x * lax.rsqrt(ms + EPS) * norm_g
    return h * (1.0 + scale) + shift


def _adaln_kernel(ct_ref, w_ref, b_ref, o_ref):
    cond = _silu(ct_ref[...])
    w = w_ref[0]
    for b in range(cond.shape[1]):
        o_ref[0, b:b + 1, :] = jnp.sum(cond[:, b:b + 1] * w, axis=0, keepdims=True) + b_ref[0]


def _adaln(c, ada_w, ada_b):
    depth, d, n = ada_w.shape
    bsz = c.shape[0]
    nt = 1024
    return pl.pallas_call(
        _adaln_kernel,
        grid=(depth, n // nt),
        in_specs=[
            pl.BlockSpec((d, bsz), lambda l, j: (0, 0)),
            pl.BlockSpec((1, d, nt), lambda l, j: (l, 0, j)),
            pl.BlockSpec((1, 1, nt), lambda l, j: (l, 0, j)),
        ],
        out_specs=pl.BlockSpec((1, bsz, nt), lambda l, j: (l, 0, j)),
        out_shape=jax.ShapeDtypeStruct((depth, bsz, n), F32),
        compiler_params=_params(("arbitrary", "arbitrary")),
        name="adaln_mod",
    )(c.T, ada_w, ada_b.reshape(depth, 1, n))


def _granule_transpose(blocks, slot):
    blocks = list(blocks)
    for d in (4, 2, 1):
        upper = (slot & d) != 0
        for a in range(GRANULES):
            if a & d:
                continue
            lo, hi = blocks[a], blocks[a + d]
            blocks[a] = jnp.where(upper, pltpu.roll(hi, GRANULE * d, 1), lo)
            blocks[a + d] = jnp.where(upper, hi, pltpu.roll(lo, LANES - GRANULE * d, 1))
    return blocks


def _tok_rows_of_chunk(n):
    return slice(n * CHUNK_PITCH, n * CHUNK_PITCH + S5_CHUNK)


def _to_chunk_rows(tok_ref, out_ref, tok_rows):
    n_grp, _, k = out_ref.shape
    chunk = k // GRANULE
    n_rows = min(RELAYOUT_ROWS, (tok_rows.stop - tok_rows.start) // chunk)
    slot = lax.broadcasted_iota(jnp.int32, (n_rows, LANES), 1) // GRANULE
    for r0 in range(tok_rows.start // chunk, tok_rows.stop // chunk, n_rows):
        for tb in range(chunk // GRANULES):
            for j in range(n_grp // GRANULES):
                rows = [tok_ref[j, pl.ds(r0 * CHUNK_PITCH + GRANULES * tb + tp, n_rows,
                                         stride=CHUNK_PITCH), :]
                        for tp in range(GRANULES)]
                for gp, blk in enumerate(_granule_transpose(rows, slot)):
                    out_ref[GRANULES * j + gp, r0:r0 + n_rows, tb * LANES:(tb + 1) * LANES] = (
                        blk.astype(out_ref.dtype))


def _to_token_rows(grp_ref, tok_ref, tok_rows):
    n_grp, _, k = grp_ref.shape
    chunk = k // GRANULE
    n_rows = min(RELAYOUT_ROWS, (tok_rows.stop - tok_rows.start) // chunk)
    slot = lax.broadcasted_iota(jnp.int32, (n_rows, LANES), 1) // GRANULE
    for r0 in range(tok_rows.start // chunk, tok_rows.stop // chunk, n_rows):
        for tb in range(chunk // GRANULES):
            for j in range(n_grp // GRANULES):
                cols = [grp_ref[GRANULES * j + gp, r0:r0 + n_rows, tb * LANES:(tb + 1) * LANES].astype(F32)
                        for gp in range(GRANULES)]
                for tp, blk in enumerate(_granule_transpose(cols, slot)):
                    tok_ref[j, pl.ds(r0 * CHUNK_PITCH + GRANULES * tb + tp, n_rows,
                                     stride=CHUNK_PITCH), :] = blk


def _inproj_even_kernel(x_ref, mod_ref, ng_ref, w_ref, lbl_ref, tri_ref, *refs, lb_row, n_cast):
    f32_refs, refs = refs[:n_cast], refs[n_cast:]
    u_ref, u2_ref, q_ref, k_ref, g_ref, v_ref, gt_ref = refs[:7]
    bf16_refs, tok_ref = refs[7:7 + n_cast], refs[7 + n_cast]
    _cast_weights(f32_refs, bf16_refs)
    w = u_ref.shape[1]
    lg = lbl_ref[...]
    e = jnp.exp(lg - jnp.max(lg, axis=0, keepdims=True))
    lb = jnp.sum(e[:lb_row + 1], axis=0, keepdims=True) / jnp.sum(e, axis=0, keepdims=True)
    h = _norm_modulate(x_ref[...], ng_ref[...], mod_ref[0, 0:1, :], mod_ref[0, 1:2, :])
    hb = h.astype(BF16)
    part = lambda p: _dot(hb, w_ref[:, p * w:(p + 1) * w])
    rows = slice(0, x_ref.shape[0])
    f = part(2)
    g_ref[...] = _chunk_cumsum(jnp.log(lb + (1.0 - lb) * jax.nn.sigmoid(f)) * LOG2E, tri_ref[...])
    k_ref[...] = ((1.0 - lb) * jax.nn.sigmoid(-f)).astype(BF16)
    z = part(0)
    for j in range(tok_ref.shape[0]):
        for n in range(x_ref.shape[0] // S5_CHUNK):
            tok_ref[j, _tok_rows_of_chunk(n), :] = z[n * S5_CHUNK:(n + 1) * S5_CHUNK,
                                                     j * LANES:(j + 1) * LANES]
    _to_chunk_rows(tok_ref, u2_ref, rows)
    u_ref[...] = z.astype(BF16)
    q_ref[...] = _silu(part(1)).astype(BF16)
    v_ref[...] = part(3).astype(BF16)
    gt_ref[...] = _silu(part(4)).astype(BF16)


def _inproj_even(x2, mod, norm_g, w_in, lb_logits, lb_row, casts, tiles_per_batch):
    t, d = x2.shape
    w = w_in.shape[1] // 5
    tb = TOKEN_TILE
    row = lambda i: (i, 0)
    out_bf = jax.ShapeDtypeStruct((t, w), BF16)
    groups = w // GRANULE
    k = S5_CHUNK * GRANULE
    tok = pl.BlockSpec((tb, w), row)
    tri3 = jnp.asarray(_prefix_matrix(RECUR_CHUNK), BF16)
    cast_in, cast_out = _cast_specs(casts, t // tb)
    return pl.pallas_call(
        functools.partial(_inproj_even_kernel, lb_row=lb_row, n_cast=len(casts)),
        grid=(t // tb,),
        in_specs=[
            pl.BlockSpec((tb, d), row),
            pl.BlockSpec((1, N_MOD, d), lambda i: (i // tiles_per_batch, 0, 0)),
            _const_spec((1, d)),
            _const_spec(w_in.shape),
            _const_spec(lb_logits.shape),
            _const_spec(tri3.shape),
        ] + cast_in,
        out_specs=[tok, pl.BlockSpec((groups, tb // S5_CHUNK, k), lambda i: (0, i, 0)),
                   tok, tok, tok, tok, tok] + cast_out,
        out_shape=[out_bf, jax.ShapeDtypeStruct((groups, t // S5_CHUNK, k), BF16),
                   out_bf, out_bf, jax.ShapeDtypeStruct((t, w), F32), out_bf, out_bf]
                  + [jax.ShapeDtypeStruct(a.shape[1:], BF16) for a, _ in casts],
        scratch_shapes=[pltpu.VMEM((w // LANES, tb // S5_CHUNK * CHUNK_PITCH, LANES), F32)],
        compiler_params=_params(("arbitrary",)),
        name="inproj_even",
    )(x2, mod, norm_g, w_in, lb_logits, tri3, *[a for a, _ in casts])


def _inproj_odd_kernel(x_ref, mod_ref, ng_ref, w_ref, wa1_ref, wa2_ref, ba_ref, tri_ref,
                       f1_ref, f3_ref, f2_ref,
                       q_ref, k_ref, g_ref, v_ref, gt_ref, b1_ref, b3_ref, b2_ref, *, q_scale):
    _cast_weights((f1_ref, f3_ref, f2_ref), (b1_ref, b3_ref, b2_ref))
    kw = q_ref.shape[1]
    vw = v_ref.shape[1]
    h = _norm_modulate(x_ref[...], ng_ref[...], mod_ref[0, 0:1, :], mod_ref[0, 1:2, :])
    hb = h.astype(BF16)
    a1 = _dot(hb, wa1_ref[...]).astype(BF16)
    za = _dot(a1, wa2_ref[...]) + ba_ref[...]
    log_sig = jnp.minimum(za, 0.0) - jnp.log(1.0 + jnp.exp(-jnp.abs(za)))
    g_ref[...] = _chunk_cumsum(log_sig * (LOG2E / GLA_GATE_NORM), tri_ref[...])
    q_ref[...] = (_dot(hb, w_ref[:, 0:kw]) * q_scale).astype(BF16)
    k_ref[...] = _dot(hb, w_ref[:, kw:2 * kw]).astype(BF16)
    v_ref[...] = _dot(hb, w_ref[:, 2 * kw:2 * kw + vw]).astype(BF16)
    gt_ref[...] = _silu(_dot(hb, w_ref[:, 2 * kw + vw:2 * kw + 2 * vw])).astype(BF16)


def _inproj_odd(x2, mod, norm_g, w_in, w_a1, w_a2, b_a, kw, vw, q_scale, layer, ffn_f32,
                tiles_per_batch):
    t, d = x2.shape
    tb = TOKEN_TILE
    row = lambda i: (i, 0)
    tri3 = jnp.asarray(_prefix_matrix(RECUR_CHUNK), BF16)
    cast_in, cast_out = _cast_specs([(a, layer) for a in ffn_f32], t // tb)
    return pl.pallas_call(
        functools.partial(_inproj_odd_kernel, q_scale=q_scale),
        grid=(t // tb,),
        in_specs=[
            pl.BlockSpec((tb, d), row),
            pl.BlockSpec((1, N_MOD, d), lambda i: (i // tiles_per_batch, 0, 0)),
            _const_spec((1, d)),
            _const_spec(w_in.shape),
            _const_spec(w_a1.shape),
            _const_spec(w_a2.shape),
            _const_spec(b_a.shape),
            _const_spec(tri3.shape),
        ] + cast_in,
        out_specs=[pl.BlockSpec((tb, kw), row), pl.BlockSpec((tb, kw), row), pl.BlockSpec((tb, kw), row),
                   pl.BlockSpec((tb, vw), row), pl.BlockSpec((tb, vw), row)] + cast_out,
        out_shape=[jax.ShapeDtypeStruct((t, kw), BF16), jax.ShapeDtypeStruct((t, kw), BF16),
                   jax.ShapeDtypeStruct((t, kw), F32), jax.ShapeDtypeStruct((t, vw), BF16),
                   jax.ShapeDtypeStruct((t, vw), BF16)]
                  + [jax.ShapeDtypeStruct(a.shape[1:], BF16) for a in ffn_f32],
        compiler_params=_params(("arbitrary",)),
        name="inproj_odd",
    )(x2, mod, norm_g, w_in, w_a1, w_a2, b_a, tri3, *ffn_f32)


def _s5_kernel(u_ref, mt_ref, wsr_ref, wsi_ref, wir_ref, wii_ref, sr_ref, si_ref, y_ref,
               *, chunks_per_seq):
    u = u_ref[0]
    v_re = _dot(u, wsr_ref[0])
    v_im = _dot(u, wsi_ref[0])
    n_in_seq = lax.broadcasted_iota(jnp.int32, v_re.shape, 0) & (chunks_per_seq - 1)
    for lv in range(sr_ref.shape[1]):
        step = 1 << lv
        keep = n_in_seq >= step
        s_re = jnp.where(keep, pltpu.roll(v_re, step, 0), 0.0)
        s_im = jnp.where(keep, pltpu.roll(v_im, step, 0), 0.0)
        a_re, a_im = sr_ref[0, lv:lv + 1, :], si_ref[0, lv:lv + 1, :]
        v_re, v_im = v_re + a_re * s_re - a_im * s_im, v_im + a_re * s_im + a_im * s_re
    keep = n_in_seq >= 1
    x_re = jnp.where(keep, pltpu.roll(v_re, 1, 0), 0.0).astype(BF16)
    x_im = jnp.where(keep, pltpu.roll(v_im, 1, 0), 0.0).astype(BF16)
    y = _dot(u, mt_ref[0]) + _dot_nt(x_re, wir_ref[0]) + _dot_nt(x_im, wii_ref[0])
    y_ref[0] = y.astype(BF16)


def _s5_prep_kernel(bt_re_ref, bt_im_ref, c_re_ref, c_im_ref, lp_re_ref, lp_im_ref,
                    mt_ref, wsr_ref, wsi_ref, wir_ref, wii_ref, toep_ref, *, chunk):
    hg, p = bt_re_ref.shape[1:]
    k = chunk * hg
    hp = lax.Precision.HIGHEST

    tau = lax.broadcasted_iota(jnp.int32, (chunk, p), 0)
    pw_re = jnp.ones((chunk, p), F32)
    pw_im = jnp.zeros((chunk, p), F32)
    for b in range(lp_re_ref.shape[1]):
        l_re, l_im = lp_re_ref[0, b:b + 1, :], lp_im_ref[0, b:b + 1, :]
        bit = ((tau >> b) & 1) == 1
        pw_re, pw_im = (jnp.where(bit, pw_re * l_re - pw_im * l_im, pw_re),
                        jnp.where(bit, pw_re * l_im + pw_im * l_re, pw_im))

    def rows_of_powers(exponent):
        pick = (exponent == lax.broadcasted_iota(jnp.int32, (k, chunk), 1)).astype(F32)
        return (jnp.dot(pick, pw_re, precision=hp, preferred_element_type=F32),
                jnp.dot(pick, pw_im, precision=hp, preferred_element_type=F32))

    t = lax.broadcasted_iota(jnp.int32, (k, chunk), 0) // hg
    tile = lambda x: jnp.concatenate([x] * chunk, axis=0)
    b_re, b_im = tile(bt_re_ref[0]), tile(bt_im_ref[0])
    c_re, c_im = tile(c_re_ref[0]), tile(c_im_ref[0])

    p_re, p_im = rows_of_powers(t)
    z_re, z_im = c_re * p_re - c_im * p_im, c_re * p_im + c_im * p_re
    nt = (((1,), (1,)), ((), ()))
    kt = (lax.dot_general(bt_re_ref[0], z_re, nt, precision=hp, preferred_element_type=F32)
          - lax.dot_general(bt_im_ref[0], z_im, nt, precision=hp, preferred_element_type=F32))
    col_t = lax.broadcasted_iota(jnp.int32, (chunk, k), 1) // hg
    row_s = lax.broadcasted_iota(jnp.int32, (chunk, k), 0)
    for hi in range(hg):
        rows = jnp.broadcast_to(kt[hi:hi + 1, :], (chunk, k))
        rows = pltpu.roll(rows, 0, 1, stride=hg, stride_axis=0)
        rows = jnp.where(col_t >= row_s, rows, 0.0)
        for j in range(k // LANES):
            toep_ref[j, pl.ds(hi, chunk, stride=hg), :] = rows[:, j * LANES:(j + 1) * LANES]
    for j in range(k // LANES):
        mt_ref[0, :, j * LANES:(j + 1) * LANES] = toep_ref[j].astype(BF16)
    p_re, p_im = rows_of_powers(chunk - 1 - t)
    wsr_ref[0] = (b_re * p_re - b_im * p_im).astype(BF16)
    wsi_ref[0] = (b_re * p_im + b_im * p_re).astype(BF16)
    l_re, l_im = lp_re_ref[0, 0:1, :], lp_im_ref[0, 0:1, :]
    wir_ref[0] = (z_re * l_re - z_im * l_im).astype(BF16)
    wii_ref[0] = (-(z_re * l_im + z_im * l_re)).astype(BF16)


def _s5_group_kernel(u_ref, bt_re_ref, bt_im_ref, c_re_ref, c_im_ref, lp_re_ref, lp_im_ref,
                     sr_ref, si_ref, y_ref, mt_ref, wsr_ref, wsi_ref, wir_ref, wii_ref, toep_ref,
                     *, chunk, chunks_per_seq):
    _s5_prep_kernel(bt_re_ref, bt_im_ref, c_re_ref, c_im_ref, lp_re_ref, lp_im_ref,
                    mt_ref, wsr_ref, wsi_ref, wir_ref, wii_ref, toep_ref, chunk=chunk)
    _s5_kernel(u_ref, mt_ref, wsr_ref, wsi_ref, wir_ref, wii_ref, sr_ref, si_ref, y_ref,
               chunks_per_seq=chunks_per_seq)


def _s5_discretise(lam_re, lam_im, b_re, b_im, log_step, chunk, chunks_per_seq):
    delta = jnp.exp(log_step)[:, None]
    ld_re, ld_im = lam_re * delta, lam_im * delta

    def powers(exponents):
        e = jnp.asarray(exponents, F32)[None, :, None]
        mag = jnp.exp(ld_re[:, None, :] * e)
        ang = ld_im[:, None, :] * e
        return mag * jnp.cos(ang), mag * jnp.sin(ang)

    lp_re, lp_im = powers([1 << b for b in range(chunk.bit_length())])
    nr, ni = lp_re[:, 0] - 1.0, lp_im[:, 0]
    den = lam_re * lam_re + lam_im * lam_im
    s_re = (nr * lam_re + ni * lam_im) / den
    s_im = (ni * lam_re - nr * lam_im) / den
    bt_re = (s_re[:, :, None] * b_re - s_im[:, :, None] * b_im).transpose(0, 2, 1)
    bt_im = (s_re[:, :, None] * b_im + s_im[:, :, None] * b_re).transpose(0, 2, 1)
    n_lev = int(math.log2(chunks_per_seq))
    sc_re, sc_im = powers([chunk * (1 << lv) for lv in range(n_lev)])
    return bt_re, bt_im, lp_re, lp_im, sc_re, sc_im


def _s5_mix(u2, lam_re, lam_im, b_re, b_im, c_re, c_im, log_step, chunks_per_seq):
    g, r, k = u2.shape
    chunk = k // GRANULE
    p = lam_re.shape[1]
    bt_re, bt_im, lp_re, lp_im, sc_re, sc_im = _s5_discretise(
        lam_re, lam_im, b_re, b_im, log_step, chunk, chunks_per_seq)
    grp = lambda i: (i, 0, 0)
    small = lambda a: pl.BlockSpec((1,) + a.shape[1:], grp)
    params = (bt_re, bt_im, c_re, c_im, lp_re, lp_im, sc_re, sc_im)
    return pl.pallas_call(
        functools.partial(_s5_group_kernel, chunk=chunk, chunks_per_seq=chunks_per_seq),
        grid=(g,),
        in_specs=[pl.BlockSpec((1, r, k), grp)] + [small(a) for a in params],
        out_specs=pl.BlockSpec((1, r, k), grp),
        out_shape=jax.ShapeDtypeStruct((g, r, k), BF16),
        scratch_shapes=[pltpu.VMEM((1, k, k), BF16)] + [pltpu.VMEM((1, k, p), BF16)] * 4
                       + [pltpu.VMEM((k // LANES, k, LANES), F32)],
        compiler_params=_params(("arbitrary",)),
        name="s5_mix",
    )(u2, *params)


def _level_index(chunk):
    i = np.arange(chunk)[:, None]
    j = np.arange(chunk)[None, :]
    x = np.bitwise_xor(i, j)
    lvl = np.floor(np.log2(np.maximum(x, 1))).astype(np.int32)
    return np.where(i > j, lvl, np.where(i == j, -1, -2)).astype(np.int32)


def _recur_kernel(lvl_ref, q_ref, k_ref, c_ref, v_ref, gt_ref, ng_ref, y_ref, st_ref,
                  *, chunk, heads, dk, dv):
    @pl.when(pl.program_id(1) == 0)
    def _():
        st_ref[...] = jnp.zeros(st_ref.shape, F32)

    n_chunks = q_ref.shape[0] // chunk
    n_levels = int(math.log2(chunk))
    lvl = lvl_ref[...]
    row = lax.broadcasted_iota(jnp.int32, (chunk, dk), 0)
    ng = ng_ref[...]

    sub = lax.broadcasted_iota(jnp.int32, (SUBLANES, dk), 0)

    for c in range(n_chunks):
        r0 = c * chunk
        rows = slice(r0, r0 + chunk)
        for hd in range(heads):
            ks = slice(hd * dk, (hd + 1) * dk)
            vs = slice(hd * dv, (hd + 1) * dv)
            qb = q_ref[rows, ks]
            kb = k_ref[rows, ks]
            vb = v_ref[rows, vs]
            q = qb.astype(F32)
            k = kb.astype(F32)
            cum = c_ref[rows, ks]

            def row_on_sublanes(i):
                return jnp.broadcast_to(c_ref[pl.ds(r0 + i, 1), ks], (SUBLANES, dk))

            scores = jnp.where(lvl == -1, _dot_nt(qb, kb), 0.0)
            for lv in range(n_levels):
                h = 1 << lv
                second = (row & h) != 0
                if h == 1:
                    boundary = jnp.where(second, pltpu.roll(cum, 1, 0), cum)
                else:
                    pieces = []
                    for v in range(chunk // SUBLANES):
                        first_row = v * SUBLANES
                        if 2 * h <= SUBLANES:
                            piece = row_on_sublanes(first_row + h - 1)
                            for b in range(1, SUBLANES // (2 * h)):
                                piece = jnp.where(sub >= 2 * h * b,
                                                  row_on_sublanes(first_row + 2 * h * b + h - 1), piece)
                        elif first_row % (2 * h) == 0:
                            piece = row_on_sublanes(first_row + h - 1)
                        pieces.append(piece)
                    boundary = jnp.concatenate(pieces, axis=0)
                decay = jnp.exp2(-jnp.abs(cum - boundary))
                w = (jnp.where(second, q, k) * decay).astype(BF16)
                scores = jnp.where(lvl == lv, _dot_nt(w, w), scores)
            last = row_on_sublanes(chunk - 1)
            st = st_ref[hd]
            o = _dot(scores.astype(BF16), vb)
            o = o + _dot_nt((q * jnp.exp2(cum)).astype(BF16), st.astype(BF16))
            kd = (k * jnp.exp2(jnp.concatenate([last] * (chunk // SUBLANES), axis=0) - cum)).astype(BF16)
            st_ref[hd] = st * jnp.exp2(last[0:1, :]) + _dot_tn(vb, kd)
            ms = jnp.mean(o * o, axis=-1, keepdims=True)
            o = o * lax.rsqrt(ms + EPS) * ng * gt_ref[rows, vs].astype(F32)
            y_ref[rows, vs] = o.astype(BF16)


def _gated_recurrence(q, k, g, v, gate, norm_g, bsz, heads):
    t, kw = q.shape
    vw = v.shape[1]
    dk, dv = kw // heads, vw // heads
    tt = RECUR_TILE
    per_seq = (t // bsz) // tt
    row = lambda b, i: (b * per_seq + i, 0)
    lvl = jnp.asarray(_level_index(RECUR_CHUNK))
    return pl.pallas_call(
        functools.partial(_recur_kernel, chunk=RECUR_CHUNK, heads=heads, dk=dk, dv=dv),
        grid=(bsz, per_seq),
        in_specs=[
            pl.BlockSpec(lvl.shape, lambda b, i: (0, 0)),
            pl.BlockSpec((tt, kw), row), pl.BlockSpec((tt, kw), row), pl.BlockSpec((tt, kw), row),
            pl.BlockSpec((tt, vw), row), pl.BlockSpec((tt, vw), row),
            pl.BlockSpec((1, dv), lambda b, i: (0, 0)),
        ],
        out_specs=pl.BlockSpec((tt, vw), row),
        out_shape=jax.ShapeDtypeStruct((t, vw), BF16),
        scratch_shapes=[pltpu.VMEM((heads, dv, dk), F32)],
        compiler_params=_params(("arbitrary", "arbitrary")),
        name="gated_recurrence",
    )(lvl, q, k, g, v, gate, norm_g)


def _gelu_tanh(x):
    return 0.5 * x * (1.0 + jnp.tanh(math.sqrt(2.0 / math.pi) * (x + 0.044715 * (x * x * x))))


def _ffn_tail(x, mod_ref, nf_ref, w1_ref, w3_ref, w2_ref):
    h = _norm_modulate(x, nf_ref[...], mod_ref[0, 3:4, :], mod_ref[0, 4:5, :]).astype(BF16)
    acc = None
    for j in range(w1_ref.shape[1] // FFN_TILE):
        cols = slice(j * FFN_TILE, (j + 1) * FFN_TILE)
        a = _dot(h, w1_ref[:, cols])
        b = _dot(h, w3_ref[:, cols])
        part = _dot((_silu(a) * b).astype(BF16), w2_ref[cols, :])
        acc = part if acc is None else acc + part
    return x + mod_ref[0, 5:6, :] * acc


def _block_even_kernel(x_ref, y2_ref, u_ref, yb_ref, mod_ref, sd_ref, wg_ref, bg_ref, wo_ref,
                       nf_ref, w1_ref, w3_ref, w2_ref, o_ref, tok_ref):
    for rows in _sub_tiles(x_ref.shape[0], BLOCK_SUB_TILES):
        _to_token_rows(y2_ref, tok_ref, rows)
        chunks = range(rows.start // S5_CHUNK, rows.stop // S5_CHUNK)
        y = jnp.concatenate(
            [jnp.concatenate([tok_ref[j, _tok_rows_of_chunk(n), :] for n in chunks], axis=0)
             for j in range(tok_ref.shape[0])], axis=1)
        y = y + sd_ref[...] * u_ref[rows, :].astype(F32)
        y = _gelu_tanh(y)
        ya = y * jax.nn.sigmoid(_dot(y.astype(BF16), wg_ref[...]) + bg_ref[...])
        sw = ya.shape[1]
        mixed = _dot(ya.astype(BF16), wo_ref[0:sw, :]) + _dot(yb_ref[rows, :], wo_ref[sw:, :])
        x = x_ref[rows, :] + mod_ref[0, 2:3, :] * mixed
        o_ref[rows, :] = _ffn_tail(x, mod_ref, nf_ref, w1_ref, w3_ref, w2_ref)


def _block_odd_kernel(x_ref, yc_ref, mod_ref, wo_ref, nf_ref, w1_ref, w3_ref, w2_ref, fg_ref,
                      o_ref, *, final_norm):
    for rows in _sub_tiles(x_ref.shape[0], BLOCK_SUB_TILES):
        x = x_ref[rows, :] + mod_ref[0, 2:3, :] * _dot(yc_ref[rows, :], wo_ref[...])
        x = _ffn_tail(x, mod_ref, nf_ref, w1_ref, w3_ref, w2_ref)
        if final_norm:
            ms = jnp.mean(x * x, axis=-1, keepdims=True)
            x = x * lax.rsqrt(ms + EPS) * fg_ref[...]
        o_ref[rows, :] = x


def _block_even(x2, y2, u, yb, mod, s5_d, w_glu, b_glu, w_out, nf, w1t, w3t, w2t, tiles_per_batch):
    t, d = x2.shape
    tb = TOKEN_TILE
    row = lambda i: (i, 0)
    sw = u.shape[1]
    groups, _, k = y2.shape
    return pl.pallas_call(
        _block_even_kernel,
        grid=(t // tb,),
        in_specs=[
            pl.BlockSpec((tb, d), row),
            pl.BlockSpec((groups, tb * GRANULE // k, k), lambda i: (0, i, 0)),
            pl.BlockSpec((tb, sw), row),
            pl.BlockSpec((tb, yb.shape[1]), row),
            pl.BlockSpec((1, N_MOD, d), lambda i: (i // tiles_per_batch, 0, 0)),
            _const_spec(s5_d.shape), _const_spec(w_glu.shape), _const_spec(b_glu.shape),
            _const_spec(w_out.shape), _const_spec(nf.shape),
            _const_spec(w1t.shape), _const_spec(w3t.shape), _const_spec(w2t.shape),
        ],
        out_specs=pl.BlockSpec((tb, d), row),
        out_shape=jax.ShapeDtypeStruct((t, d), F32),
        scratch_shapes=[pltpu.VMEM((sw // LANES, tb // S5_CHUNK * CHUNK_PITCH, LANES), F32)],
        compiler_params=_params(("arbitrary",)),
        name="block_even",
    )(x2, y2, u, yb, mod, s5_d, w_glu, b_glu, w_out, nf, w1t, w3t, w2t)


def _block_odd(x2, yc, mod, w_out, nf, w1t, w3t, w2t, final_g, final_norm, tiles_per_batch):
    t, d = x2.shape
    tb = TOKEN_TILE
    row = lambda i: (i, 0)
    return pl.pallas_call(
        functools.partial(_block_odd_kernel, final_norm=final_norm),
        grid=(t // tb,),
        in_specs=[
            pl.BlockSpec((tb, d), row), pl.BlockSpec((tb, yc.shape[1]), row),
            pl.BlockSpec((1, N_MOD, d), lambda i: (i // tiles_per_batch, 0, 0)),
            _const_spec(w_out.shape), _const_spec(nf.shape),
            _const_spec(w1t.shape), _const_spec(w3t.shape), _const_spec(w2t.shape),
            _const_spec(final_g.shape),
        ],
        out_specs=pl.BlockSpec((tb, d), row),
        out_shape=jax.ShapeDtypeStruct((t, d), F32),
        compiler_params=_params(("arbitrary",)),
        name="block_odd",
    )(x2, yc, mod, w_out, nf, w1t, w3t, w2t, final_g)


def kernel(x, c, ada_w, ada_b, norm_mix_g, norm_ffn_g, ev_w_in, ev_w_out, s5_lam_re, s5_lam_im, s5_b_re, s5_b_im, s5_c_re, s5_c_im, s5_d, s5_log_step, s5_w_glu, s5_b_glu, hg_lb_logits, hg_norm_g, od_w_in, od_w_a1, od_w_a2, od_b_a, gla_norm_g, od_w_out, ffn_w1, ffn_w3, ffn_w2, final_norm_g):
    bsz, seq, d = x.shape
    depth = ada_w.shape[0]
    assert depth % 2 == 0, "the final norm is fused into the last (odd) layer"
    t = bsz * seq
    tiles_per_batch = seq // TOKEN_TILE
    x2 = x.reshape(t, d)
    mod = _adaln(c, ada_w, ada_b).reshape(depth, bsz, N_MOD, d)

    for layer in range(depth):
        nm = norm_mix_g[layer].reshape(1, d)
        nf = norm_ffn_g[layer].reshape(1, d)
        ffn_f32 = (ffn_w1, ffn_w3, ffn_w2)
        if layer % 2 == 0:
            e = layer // 2
            casts = [(a, layer) for a in ffn_f32] + [(ev_w_out, e), (s5_w_glu, e)]
            if layer + 1 < depth:
                casts += [(od_w_in, e), (od_w_out, e)]
            (u, u2, hq, hk, hgl, hv, hgate, w1t, w3t, w2t, w_out_even, w_glu,
             *odd_bf16) = _inproj_even(x2, mod[layer], nm, ev_w_in[e].astype(BF16), hg_lb_logits,
                                       layer, casts, tiles_per_batch)
            sw = u.shape[1]
            y2 = _s5_mix(u2, s5_lam_re[e], s5_lam_im[e], s5_b_re[e], s5_b_im[e], s5_c_re[e],
                         s5_c_im[e], s5_log_step[e], seq // S5_CHUNK)
            yb = _gated_recurrence(hq, hk, hgl, hv, hgate, hg_norm_g[e].reshape(1, -1), bsz, HG_HEADS)
            x2 = _block_even(x2, y2, u, yb, mod[layer], s5_d[e].reshape(1, sw),
                             w_glu, s5_b_glu[e].reshape(1, sw),
                             w_out_even, nf, w1t, w3t, w2t, tiles_per_batch)
        else:
            o = layer // 2
            kw = od_w_a2.shape[-1]
            vw = od_w_out.shape[1]
            rank = od_w_a1.shape[-1]
            w_a1 = jnp.pad(od_w_a1[o], ((0, 0), (0, LANES - rank))).astype(BF16)
            w_a2 = jnp.pad(od_w_a2[o], ((0, LANES - rank), (0, 0))).astype(BF16)
            w_in_odd, w_out_odd = odd_bf16
            q, k, g, v, gate, w1t, w3t, w2t = _inproj_odd(
                x2, mod[layer], nm, w_in_odd, w_a1, w_a2, od_b_a[o].reshape(1, kw),
                kw, vw, (kw // GLA_HEADS) ** -0.5, layer, ffn_f32, tiles_per_batch)
            yc = _gated_recurrence(q, k, g, v, gate, gla_norm_g[o].reshape(1, -1), bsz, GLA_HEADS)
            x2 = _block_odd(x2, yc, mod[layer], w_out_odd, nf, w1t, w3t, w2t,
                            final_norm_g.reshape(1, d), layer == depth - 1, tiles_per_batch)
    return x2.reshape(bsz, seq, d)
```

```python
import functools
import math

import numpy as np
import jax
import jax.numpy as jnp
from jax import lax
from jax.experimental import pallas as pl
from jax.experimental.pallas import tpu as pltpu

F32 = jnp.float32
BF16 = jnp.bfloat16
EPS = 1e-6
LOG2E = math.log2(math.e)

HG_HEADS = 4
GLA_HEADS = 4
GLA_GATE_NORM = 16.0
N_MOD = 6

VMEM_LIMIT_BYTES = 56 * 1024 * 1024
LANES = 128
SUBLANES = 8
BF16_ROWS = 16
GRANULE = 16
GRANULES = LANES // GRANULE
RELAYOUT_ROWS = 32

TOKEN_TILE = 1024
RECUR_TILE = 1024
RECUR_CHUNK = 128
S5_CHUNK = 32
CHUNK_PITCH = 40
FFN_TILE = 256
BLOCK_SUB_TILES = 2


def _dot(a, b):
    return jnp.dot(a, b, preferred_element_type=F32)


def _dot_nt(a, b):
    return lax.dot_general(a, b, (((1,), (1,)), ((), ())), preferred_element_type=F32)


def _dot_tn(a, b):
    return lax.dot_general(a, b, (((0,), (0,)), ((), ())), preferred_element_type=F32)


def _silu(x):
    return x * jax.nn.sigmoid(x)


def _params(semantics):
    return pltpu.CompilerParams(dimension_semantics=semantics, vmem_limit_bytes=VMEM_LIMIT_BYTES)


def _const_spec(shape):
    nd = len(shape)
    return pl.BlockSpec(shape, lambda *_: (0,) * nd, pipeline_mode=pl.Buffered(1))


def _sub_tiles(n_rows, n_sub):
    step = n_rows // n_sub
    return [slice(i * step, (i + 1) * step) for i in range(n_sub)]


def _cast_specs(items, n_steps):
    in_specs, out_specs = [], []
    for w, layer in items:
        _, n_rows, n_cols = w.shape
        rows = next(r for r in range(BF16_ROWS, n_rows + 1, BF16_ROWS)
                    if n_rows % r == 0 and r * n_steps >= n_rows)
        last = n_rows // rows - 1
        in_specs.append(pl.BlockSpec((None, rows, n_cols),
                                     lambda i, last=last, layer=layer: (layer, jnp.minimum(i, last), 0)))
        out_specs.append(pl.BlockSpec((rows, n_cols), lambda i, last=last: (jnp.minimum(i, last), 0)))
    return in_specs, out_specs


def _cast_weights(f32_refs, bf16_refs):
    for src, dst in zip(f32_refs, bf16_refs):
        dst[...] = src[...].astype(BF16)


def _prefix_matrix(chunk):
    tri = np.tril(np.ones((chunk, chunk), np.float32))
    return np.concatenate([tri, tri, tri], axis=1)


def _chunk_cumsum(g, tri3):
    chunk = tri3.shape[0]
    hi = g.astype(BF16)
    r1 = g - hi.astype(F32)
    mid = r1.astype(BF16)
    lo = (r1 - mid.astype(F32)).astype(BF16)
    out = []
    for c in range(g.shape[0] // chunk):
        rows = slice(c * chunk, (c + 1) * chunk)
        out.append(_dot(tri3, jnp.concatenate([hi[rows], mid[rows], lo[rows]], axis=0)))
    return jnp.concatenate(out, axis=0)


def _norm_modulate(x, norm_g, shift, scale):
    ms = jnp.mean(x * x, axis=-1, keepdims=True)
    h = x * lax.rsqrt(ms + EPS) * norm_g
    return h * (1.0 + scale) + shift


def _adaln_kernel(ct_ref, w_ref, b_ref, o_ref):
    cond = _silu(ct_ref[...])
    w = w_ref[0]
    for b in range(cond.shape[1]):
        o_ref[0, b:b + 1, :] = jnp.sum(cond[:, b:b + 1] * w, axis=0, keepdims=True) + b_ref[0]


def _adaln(c, ada_w, ada_b):
    depth, d, n = ada_w.shape
    bsz = c.shape[0]
    nt = 1024
    return pl.pallas_call(
        _adaln_kernel,
        grid=(depth, n // nt),
        in_specs=[
            pl.BlockSpec((d, bsz), lambda l, j: (0, 0)),
            pl.BlockSpec((1, d, nt), lambda l, j: (l, 0, j)),
            pl.BlockSpec((1, 1, nt), lambda l, j: (l, 0, j)),
        ],
        out_specs=pl.BlockSpec((1, bsz, nt), lambda l, j: (l, 0, j)),
        out_shape=jax.ShapeDtypeStruct((depth, bsz, n), F32),
        compiler_params=_params(("arbitrary", "arbitrary")),
        name="adaln_mod",
    )(c.T, ada_w, ada_b.reshape(depth, 1, n))


def _granule_transpose(blocks, slot):
    blocks = list(blocks)
    for d in (4, 2, 1):
        upper = (slot & d) != 0
        for a in range(GRANULES):
            if a & d:
                continue
            lo, hi = blocks[a], blocks[a + d]
            blocks[a] = jnp.where(upper, pltpu.roll(hi, GRANULE * d, 1), lo)
            blocks[a + d] = jnp.where(upper, hi, pltpu.roll(lo, LANES - GRANULE * d, 1))
    return blocks


def _tok_rows_of_chunk(n):
    return slice(n * CHUNK_PITCH, n * CHUNK_PITCH + S5_CHUNK)


def _to_chunk_rows(tok_ref, out_ref, tok_rows):
    n_grp, _, k = out_ref.shape
    chunk = k // GRANULE
    n_rows = min(RELAYOUT_ROWS, (tok_rows.stop - tok_rows.start) // chunk)
    slot = lax.broadcasted_iota(jnp.int32, (n_rows, LANES), 1) // GRANULE
    for r0 in range(tok_rows.start // chunk, tok_rows.stop // chunk, n_rows):
        for tb in range(chunk // GRANULES):
            for j in range(n_grp // GRANULES):
                rows = [tok_ref[j, pl.ds(r0 * CHUNK_PITCH + GRANULES * tb + tp, n_rows,
                                         stride=CHUNK_PITCH), :]
                        for tp in range(GRANULES)]
                for gp, blk in enumerate(_granule_transpose(rows, slot)):
                    out_ref[GRANULES * j + gp, r0:r0 + n_rows, tb * LANES:(tb + 1) * LANES] = (
                        blk.astype(out_ref.dtype))


def _to_token_rows(grp_ref, tok_ref, tok_rows):
    n_grp, _, k = grp_ref.shape
    chunk = k // GRANULE
    n_rows = min(RELAYOUT_ROWS, (tok_rows.stop - tok_rows.start) // chunk)
    slot = lax.broadcasted_iota(jnp.int32, (n_rows, LANES), 1) // GRANULE
    for r0 in range(tok_rows.start // chunk, tok_rows.stop // chunk, n_rows):
        for tb in range(chunk // GRANULES):
            for j in range(n_grp // GRANULES):
                cols = [grp_ref[GRANULES * j + gp, r0:r0 + n_rows, tb * LANES:(tb + 1) * LANES].astype(F32)
                        for gp in range(GRANULES)]
                for tp, blk in enumerate(_granule_transpose(cols, slot)):
                    tok_ref[j, pl.ds(r0 * CHUNK_PITCH + GRANULES * tb + tp, n_rows,
                                     stride=CHUNK_PITCH), :] = blk


def _inproj_even_kernel(x_ref, mod_ref, ng_ref, w_ref, lbl_ref, tri_ref, *refs, lb_row, n_cast):
    f32_refs, refs = refs[:n_cast], refs[n_cast:]
    u_ref, u2_ref, q_ref, k_ref, g_ref, v_ref, gt_ref = refs[:7]
    bf16_refs, tok_ref = refs[7:7 + n_cast], refs[7 + n_cast]
    _cast_weights(f32_refs, bf16_refs)
    w = u_ref.shape[1]
    lg = lbl_ref[...]
    e = jnp.exp(lg - jnp.max(lg, axis=0, keepdims=True))
    lb = jnp.sum(e[:lb_row + 1], axis=0, keepdims=True) / jnp.sum(e, axis=0, keepdims=True)
    h = _norm_modulate(x_ref[...], ng_ref[...], mod_ref[0, 0:1, :], mod_ref[0, 1:2, :])
    hb = h.astype(BF16)
    part = lambda p: _dot(hb, w_ref[:, p * w:(p + 1) * w])
    rows = slice(0, x_ref.shape[0])
    f = part(2)
    g_ref[...] = _chunk_cumsum(jnp.log(lb + (1.0 - lb) * jax.nn.sigmoid(f)) * LOG2E, tri_ref[...])
    k_ref[...] = ((1.0 - lb) * jax.nn.sigmoid(-f)).astype(BF16)
    z = part(0)
    for j in range(tok_ref.shape[0]):
        for n in range(x_ref.shape[0] // S5_CHUNK):
            tok_ref[j, _tok_rows_of_chunk(n), :] = z[n * S5_CHUNK:(n + 1) * S5_CHUNK,
                                                     j * LANES:(j + 1) * LANES]
    _to_chunk_rows(tok_ref, u2_ref, rows)
    u_ref[...] = z.astype(BF16)
    q_ref[...] = _silu(part(1)).astype(BF16)
    v_ref[...] = part(3).astype(BF16)
    gt_ref[...] = _silu(part(4)).astype(BF16)


def _inproj_even(x2, mod, norm_g, w_in, lb_logits, lb_row, casts, tiles_per_batch):
    t, d = x2.shape
    w = w_in.shape[1] // 5
    tb = TOKEN_TILE
    row = lambda i: (i, 0)
    out_bf = jax.ShapeDtypeStruct((t, w), BF16)
    groups = w // GRANULE
    k = S5_CHUNK * GRANULE
    tok = pl.BlockSpec((tb, w), row)
    tri3 = jnp.asarray(_prefix_matrix(RECUR_CHUNK), BF16)
    cast_in, cast_out = _cast_specs(casts, t // tb)
    return pl.pallas_call(
        functools.partial(_inproj_even_kernel, lb_row=lb_row, n_cast=len(casts)),
        grid=(t // tb,),
        in_specs=[
            pl.BlockSpec((tb, d), row),
            pl.BlockSpec((1, N_MOD, d), lambda i: (i // tiles_per_batch, 0, 0)),
            _const_spec((1, d)),
            _const_spec(w_in.shape),
            _const_spec(lb_logits.shape),
            _const_spec(tri3.shape),
        ] + cast_in,
        out_specs=[tok, pl.BlockSpec((groups, tb // S5_CHUNK, k), lambda i: (0, i, 0)),
                   tok, tok, tok, tok, tok] + cast_out,
        out_shape=[out_bf, jax.ShapeDtypeStruct((groups, t // S5_CHUNK, k), BF16),
                   out_bf, out_bf, jax.ShapeDtypeStruct((t, w), F32), out_bf, out_bf]
                  + [jax.ShapeDtypeStruct(a.shape[1:], BF16) for a, _ in casts],
        scratch_shapes=[pltpu.VMEM((w // LANES, tb // S5_CHUNK * CHUNK_PITCH, LANES), F32)],
        compiler_params=_params(("arbitrary",)),
        name="inproj_even",
    )(x2, mod, norm_g, w_in, lb_logits, tri3, *[a for a, _ in casts])


def _inproj_odd_kernel(x_ref, mod_ref, ng_ref, w_ref, wa1_ref, wa2_ref, ba_ref, tri_ref,
                       f1_ref, f3_ref, f2_ref,
                       q_ref, k_ref, g_ref, v_ref, gt_ref, b1_ref, b3_ref, b2_ref, *, q_scale):
    _cast_weights((f1_ref, f3_ref, f2_ref), (b1_ref, b3_ref, b2_ref))
    kw = q_ref.shape[1]
    vw = v_ref.shape[1]
    h = _norm_modulate(x_ref[...], ng_ref[...], mod_ref[0, 0:1, :], mod_ref[0, 1:2, :])
    hb = h.astype(BF16)
    a1 = _dot(hb, wa1_ref[...]).astype(BF16)
    za = _dot(a1, wa2_ref[...]) + ba_ref[...]
    log_sig = jnp.minimum(za, 0.0) - jnp.log(1.0 + jnp.exp(-jnp.abs(za)))
    g_ref[...] = _chunk_cumsum(log_sig * (LOG2E / GLA_GATE_NORM), tri_ref[...])
    q_ref[...] = (_dot(hb, w_ref[:, 0:kw]) * q_scale).astype(BF16)
    k_ref[...] = _dot(hb, w_ref[:, kw:2 * kw]).astype(BF16)
    v_ref[...] = _dot(hb, w_ref[:, 2 * kw:2 * kw + vw]).astype(BF16)
    gt_ref[...] = _silu(_dot(hb, w_ref[:, 2 * kw + vw:2 * kw + 2 * vw])).astype(BF16)


def _inproj_odd(x2, mod, norm_g, w_in, w_a1, w_a2, b_a, kw, vw, q_scale, layer, ffn_f32,
                tiles_per_batch):
    t, d = x2.shape
    tb = TOKEN_TILE
    row = lambda i: (i, 0)
    tri3 = jnp.asarray(_prefix_matrix(RECUR_CHUNK), BF16)
    cast_in, cast_out = _cast_specs([(a, layer) for a in ffn_f32], t // tb)
    return pl.pallas_call(
        functools.partial(_inproj_odd_kernel, q_scale=q_scale),
        grid=(t // tb,),
        in_specs=[
            pl.BlockSpec((tb, d), row),
            pl.BlockSpec((1, N_MOD, d), lambda i: (i // tiles_per_batch, 0, 0)),
            _const_spec((1, d)),
            _const_spec(w_in.shape),
            _const_spec(w_a1.shape),
            _const_spec(w_a2.shape),
            _const_spec(b_a.shape),
            _const_spec(tri3.shape),
        ] + cast_in,
        out_specs=[pl.BlockSpec((tb, kw), row), pl.BlockSpec((tb, kw), row), pl.BlockSpec((tb, kw), row),
                   pl.BlockSpec((tb, vw), row), pl.BlockSpec((tb, vw), row)] + cast_out,
        out_shape=[jax.ShapeDtypeStruct((t, kw), BF16), jax.ShapeDtypeStruct((t, kw), BF16),
                   jax.ShapeDtypeStruct((t, kw), F32), jax.ShapeDtypeStruct((t, vw), BF16),
                   jax.ShapeDtypeStruct((t, vw), BF16)]
                  + [jax.ShapeDtypeStruct(a.shape[1:], BF16) for a in ffn_f32],
        compiler_params=_params(("arbitrary",)),
        name="inproj_odd",
    )(x2, mod, norm_g, w_in, w_a1, w_a2, b_a, tri3, *ffn_f32)


def _s5_kernel(u_ref, mt_ref, wsr_ref, wsi_ref, wir_ref, wii_ref, sr_ref, si_ref, y_ref,
               *, chunks_per_seq):
    u = u_ref[0]
    v_re = _dot(u, wsr_ref[0])
    v_im = _dot(u, wsi_ref[0])
    n_in_seq = lax.broadcasted_iota(jnp.int32, v_re.shape, 0) & (chunks_per_seq - 1)
    for lv in range(sr_ref.shape[1]):
        step = 1 << lv
        keep = n_in_seq >= step
        s_re = jnp.where(keep, pltpu.roll(v_re, step, 0), 0.0)
        s_im = jnp.where(keep, pltpu.roll(v_im, step, 0), 0.0)
        a_re, a_im = sr_ref[0, lv:lv + 1, :], si_ref[0, lv:lv + 1, :]
        v_re, v_im = v_re + a_re * s_re - a_im * s_im, v_im + a_re * s_im + a_im * s_re
    keep = n_in_seq >= 1
    x_re = jnp.where(keep, pltpu.roll(v_re, 1, 0), 0.0).astype(BF16)
    x_im = jnp.where(keep, pltpu.roll(v_im, 1, 0), 0.0).astype(BF16)
    y = _dot(u, mt_ref[0]) + _dot_nt(x_re, wir_ref[0]) + _dot_nt(x_im, wii_ref[0])
    y_ref[0] = y.astype(BF16)


def _s5_prep_kernel(bt_re_ref, bt_im_ref, c_re_ref, c_im_ref, lp_re_ref, lp_im_ref,
                    mt_ref, wsr_ref, wsi_ref, wir_ref, wii_ref, toep_ref, *, chunk):
    hg, p = bt_re_ref.shape[1:]
    k = chunk * hg
    hp = lax.Precision.HIGHEST

    tau = lax.broadcasted_iota(jnp.int32, (chunk, p), 0)
    pw_re = jnp.ones((chunk, p), F32)
    pw_im = jnp.zeros((chunk, p), F32)
    for b in range(lp_re_ref.shape[1]):
        l_re, l_im = lp_re_ref[0, b:b + 1, :], lp_im_ref[0, b:b + 1, :]
        bit = ((tau >> b) & 1) == 1
        pw_re, pw_im = (jnp.where(bit, pw_re * l_re - pw_im * l_im, pw_re),
                        jnp.where(bit, pw_re * l_im + pw_im * l_re, pw_im))

    def rows_of_powers(exponent):
        pick = (exponent == lax.broadcasted_iota(jnp.int32, (k, chunk), 1)).astype(F32)
        return (jnp.dot(pick, pw_re, precision=hp, preferred_element_type=F32),
                jnp.dot(pick, pw_im, precision=hp, preferred_element_type=F32))

    t = lax.broadcasted_iota(jnp.int32, (k, chunk), 0) // hg
    tile = lambda x: jnp.concatenate([x] * chunk, axis=0)
    b_re, b_im = tile(bt_re_ref[0]), tile(bt_im_ref[0])
    c_re, c_im = tile(c_re_ref[0]), tile(c_im_ref[0])

    p_re, p_im = rows_of_powers(t)
    z_re, z_im = c_re * p_re - c_im * p_im, c_re * p_im + c_im * p_re
    nt = (((1,), (1,)), ((), ()))
    kt = (lax.dot_general(bt_re_ref[0], z_re, nt, precision=hp, preferred_element_type=F32)
          - lax.dot_general(bt_im_ref[0], z_im, nt, precision=hp, preferred_element_type=F32))
    col_t = lax.broadcasted_iota(jnp.int32, (chunk, k), 1) // hg
    row_s = lax.broadcasted_iota(jnp.int32, (chunk, k), 0)
    for hi in range(hg):
        rows = jnp.broadcast_to(kt[hi:hi + 1, :], (chunk, k))
        rows = pltpu.roll(rows, 0, 1, stride=hg, stride_axis=0)
        rows = jnp.where(col_t >= row_s, rows, 0.0)
        for j in range(k // LANES):
            toep_ref[j, pl.ds(hi, chunk, stride=hg), :] = rows[:, j * LANES:(j + 1) * LANES]
    for j in range(k // LANES):
        mt_ref[0, :, j * LANES:(j + 1) * LANES] = toep_ref[j].astype(BF16)
    p_re, p_im = rows_of_powers(chunk - 1 - t)
    wsr_ref[0] = (b_re * p_re - b_im * p_im).astype(BF16)
    wsi_ref[0] = (b_re * p_im + b_im * p_re).astype(BF16)
    l_re, l_im = lp_re_ref[0, 0:1, :], lp_im_ref[0, 0:1, :]
    wir_ref[0] = (z_re * l_re - z_im * l_im).astype(BF16)
    wii_ref[0] = (-(z_re * l_im + z_im * l_re)).astype(BF16)


def _s5_group_kernel(u_ref, bt_re_ref, bt_im_ref, c_re_ref, c_im_ref, lp_re_ref, lp_im_ref,
                     sr_ref, si_ref, y_ref, mt_ref, wsr_ref, wsi_ref, wir_ref, wii_ref, toep_ref,
                     *, chunk, chunks_per_seq):
    _s5_prep_kernel(bt_re_ref, bt_im_ref, c_re_ref, c_im_ref, lp_re_ref, lp_im_ref,
                    mt_ref, wsr_ref, wsi_ref, wir_ref, wii_ref, toep_ref, chunk=chunk)
    _s5_kernel(u_ref, mt_ref, wsr_ref, wsi_ref, wir_ref, wii_ref, sr_ref, si_ref, y_ref,
               chunks_per_seq=chunks_per_seq)


def _s5_discretise(lam_re, lam_im, b_re, b_im, log_step, chunk, chunks_per_seq):
    delta = jnp.exp(log_step)[:, None]
    ld_re, ld_im = lam_re * delta, lam_im * delta

    def powers(exponents):
        e = jnp.asarray(exponents, F32)[None, :, None]
        mag = jnp.exp(ld_re[:, None, :] * e)
        ang = ld_im[:, None, :] * e
        return mag * jnp.cos(ang), mag * jnp.sin(ang)

    lp_re, lp_im = powers([1 << b for b in range(chunk.bit_length())])
    nr, ni = lp_re[:, 0] - 1.0, lp_im[:, 0]
    den = lam_re * lam_re + lam_im * lam_im
    s_re = (nr * lam_re + ni * lam_im) / den
    s_im = (ni * lam_re - nr * lam_im) / den
    bt_re = (s_re[:, :, None] * b_re - s_im[:, :, None] * b_im).transpose(0, 2, 1)
    bt_im = (s_re[:, :, None] * b_im + s_im[:, :, None] * b_re).transpose(0, 2, 1)
    n_lev = int(math.log2(chunks_per_seq))
    sc_re, sc_im = powers([chunk * (1 << lv) for lv in range(n_lev)])
    return bt_re, bt_im, lp_re, lp_im, sc_re, sc_im


def _s5_mix(u2, lam_re, lam_im, b_re, b_im, c_re, c_im, log_step, chunks_per_seq):
    g, r, k = u2.shape
    chunk = k // GRANULE
    p = lam_re.shape[1]
    bt_re, bt_im, lp_re, lp_im, sc_re, sc_im = _s5_discretise(
        lam_re, lam_im, b_re, b_im, log_step, chunk, chunks_per_seq)
    grp = lambda i: (i, 0, 0)
    small = lambda a: pl.BlockSpec((1,) + a.shape[1:], grp)
    params = (bt_re, bt_im, c_re, c_im, lp_re, lp_im, sc_re, sc_im)
    return pl.pallas_call(
        functools.partial(_s5_group_kernel, chunk=chunk, chunks_per_seq=chunks_per_seq),
        grid=(g,),
        in_specs=[pl.BlockSpec((1, r, k), grp)] + [small(a) for a in params],
        out_specs=pl.BlockSpec((1, r, k), grp),
        out_shape=jax.ShapeDtypeStruct((g, r, k), BF16),
        scratch_shapes=[pltpu.VMEM((1, k, k), BF16)] + [pltpu.VMEM((1, k, p), BF16)] * 4
                       + [pltpu.VMEM((k // LANES, k, LANES), F32)],
        compiler_params=_params(("arbitrary",)),
        name="s5_mix",
    )(u2, *params)


def _level_index(chunk):
    i = np.arange(chunk)[:, None]
    j = np.arange(chunk)[None, :]
    x = np.bitwise_xor(i, j)
    lvl = np.floor(np.log2(np.maximum(x, 1))).astype(np.int32)
    return np.where(i > j, lvl, np.where(i == j, -1, -2)).astype(np.int32)


def _recur_kernel(lvl_ref, q_ref, k_ref, c_ref, v_ref, gt_ref, ng_ref, y_ref, st_ref,
                  *, chunk, heads, dk, dv):
    @pl.when(pl.program_id(1) == 0)
    def _():
        st_ref[...] = jnp.zeros(st_ref.shape, F32)

    n_chunks = q_ref.shape[0] // chunk
    n_levels = int(math.log2(chunk))
    lvl = lvl_ref[...]
    row = lax.broadcasted_iota(jnp.int32, (chunk, dk), 0)
    ng = ng_ref[...]

    sub = lax.broadcasted_iota(jnp.int32, (SUBLANES, dk), 0)

    for c in range(n_chunks):
        r0 = c * chunk
        rows = slice(r0, r0 + chunk)
        for hd in range(heads):
            ks = slice(hd * dk, (hd + 1) * dk)
            vs = slice(hd * dv, (hd + 1) * dv)
            qb = q_ref[rows, ks]
            kb = k_ref[rows, ks]
            vb = v_ref[rows, vs]
            q = qb.astype(F32)
            k = kb.astype(F32)
            cum = c_ref[rows, ks]

            def row_on_sublanes(i):
                return jnp.broadcast_to(c_ref[pl.ds(r0 + i, 1), ks], (SUBLANES, dk))

            scores = jnp.where(lvl == -1, _dot_nt(qb, kb), 0.0)
            for lv in range(n_levels):
                h = 1 << lv
                second = (row & h) != 0
                if h == 1:
                    boundary = jnp.where(second, pltpu.roll(cum, 1, 0), cum)
                else:
                    pieces = []
                    for v in range(chunk // SUBLANES):
                        first_row = v * SUBLANES
                        if 2 * h <= SUBLANES:
                            piece = row_on_sublanes(first_row + h - 1)
                            for b in range(1, SUBLANES // (2 * h)):
                                piece = jnp.where(sub >= 2 * h * b,
                                                  row_on_sublanes(first_row + 2 * h * b + h - 1), piece)
                        elif first_row % (2 * h) == 0:
                            piece = row_on_sublanes(first_row + h - 1)
                        pieces.append(piece)
                    boundary = jnp.concatenate(pieces, axis=0)
                decay = jnp.exp2(-jnp.abs(cum - boundary))
                w = (jnp.where(second, q, k) * decay).astype(BF16)
                scores = jnp.where(lvl == lv, _dot_nt(w, w), scores)
            last = row_on_sublanes(chunk - 1)
            st = st_ref[hd]
            o = _dot(scores.astype(BF16), vb)
            o = o + _dot_nt((q * jnp.exp2(cum)).astype(BF16), st.astype(BF16))
            kd = (k * jnp.exp2(jnp.concatenate([last] * (chunk // SUBLANES), axis=0) - cum)).astype(BF16)
            st_ref[hd] = st * jnp.exp2(last[0:1, :]) + _dot_tn(vb, kd)
            ms = jnp.mean(o * o, axis=-1, keepdims=True)
            o = o * lax.rsqrt(ms + EPS) * ng * gt_ref[rows, vs].astype(F32)
            y_ref[rows, vs] = o.astype(BF16)


def _gated_recurrence(q, k, g, v, gate, norm_g, bsz, heads):
    t, kw = q.shape
    vw = v.shape[1]
    dk, dv = kw // heads, vw // heads
    tt = RECUR_TILE
    per_seq = (t // bsz) // tt
    row = lambda b, i: (b * per_seq + i, 0)
    lvl = jnp.asarray(_level_index(RECUR_CHUNK))
    return pl.pallas_call(
        functools.partial(_recur_kernel, chunk=RECUR_CHUNK, heads=heads, dk=dk, dv=dv),
        grid=(bsz, per_seq),
        in_specs=[
            pl.BlockSpec(lvl.shape, lambda b, i: (0, 0)),
            pl.BlockSpec((tt, kw), row), pl.BlockSpec((tt, kw), row), pl.BlockSpec((tt, kw), row),
            pl.BlockSpec((tt, vw), row), pl.BlockSpec((tt, vw), row),
            pl.BlockSpec((1, dv), lambda b, i: (0, 0)),
        ],
        out_specs=pl.BlockSpec((tt, vw), row),
        out_shape=jax.ShapeDtypeStruct((t, vw), BF16),
        scratch_shapes=[pltpu.VMEM((heads, dv, dk), F32)],
        compiler_params=_params(("arbitrary", "arbitrary")),
        name="gated_recurrence",
    )(lvl, q, k, g, v, gate, norm_g)


def _gelu_tanh(x):
    return 0.5 * x * (1.0 + jnp.tanh(math.sqrt(2.0 / math.pi) * (x + 0.044715 * (x * x * x))))


def _ffn_tail(x, mod_ref, nf_ref, w1_ref, w3_ref, w2_ref):
    h = _norm_modulate(x, nf_ref[...], mod_ref[0, 3:4, :], mod_ref[0, 4:5, :]).astype(BF16)
    acc = None
    for j in range(w1_ref.shape[1] // FFN_TILE):
        cols = slice(j * FFN_TILE, (j + 1) * FFN_TILE)
        a = _dot(h, w1_ref[:, cols])
        b = _dot(h, w3_ref[:, cols])
        part = _dot((_silu(a) * b).astype(BF16), w2_ref[cols, :])
        acc = part if acc is None else acc + part
    return x + mod_ref[0, 5:6, :] * acc


def _block_even_kernel(x_ref, y2_ref, u_ref, yb_ref, mod_ref, sd_ref, wg_ref, bg_ref, wo_ref,
                       nf_ref, w1_ref, w3_ref, w2_ref, o_ref, tok_ref):
    xs = []
    for rows in _sub_tiles(x_ref.shape[0], BLOCK_SUB_TILES):
        _to_token_rows(y2_ref, tok_ref, rows)
        chunks = range(rows.start // S5_CHUNK, rows.stop // S5_CHUNK)
        y = jnp.concatenate(
            [jnp.concatenate([tok_ref[j, _tok_rows_of_chunk(n), :] for n in chunks], axis=0)
             for j in range(tok_ref.shape[0])], axis=1)
        y = y + sd_ref[...] * u_ref[rows, :].astype(F32)
        y = _gelu_tanh(y)
        ya = y * jax.nn.sigmoid(_dot(y.astype(BF16), wg_ref[...]) + bg_ref[...])
        sw = ya.shape[1]
        mixed = _dot(ya.astype(BF16), wo_ref[0:sw, :]) + _dot(yb_ref[rows, :], wo_ref[sw:, :])
        xs.append(x_ref[rows, :] + mod_ref[0, 2:3, :] * mixed)
    for rows, x in zip(_sub_tiles(x_ref.shape[0], BLOCK_SUB_TILES), xs):
        o_ref[rows, :] = _ffn_tail(x, mod_ref, nf_ref, w1_ref, w3_ref, w2_ref)


def _block_odd_kernel(x_ref, yc_ref, mod_ref, wo_ref, nf_ref, w1_ref, w3_ref, w2_ref, fg_ref,
                      o_ref, *, final_norm):
    for rows in _sub_tiles(x_ref.shape[0], BLOCK_SUB_TILES):
        x = x_ref[rows, :] + mod_ref[0, 2:3, :] * _dot(yc_ref[rows, :], wo_ref[...])
        x = _ffn_tail(x, mod_ref, nf_ref, w1_ref, w3_ref, w2_ref)
        if final_norm:
            ms = jnp.mean(x * x, axis=-1, keepdims=True)
            x = x * lax.rsqrt(ms + EPS) * fg_ref[...]
        o_ref[rows, :] = x


def _block_even(x2, y2, u, yb, mod, s5_d, w_glu, b_glu, w_out, nf, w1t, w3t, w2t, tiles_per_batch):
    t, d = x2.shape
    tb = TOKEN_TILE
    row = lambda i: (i, 0)
    sw = u.shape[1]
    groups, _, k = y2.shape
    return pl.pallas_call(
        _block_even_kernel,
        grid=(t // tb,),
        in_specs=[
            pl.BlockSpec((tb, d), row),
            pl.BlockSpec((groups, tb * GRANULE // k, k), lambda i: (0, i, 0)),
            pl.BlockSpec((tb, sw), row),
            pl.BlockSpec((tb, yb.shape[1]), row),
            pl.BlockSpec((1, N_MOD, d), lambda i: (i // tiles_per_batch, 0, 0)),
            _const_spec(s5_d.shape), _const_spec(w_glu.shape), _const_spec(b_glu.shape),
            _const_spec(w_out.shape), _const_spec(nf.shape),
            _const_spec(w1t.shape), _const_spec(w3t.shape), _const_spec(w2t.shape),
        ],
        out_specs=pl.BlockSpec((tb, d), row),
        out_shape=jax.ShapeDtypeStruct((t, d), F32),
        scratch_shapes=[pltpu.VMEM((sw // LANES, tb // S5_CHUNK * CHUNK_PITCH, LANES), F32)],
        compiler_params=_params(("arbitrary",)),
        name="block_even",
    )(x2, y2, u, yb, mod, s5_d, w_glu, b_glu, w_out, nf, w1t, w3t, w2t)


def _block_odd(x2, yc, mod, w_out, nf, w1t, w3t, w2t, final_g, final_norm, tiles_per_batch):
    t, d = x2.shape
    tb = TOKEN_TILE
    row = lambda i: (i, 0)
    return pl.pallas_call(
        functools.partial(_block_odd_kernel, final_norm=final_norm),
        grid=(t // tb,),
        in_specs=[
            pl.BlockSpec((tb, d), row), pl.BlockSpec((tb, yc.shape[1]), row),
            pl.BlockSpec((1, N_MOD, d), lambda i: (i // tiles_per_batch, 0, 0)),
            _const_spec(w_out.shape), _const_spec(nf.shape),
            _const_spec(w1t.shape), _const_spec(w3t.shape), _const_spec(w2t.shape),
            _const_spec(final_g.shape),
        ],
        out_specs=pl.BlockSpec((tb, d), row),
        out_shape=jax.ShapeDtypeStruct((t, d), F32),
        compiler_params=_params(("arbitrary",)),
        name="block_odd",
    )(x2, yc, mod, w_out, nf, w1t, w3t, w2t, final_g)


def kernel(x, c, ada_w, ada_b, norm_mix_g, norm_ffn_g, ev_w_in, ev_w_out, s5_lam_re, s5_lam_im, s5_b_re, s5_b_im, s5_c_re, s5_c_im, s5_d, s5_log_step, s5_w_glu, s5_b_glu, hg_lb_logits, hg_norm_g, od_w_in, od_w_a1, od_w_a2, od_b_a, gla_norm_g, od_w_out, ffn_w1, ffn_w3, ffn_w2, final_norm_g):
    bsz, seq, d = x.shape
    depth = ada_w.shape[0]
    assert depth % 2 == 0, "the final norm is fused into the last (odd) layer"
    t = bsz * seq
    tiles_per_batch = seq // TOKEN_TILE
    x2 = x.reshape(t, d)
    mod = _adaln(c, ada_w, ada_b).reshape(depth, bsz, N_MOD, d)

    for layer in range(depth):
        nm = norm_mix_g[layer].reshape(1, d)
        nf = norm_ffn_g[layer].reshape(1, d)
        ffn_f32 = (ffn_w1, ffn_w3, ffn_w2)
        if layer % 2 == 0:
            e = layer // 2
            casts = [(a, layer) for a in ffn_f32] + [(ev_w_out, e), (s5_w_glu, e)]
            if layer + 1 < depth:
                casts += [(od_w_in, e), (od_w_out, e)]
            (u, u2, hq, hk, hgl, hv, hgate, w1t, w3t, w2t, w_out_even, w_glu,
             *odd_bf16) = _inproj_even(x2, mod[layer], nm, ev_w_in[e].astype(BF16), hg_lb_logits,
                                       layer, casts, tiles_per_batch)
            sw = u.shape[1]
            y2 = _s5_mix(u2, s5_lam_re[e], s5_lam_im[e], s5_b_re[e], s5_b_im[e], s5_c_re[e],
                         s5_c_im[e], s5_log_step[e], seq // S5_CHUNK)
            yb = _gated_recurrence(hq, hk, hgl, hv, hgate, hg_norm_g[e].reshape(1, -1), bsz, HG_HEADS)
            x2 = _block_even(x2, y2, u, yb, mod[layer], s5_d[e].reshape(1, sw),
                             w_glu, s5_b_glu[e].reshape(1, sw),
                             w_out_even, nf, w1t, w3t, w2t, tiles_per_batch)
        else:
            o = layer // 2
            kw = od_w_a2.shape[-1]
            vw = od_w_out.shape[1]
            rank = od_w_a1.shape[-1]
            w_a1 = jnp.pad(od_w_a1[o], ((0, 0), (0, LANES - rank))).astype(BF16)
            w_a2 = jnp.pad(od_w_a2[o], ((0, LANES - rank), (0, 0))).astype(BF16)
            w_in_odd, w_out_odd = odd_bf16
            q, k, g, v, gate, w1t, w3t, w2t = _inproj_odd(
                x2, mod[layer], nm, w_in_odd, w_a1, w_a2, od_b_a[o].reshape(1, kw),
                kw, vw, (kw // GLA_HEADS) ** -0.5, layer, ffn_f32, tiles_per_batch)
            yc = _gated_recurrence(q, k, g, v, gate, gla_norm_g[o].reshape(1, -1), bsz, GLA_HEADS)
            x2 = _block_odd(x2, yc, mod[layer], w_out_odd, nf, w1t, w3t, w2t,
                            final_norm_g.reshape(1, d), layer == depth - 1, tiles_per_batch)
    return x2.reshape(bsz, seq, d)
```

```python
import functools
import math

import numpy as np
import jax
import jax.numpy as jnp
from jax import lax
from jax.experimental import pallas as pl
from jax.experimental.pallas import tpu as pltpu

F32 = jnp.float32
BF16 = jnp.bfloat16
EPS = 1e-6
LOG2E = math.log2(math.e)

HG_HEADS = 4
GLA_HEADS = 4
GLA_GATE_NORM = 16.0
N_MOD = 6

VMEM_LIMIT_BYTES = 56 * 1024 * 1024
LANES = 128
SUBLANES = 8
BF16_ROWS = 16
GRANULE = 16
GRANULES = LANES // GRANULE
RELAYOUT_ROWS = 32

TOKEN_TILE = 1024
RECUR_TILE = 1024
RECUR_CHUNK = 128
S5_CHUNK = 32
CHUNK_PITCH = 40
FFN_TILE = 256
BLOCK_SUB_TILES = 2


def _dot(a, b):
    return jnp.dot(a, b, preferred_element_type=F32)


def _dot_nt(a, b):
    return lax.dot_general(a, b, (((1,), (1,)), ((), ())), preferred_element_type=F32)


def _dot_tn(a, b):
    return lax.dot_general(a, b, (((0,), (0,)), ((), ())), preferred_element_type=F32)


def _silu(x):
    return x * jax.nn.sigmoid(x)


def _params(semantics):
    return pltpu.CompilerParams(dimension_semantics=semantics, vmem_limit_bytes=VMEM_LIMIT_BYTES)


def _const_spec(shape):
    nd = len(shape)
    return pl.BlockSpec(shape, lambda *_: (0,) * nd, pipeline_mode=pl.Buffered(1))


def _sub_tiles(n_rows, n_sub):
    step = n_rows // n_sub
    return [slice(i * step, (i + 1) * step) for i in range(n_sub)]


def _cast_specs(items, n_steps):
    in_specs, out_specs = [], []
    for w, layer in items:
        _, n_rows, n_cols = w.shape
        rows = next(r for r in range(BF16_ROWS, n_rows + 1, BF16_ROWS)
                    if n_rows % r == 0 and r * n_steps >= n_rows)
        last = n_rows // rows - 1
        in_specs.append(pl.BlockSpec((None, rows, n_cols),
                                     lambda i, last=last, layer=layer: (layer, jnp.minimum(i, last), 0)))
        out_specs.append(pl.BlockSpec((rows, n_cols), lambda i, last=last: (jnp.minimum(i, last), 0)))
    return in_specs, out_specs


def _cast_weights(f32_refs, bf16_refs):
    for src, dst in zip(f32_refs, bf16_refs):
        dst[...] = src[...].astype(BF16)


def _prefix_matrix(chunk):
    tri = np.tril(np.ones((chunk, chunk), np.float32))
    return np.concatenate([tri, tri, tri], axis=1)


def _chunk_cumsum(g, tri3):
    chunk = tri3.shape[0]
    hi = g.astype(BF16)
    r1 = g - hi.astype(F32)
    mid = r1.astype(BF16)
    lo = (r1 - mid.astype(F32)).astype(BF16)
    out = []
    for c in range(g.shape[0] // chunk):
        rows = slice(c * chunk, (c + 1) * chunk)
        out.append(_dot(tri3, jnp.concatenate([hi[rows], mid[rows], lo[rows]], axis=0)))
    return jnp.concatenate(out, axis=0)


def _norm_modulate(x, norm_g, shift, scale):
    ms = jnp.mean(x * x, axis=-1, keepdims=True)
    h = x * lax.rsqrt(ms + EPS) * norm_g
    return h * (1.0 + scale) + shift


def _adaln_kernel(ct_ref, w_ref, b_ref, o_ref):
    cond = _silu(ct_ref[...])
    w = w_ref[0]
    for b in range(cond.shape[1]):
        o_ref[0, b:b + 1, :] = jnp.sum(cond[:, b:b + 1] * w, axis=0, keepdims=True) + b_ref[0]


def _adaln(c, ada_w, ada_b):
    depth, d, n = ada_w.shape
    bsz = c.shape[0]
    nt = 1024
    return pl.pallas_call(
        _adaln_kernel,
        grid=(depth, n // nt),
        in_specs=[
            pl.BlockSpec((d, bsz), lambda l, j: (0, 0)),
            pl.BlockSpec((1, d, nt), lambda l, j: (l, 0, j)),
            pl.BlockSpec((1, 1, nt), lambda l, j: (l, 0, j)),
        ],
        out_specs=pl.BlockSpec((1, bsz, nt), lambda l, j: (l, 0, j)),
        out_shape=jax.ShapeDtypeStruct((depth, bsz, n), F32),
        compiler_params=_params(("arbitrary", "arbitrary")),
        name="adaln_mod",
    )(c.T, ada_w, ada_b.reshape(depth, 1, n))


def _granule_transpose(blocks, slot):
    blocks = list(blocks)
    for d in (4, 2, 1):
        upper = (slot & d) != 0
        for a in range(GRANULES):
            if a & d:
                continue
            lo, hi = blocks[a], blocks[a + d]
            blocks[a] = jnp.where(upper, pltpu.roll(hi, GRANULE * d, 1), lo)
            blocks[a + d] = jnp.where(upper, hi, pltpu.roll(lo, LANES - GRANULE * d, 1))
    return blocks


def _tok_rows_of_chunk(n):
    return slice(n * CHUNK_PITCH, n * CHUNK_PITCH + S5_CHUNK)


def _to_chunk_rows(tok_ref, out_ref, tok_rows):
    n_grp, _, k = out_ref.shape
    chunk = k // GRANULE
    n_rows = min(RELAYOUT_ROWS, (tok_rows.stop - tok_rows.start) // chunk)
    slot = lax.broadcasted_iota(jnp.int32, (n_rows, LANES), 1) // GRANULE
    for r0 in range(tok_rows.start // chunk, tok_rows.stop // chunk, n_rows):
        for tb in range(chunk // GRANULES):
            for j in range(n_grp // GRANULES):
                rows = [tok_ref[j, pl.ds(r0 * CHUNK_PITCH + GRANULES * tb + tp, n_rows,
                                         stride=CHUNK_PITCH), :]
                        for tp in range(GRANULES)]
                for gp, blk in enumerate(_granule_transpose(rows, slot)):
                    out_ref[GRANULES * j + gp, r0:r0 + n_rows, tb * LANES:(tb + 1) * LANES] = (
                        blk.astype(out_ref.dtype))


def _to_token_rows(grp_ref, tok_ref, tok_rows):
    n_grp, _, k = grp_ref.shape
    chunk = k // GRANULE
    n_rows = min(RELAYOUT_ROWS, (tok_rows.stop - tok_rows.start) // chunk)
    slot = lax.broadcasted_iota(jnp.int32, (n_rows, LANES), 1) // GRANULE
    for r0 in range(tok_rows.start // chunk, tok_rows.stop // chunk, n_rows):
        for tb in range(chunk // GRANULES):
            for j in range(n_grp // GRANULES):
                cols = [grp_ref[GRANULES * j + gp, r0:r0 + n_rows, tb * LANES:(tb + 1) * LANES].astype(F32)
                        for gp in range(GRANULES)]
                for tp, blk in enumerate(_granule_transpose(cols, slot)):
                    tok_ref[j, pl.ds(r0 * CHUNK_PITCH + GRANULES * tb + tp, n_rows,
                                     stride=CHUNK_PITCH), :] = blk


def _inproj_even_kernel(x_ref, mod_ref, ng_ref, w_ref, lbl_ref, tri_ref, *refs, lb_row, n_cast):
    f32_refs, refs = refs[:n_cast], refs[n_cast:]
    u_ref, u2_ref, q_ref, k_ref, g_ref, v_ref, gt_ref = refs[:7]
    bf16_refs, tok_ref = refs[7:7 + n_cast], refs[7 + n_cast]
    _cast_weights(f32_refs, bf16_refs)
    w = u_ref.shape[1]
    lg = lbl_ref[...]
    e = jnp.exp(lg - jnp.max(lg, axis=0, keepdims=True))
    lb = jnp.sum(e[:lb_row + 1], axis=0, keepdims=True) / jnp.sum(e, axis=0, keepdims=True)
    h = _norm_modulate(x_ref[...], ng_ref[...], mod_ref[0, 0:1, :], mod_ref[0, 1:2, :])
    hb = h.astype(BF16)
    part = lambda p: _dot(hb, w_ref[:, p * w:(p + 1) * w])
    rows = slice(0, x_ref.shape[0])
    f = part(2)
    g_ref[...] = _chunk_cumsum(jnp.log(lb + (1.0 - lb) * jax.nn.sigmoid(f)) * LOG2E, tri_ref[...])
    k_ref[...] = ((1.0 - lb) * jax.nn.sigmoid(-f)).astype(BF16)
    z = part(0)
    for j in range(tok_ref.shape[0]):
        for n in range(x_ref.shape[0] // S5_CHUNK):
            tok_ref[j, _tok_rows_of_chunk(n), :] = z[n * S5_CHUNK:(n + 1) * S5_CHUNK,
                                                     j * LANES:(j + 1) * LANES]
    _to_chunk_rows(tok_ref, u2_ref, rows)
    u_ref[...] = z.astype(BF16)
    q_ref[...] = _silu(part(1)).astype(BF16)
    v_ref[...] = part(3).astype(BF16)
    gt_ref[...] = _silu(part(4)).astype(BF16)


def _inproj_even(x2, mod, norm_g, w_in, lb_logits, lb_row, casts, tiles_per_batch):
    t, d = x2.shape
    w = w_in.shape[1] // 5
    tb = TOKEN_TILE
    row = lambda i: (i, 0)
    out_bf = jax.ShapeDtypeStruct((t, w), BF16)
    groups = w // GRANULE
    k = S5_CHUNK * GRANULE
    tok = pl.BlockSpec((tb, w), row)
    tri3 = jnp.asarray(_prefix_matrix(RECUR_CHUNK), BF16)
    cast_in, cast_out = _cast_specs(casts, t // tb)
    return pl.pallas_call(
        functools.partial(_inproj_even_kernel, lb_row=lb_row, n_cast=len(casts)),
        grid=(t // tb,),
        in_specs=[
            pl.BlockSpec((tb, d), row),
            pl.BlockSpec((1, N_MOD, d), lambda i: (i // tiles_per_batch, 0, 0)),
            _const_spec((1, d)),
            _const_spec(w_in.shape),
            _const_spec(lb_logits.shape),
            _const_spec(tri3.shape),
        ] + cast_in,
        out_specs=[tok, pl.BlockSpec((groups, tb // S5_CHUNK, k), lambda i: (0, i, 0)),
                   tok, tok, tok, tok, tok] + cast_out,
        out_shape=[out_bf, jax.ShapeDtypeStruct((groups, t // S5_CHUNK, k), BF16),
                   out_bf, out_bf, jax.ShapeDtypeStruct((t, w), F32), out_bf, out_bf]
                  + [jax.ShapeDtypeStruct(a.shape[1:], BF16) for a, _ in casts],
        scratch_shapes=[pltpu.VMEM((w // LANES, tb // S5_CHUNK * CHUNK_PITCH, LANES), F32)],
        compiler_params=_params(("arbitrary",)),
        name="inproj_even",
    )(x2, mod, norm_g, w_in, lb_logits, tri3, *[a for a, _ in casts])


def _inproj_odd_kernel(x_ref, mod_ref, ng_ref, w_ref, wa1_ref, wa2_ref, ba_ref, tri_ref,
                       f1_ref, f3_ref, f2_ref,
                       q_ref, k_ref, g_ref, v_ref, gt_ref, b1_ref, b3_ref, b2_ref, *, q_scale):
    _cast_weights((f1_ref, f3_ref, f2_ref), (b1_ref, b3_ref, b2_ref))
    kw = q_ref.shape[1]
    vw = v_ref.shape[1]
    h = _norm_modulate(x_ref[...], ng_ref[...], mod_ref[0, 0:1, :], mod_ref[0, 1:2, :])
    hb = h.astype(BF16)
    a1 = _dot(hb, wa1_ref[...]).astype(BF16)
    za = _dot(a1, wa2_ref[...]) + ba_ref[...]
    log_sig = jnp.minimum(za, 0.0) - jnp.log(1.0 + jnp.exp(-jnp.abs(za)))
    g_ref[...] = _chunk_cumsum(log_sig * (LOG2E / GLA_GATE_NORM), tri_ref[...])
    q_ref[...] = (_dot(hb, w_ref[:, 0:kw]) * q_scale).astype(BF16)
    k_ref[...] = _dot(hb, w_ref[:, kw:2 * kw]).astype(BF16)
    v_ref[...] = _dot(hb, w_ref[:, 2 * kw:2 * kw + vw]).astype(BF16)
    gt_ref[...] = _silu(_dot(hb, w_ref[:, 2 * kw + vw:2 * kw + 2 * vw])).astype(BF16)


def _inproj_odd(x2, mod, norm_g, w_in, w_a1, w_a2, b_a, kw, vw, q_scale, layer, ffn_f32,
                tiles_per_batch):
    t, d = x2.shape
    tb = TOKEN_TILE
    row = lambda i: (i, 0)
    tri3 = jnp.asarray(_prefix_matrix(RECUR_CHUNK), BF16)
    cast_in, cast_out = _cast_specs([(a, layer) for a in ffn_f32], t // tb)
    return pl.pallas_call(
        functools.partial(_inproj_odd_kernel, q_scale=q_scale),
        grid=(t // tb,),
        in_specs=[
            pl.BlockSpec((tb, d), row),
            pl.BlockSpec((1, N_MOD, d), lambda i: (i // tiles_per_batch, 0, 0)),
            _const_spec((1, d)),
            _const_spec(w_in.shape),
            _const_spec(w_a1.shape),
            _const_spec(w_a2.shape),
            _const_spec(b_a.shape),
            _const_spec(tri3.shape),
        ] + cast_in,
        out_specs=[pl.BlockSpec((tb, kw), row), pl.BlockSpec((tb, kw), row), pl.BlockSpec((tb, kw), row),
                   pl.BlockSpec((tb, vw), row), pl.BlockSpec((tb, vw), row)] + cast_out,
        out_shape=[jax.ShapeDtypeStruct((t, kw), BF16), jax.ShapeDtypeStruct((t, kw), BF16),
                   jax.ShapeDtypeStruct((t, kw), F32), jax.ShapeDtypeStruct((t, vw), BF16),
                   jax.ShapeDtypeStruct((t, vw), BF16)]
                  + [jax.ShapeDtypeStruct(a.shape[1:], BF16) for a in ffn_f32],
        compiler_params=_params(("arbitrary",)),
        name="inproj_odd",
    )(x2, mod, norm_g, w_in, w_a1, w_a2, b_a, tri3, *ffn_f32)


def _s5_kernel(u_ref, mt_ref, wsr_ref, wsi_ref, wir_ref, wii_ref, sr_ref, si_ref, y_ref,
               *, chunks_per_seq):
    u = u_ref[0]
    v_re = _dot(u, wsr_ref[0])
    v_im = _dot(u, wsi_ref[0])
    n_in_seq = lax.broadcasted_iota(jnp.int32, v_re.shape, 0) & (chunks_per_seq - 1)
    for lv in range(sr_ref.shape[1]):
        step = 1 << lv
        keep = n_in_seq >= step
        s_re = jnp.where(keep, pltpu.roll(v_re, step, 0), 0.0)
        s_im = jnp.where(keep, pltpu.roll(v_im, step, 0), 0.0)
        a_re, a_im = sr_ref[0, lv:lv + 1, :], si_ref[0, lv:lv + 1, :]
        v_re, v_im = v_re + a_re * s_re - a_im * s_im, v_im + a_re * s_im + a_im * s_re
    keep = n_in_seq >= 1
    x_re = jnp.where(keep, pltpu.roll(v_re, 1, 0), 0.0).astype(BF16)
    x_im = jnp.where(keep, pltpu.roll(v_im, 1, 0), 0.0).astype(BF16)
    y = _dot(u, mt_ref[0]) + _dot_nt(x_re, wir_ref[0]) + _dot_nt(x_im, wii_ref[0])
    y_ref[0] = y.astype(BF16)


def _s5_prep_kernel(bt_re_ref, bt_im_ref, c_re_ref, c_im_ref, lp_re_ref, lp_im_ref,
                    mt_ref, wsr_ref, wsi_ref, wir_ref, wii_ref, toep_ref, *, chunk):
    hg, p = bt_re_ref.shape[1:]
    k = chunk * hg
    hp = lax.Precision.HIGHEST

    tau = lax.broadcasted_iota(jnp.int32, (chunk, p), 0)
    pw_re = jnp.ones((chunk, p), F32)
    pw_im = jnp.zeros((chunk, p), F32)
    for b in range(lp_re_ref.shape[1]):
        l_re, l_im = lp_re_ref[0, b:b + 1, :], lp_im_ref[0, b:b + 1, :]
        bit = ((tau >> b) & 1) == 1
        pw_re, pw_im = (jnp.where(bit, pw_re * l_re - pw_im * l_im, pw_re),
                        jnp.where(bit, pw_re * l_im + pw_im * l_re, pw_im))

    def rows_of_powers(exponent):
        pick = (exponent == lax.broadcasted_iota(jnp.int32, (k, chunk), 1)).astype(F32)
        return (jnp.dot(pick, pw_re, precision=hp, preferred_element_type=F32),
                jnp.dot(pick, pw_im, precision=hp, preferred_element_type=F32))

    t = lax.broadcasted_iota(jnp.int32, (k, chunk), 0) // hg
    tile = lambda x: jnp.concatenate([x] * chunk, axis=0)
    b_re, b_im = tile(bt_re_ref[0]), tile(bt_im_ref[0])
    c_re, c_im = tile(c_re_ref[0]), tile(c_im_ref[0])

    p_re, p_im = rows_of_powers(t)
    z_re, z_im = c_re * p_re - c_im * p_im, c_re * p_im + c_im * p_re
    nt = (((1,), (1,)), ((), ()))
    kt = (lax.dot_general(bt_re_ref[0], z_re, nt, precision=hp, preferred_element_type=F32)
          - lax.dot_general(bt_im_ref[0], z_im, nt, precision=hp, preferred_element_type=F32))
    col_t = lax.broadcasted_iota(jnp.int32, (chunk, k), 1) // hg
    row_s = lax.broadcasted_iota(jnp.int32, (chunk, k), 0)
    for hi in range(hg):
        rows = jnp.broadcast_to(kt[hi:hi + 1, :], (chunk, k))
        rows = pltpu.roll(rows, 0, 1, stride=hg, stride_axis=0)
        rows = jnp.where(col_t >= row_s, rows, 0.0)
        for j in range(k // LANES):
            toep_ref[j, pl.ds(hi, chunk, stride=hg), :] = rows[:, j * LANES:(j + 1) * LANES]
    for j in range(k // LANES):
        mt_ref[0, :, j * LANES:(j + 1) * LANES] = toep_ref[j].astype(BF16)
    p_re, p_im = rows_of_powers(chunk - 1 - t)
    wsr_ref[0] = (b_re * p_re - b_im * p_im).astype(BF16)
    wsi_ref[0] = (b_re * p_im + b_im * p_re).astype(BF16)
    l_re, l_im = lp_re_ref[0, 0:1, :], lp_im_ref[0, 0:1, :]
    wir_ref[0] = (z_re * l_re - z_im * l_im).astype(BF16)
    wii_ref[0] = (-(z_re * l_im + z_im * l_re)).astype(BF16)


def _s5_group_kernel(u_ref, bt_re_ref, bt_im_ref, c_re_ref, c_im_ref, lp_re_ref, lp_im_ref,
                     sr_ref, si_ref, y_ref, mt_ref, wsr_ref, wsi_ref, wir_ref, wii_ref, toep_ref,
                     *, chunk, chunks_per_seq):
    _s5_prep_kernel(bt_re_ref, bt_im_ref, c_re_ref, c_im_ref, lp_re_ref, lp_im_ref,
                    mt_ref, wsr_ref, wsi_ref, wir_ref, wii_ref, toep_ref, chunk=chunk)
    _s5_kernel(u_ref, mt_ref, wsr_ref, wsi_ref, wir_ref, wii_ref, sr_ref, si_ref, y_ref,
               chunks_per_seq=chunks_per_seq)


def _s5_discretise(lam_re, lam_im, b_re, b_im, log_step, chunk, chunks_per_seq):
    delta = jnp.exp(log_step)[:, None]
    ld_re, ld_im = lam_re * delta, lam_im * delta

    def powers(exponents):
        e = jnp.asarray(exponents, F32)[None, :, None]
        mag = jnp.exp(ld_re[:, None, :] * e)
        ang = ld_im[:, None, :] * e
        return mag * jnp.cos(ang), mag * jnp.sin(ang)

    lp_re, lp_im = powers([1 << b for b in range(chunk.bit_length())])
    nr, ni = lp_re[:, 0] - 1.0, lp_im[:, 0]
    den = lam_re * lam_re + lam_im * lam_im
    s_re = (nr * lam_re + ni * lam_im) / den
    s_im = (ni * lam_re - nr * lam_im) / den
    bt_re = (s_re[:, :, None] * b_re - s_im[:, :, None] * b_im).transpose(0, 2, 1)
    bt_im = (s_re[:, :, None] * b_im + s_im[:, :, None] * b_re).transpose(0, 2, 1)
    n_lev = int(math.log2(chunks_per_seq))
    sc_re, sc_im = powers([chunk * (1 << lv) for lv in range(n_lev)])
    return bt_re, bt_im, lp_re, lp_im, sc_re, sc_im


def _s5_mix(u2, lam_re, lam_im, b_re, b_im, c_re, c_im, log_step, chunks_per_seq):
    g, r, k = u2.shape
    chunk = k // GRANULE
    p = lam_re.shape[1]
    bt_re, bt_im, lp_re, lp_im, sc_re, sc_im = _s5_discretise(
        lam_re, lam_im, b_re, b_im, log_step, chunk, chunks_per_seq)
    grp = lambda i: (i, 0, 0)
    small = lambda a: pl.BlockSpec((1,) + a.shape[1:], grp)
    params = (bt_re, bt_im, c_re, c_im, lp_re, lp_im, sc_re, sc_im)
    return pl.pallas_call(
        functools.partial(_s5_group_kernel, chunk=chunk, chunks_per_seq=chunks_per_seq),
        grid=(g,),
        in_specs=[pl.BlockSpec((1, r, k), grp)] + [small(a) for a in params],
        out_specs=pl.BlockSpec((1, r, k), grp),
        out_shape=jax.ShapeDtypeStruct((g, r, k), BF16),
        scratch_shapes=[pltpu.VMEM((1, k, k), BF16)] + [pltpu.VMEM((1, k, p), BF16)] * 4
                       + [pltpu.VMEM((k // LANES, k, LANES), F32)],
        compiler_params=_params(("arbitrary",)),
        name="s5_mix",
    )(u2, *params)


def _level_index(chunk):
    i = np.arange(chunk)[:, None]
    j = np.arange(chunk)[None, :]
    x = np.bitwise_xor(i, j)
    lvl = np.floor(np.log2(np.maximum(x, 1))).astype(np.int32)
    return np.where(i > j, lvl, np.where(i == j, -1, -2)).astype(np.int32)


def _recur_kernel(lvl_ref, q_ref, k_ref, c_ref, v_ref, gt_ref, ng_ref, y_ref, st_ref,
                  *, chunk, heads, dk, dv):
    @pl.when(pl.program_id(1) == 0)
    def _():
        st_ref[...] = jnp.zeros(st_ref.shape, F32)

    n_chunks = q_ref.shape[0] // chunk
    n_levels = int(math.log2(chunk))
    lvl = lvl_ref[...]
    row = lax.broadcasted_iota(jnp.int32, (chunk, dk), 0)
    ng = ng_ref[...]

    sub = lax.broadcasted_iota(jnp.int32, (SUBLANES, dk), 0)

    for c in range(n_chunks):
        r0 = c * chunk
        rows = slice(r0, r0 + chunk)
        for hd in range(heads):
            ks = slice(hd * dk, (hd + 1) * dk)
            vs = slice(hd * dv, (hd + 1) * dv)
            qb = q_ref[rows, ks]
            kb = k_ref[rows, ks]
            vb = v_ref[rows, vs]
            q = qb.astype(F32)
            k = kb.astype(F32)
            cum = c_ref[rows, ks]

            def row_on_sublanes(i):
                return jnp.broadcast_to(c_ref[pl.ds(r0 + i, 1), ks], (SUBLANES, dk))

            scores = jnp.where(lvl == -1, _dot_nt(qb, kb), 0.0)
            for lv in range(n_levels):
                h = 1 << lv
                second = (row & h) != 0
                if h == 1:
                    boundary = jnp.where(second, pltpu.roll(cum, 1, 0), cum)
                else:
                    pieces = []
                    for v in range(chunk // SUBLANES):
                        first_row = v * SUBLANES
                        if 2 * h <= SUBLANES:
                            piece = row_on_sublanes(first_row + h - 1)
                            for b in range(1, SUBLANES // (2 * h)):
                                piece = jnp.where(sub >= 2 * h * b,
                                                  row_on_sublanes(first_row + 2 * h * b + h - 1), piece)
                        elif first_row % (2 * h) == 0:
                            piece = row_on_sublanes(first_row + h - 1)
                        pieces.append(piece)
                    boundary = jnp.concatenate(pieces, axis=0)
                decay = jnp.exp2(-jnp.abs(cum - boundary))
                w = (jnp.where(second, q, k) * decay).astype(BF16)
                scores = jnp.where(lvl == lv, _dot_nt(w, w), scores)
            last = row_on_sublanes(chunk - 1)
            st = st_ref[hd]
            o = _dot(scores.astype(BF16), vb)
            o = o + _dot_nt((q * jnp.exp2(cum)).astype(BF16), st.astype(BF16))
            kd = (k * jnp.exp2(jnp.concatenate([last] * (chunk // SUBLANES), axis=0) - cum)).astype(BF16)
            st_ref[hd] = st * jnp.exp2(last[0:1, :]) + _dot_tn(vb, kd)
            ms = jnp.mean(o * o, axis=-1, keepdims=True)
            o = o * lax.rsqrt(ms + EPS) * ng * gt_ref[rows, vs].astype(F32)
            y_ref[rows, vs] = o.astype(BF16)


def _gated_recurrence(q, k, g, v, gate, norm_g, bsz, heads):
    t, kw = q.shape
    vw = v.shape[1]
    dk, dv = kw // heads, vw // heads
    tt = RECUR_TILE
    per_seq = (t // bsz) // tt
    row = lambda b, i: (b * per_seq + i, 0)
    lvl = jnp.asarray(_level_index(RECUR_CHUNK))
    return pl.pallas_call(
        functools.partial(_recur_kernel, chunk=RECUR_CHUNK, heads=heads, dk=dk, dv=dv),
        grid=(bsz, per_seq),
        in_specs=[
            pl.BlockSpec(lvl.shape, lambda b, i: (0, 0)),
            pl.BlockSpec((tt, kw), row), pl.BlockSpec((tt, kw), row), pl.BlockSpec((tt, kw), row),
            pl.BlockSpec((tt, vw), row), pl.BlockSpec((tt, vw), row),
            pl.BlockSpec((1, dv), lambda b, i: (0, 0)),
        ],
        out_specs=pl.BlockSpec((tt, vw), row),
        out_shape=jax.ShapeDtypeStruct((t, vw), BF16),
        scratch_shapes=[pltpu.VMEM((heads, dv, dk), F32)],
        compiler_params=_params(("arbitrary", "arbitrary")),
        name="gated_recurrence",
    )(lvl, q, k, g, v, gate, norm_g)


def _gelu_tanh(x):
    return 0.5 * x * (1.0 + jnp.tanh(math.sqrt(2.0 / math.pi) * (x + 0.044715 * (x * x * x))))


def _ffn_tail(x, mod_ref, nf_ref, w1_ref, w3_ref, w2_ref):
    h = _norm_modulate(x, nf_ref[...], mod_ref[0, 3:4, :], mod_ref[0, 4:5, :]).astype(BF16)
    acc = None
    for j in range(w1_ref.shape[1] // FFN_TILE):
        cols = slice(j * FFN_TILE, (j + 1) * FFN_TILE)
        a = _dot(h, w1_ref[:, cols])
        b = _dot(h, w3_ref[:, cols])
        part = _dot((_silu(a) * b).astype(BF16), w2_ref[cols, :])
        acc = part if acc is None else acc + part
    return x + mod_ref[0, 5:6, :] * acc


def _block_even_kernel(x_ref, y2_ref, u_ref, yb_ref, mod_ref, sd_ref, wg_ref, bg_ref, wo_ref,
                       nf_ref, w1_ref, w3_ref, w2_ref, o_ref, tok_ref):
    xs = []
    for rows in _sub_tiles(x_ref.shape[0], BLOCK_SUB_TILES):
        _to_token_rows(y2_ref, tok_ref, rows)
        chunks = range(rows.start // S5_CHUNK, rows.stop // S5_CHUNK)
        y = jnp.concatenate(
            [jnp.concatenate([tok_ref[j, _tok_rows_of_chunk(n), :] for n in chunks], axis=0)
             for j in range(tok_ref.shape[0])], axis=1)
        y = y + sd_ref[...] * u_ref[rows, :].astype(F32)
        y = _gelu_tanh(y)
        ya = y * jax.nn.sigmoid(_dot(y.astype(BF16), wg_ref[...]) + bg_ref[...])
        sw = ya.shape[1]
        mixed = _dot(ya.astype(BF16), wo_ref[0:sw, :]) + _dot(yb_ref[rows, :], wo_ref[sw:, :])
        xs.append(x_ref[rows, :] + mod_ref[0, 2:3, :] * mixed)
    for rows, x in zip(_sub_tiles(x_ref.shape[0], BLOCK_SUB_TILES), xs):
        o_ref[rows, :] = _ffn_tail(x, mod_ref, nf_ref, w1_ref, w3_ref, w2_ref)


def _block_odd_kernel(x_ref, yc_ref, mod_ref, wo_ref, nf_ref, w1_ref, w3_ref, w2_ref, fg_ref,
                      o_ref, *, final_norm):
    tiles = _sub_tiles(x_ref.shape[0], BLOCK_SUB_TILES)
    xs = [x_ref[rows, :] + mod_ref[0, 2:3, :] * _dot(yc_ref[rows, :], wo_ref[...]) for rows in tiles]
    for rows, x in zip(tiles, xs):
        x = _ffn_tail(x, mod_ref, nf_ref, w1_ref, w3_ref, w2_ref)
        if final_norm:
            ms = jnp.mean(x * x, axis=-1, keepdims=True)
            x = x * lax.rsqrt(ms + EPS) * fg_ref[...]
        o_ref[rows, :] = x


def _block_even(x2, y2, u, yb, mod, s5_d, w_glu, b_glu, w_out, nf, w1t, w3t, w2t, tiles_per_batch):
    t, d = x2.shape
    tb = TOKEN_TILE
    row = lambda i: (i, 0)
    sw = u.shape[1]
    groups, _, k = y2.shape
    return pl.pallas_call(
        _block_even_kernel,
        grid=(t // tb,),
        in_specs=[
            pl.BlockSpec((tb, d), row),
            pl.BlockSpec((groups, tb * GRANULE // k, k), lambda i: (0, i, 0)),
            pl.BlockSpec((tb, sw), row),
            pl.BlockSpec((tb, yb.shape[1]), row),
            pl.BlockSpec((1, N_MOD, d), lambda i: (i // tiles_per_batch, 0, 0)),
            _const_spec(s5_d.shape), _const_spec(w_glu.shape), _const_spec(b_glu.shape),
            _const_spec(w_out.shape), _const_spec(nf.shape),
            _const_spec(w1t.shape), _const_spec(w3t.shape), _const_spec(w2t.shape),
        ],
        out_specs=pl.BlockSpec((tb, d), row),
        out_shape=jax.ShapeDtypeStruct((t, d), F32),
        scratch_shapes=[pltpu.VMEM((sw // LANES, tb // S5_CHUNK * CHUNK_PITCH, LANES), F32)],
        compiler_params=_params(("arbitrary",)),
        name="block_even",
    )(x2, y2, u, yb, mod, s5_d, w_glu, b_glu, w_out, nf, w1t, w3t, w2t)


def _block_odd(x2, yc, mod, w_out, nf, w1t, w3t, w2t, final_g, final_norm, tiles_per_batch):
    t, d = x2.shape
    tb = TOKEN_TILE
    row = lambda i: (i, 0)
    return pl.pallas_call(
        functools.partial(_block_odd_kernel, final_norm=final_norm),
        grid=(t // tb,),
        in_specs=[
            pl.BlockSpec((tb, d), row), pl.BlockSpec((tb, yc.shape[1]), row),
            pl.BlockSpec((1, N_MOD, d), lambda i: (i // tiles_per_batch, 0, 0)),
            _const_spec(w_out.shape), _const_spec(nf.shape),
            _const_spec(w1t.shape), _const_spec(w3t.shape), _const_spec(w2t.shape),
            _const_spec(final_g.shape),
        ],
        out_specs=pl.BlockSpec((tb, d), row),
        out_shape=jax.ShapeDtypeStruct((t, d), F32),
        compiler_params=_params(("arbitrary",)),
        name="block_odd",
    )(x2, yc, mod, w_out, nf, w1t, w3t, w2t, final_g)


def kernel(x, c, ada_w, ada_b, norm_mix_g, norm_ffn_g, ev_w_in, ev_w_out, s5_lam_re, s5_lam_im, s5_b_re, s5_b_im, s5_c_re, s5_c_im, s5_d, s5_log_step, s5_w_glu, s5_b_glu, hg_lb_logits, hg_norm_g, od_w_in, od_w_a1, od_w_a2, od_b_a, gla_norm_g, od_w_out, ffn_w1, ffn_w3, ffn_w2, final_norm_g):
    bsz, seq, d = x.shape
    depth = ada_w.shape[0]
    assert depth % 2 == 0, "the final norm is fused into the last (odd) layer"
    t = bsz * seq
    tiles_per_batch = seq // TOKEN_TILE
    x2 = x.reshape(t, d)
    mod = _adaln(c, ada_w, ada_b).reshape(depth, bsz, N_MOD, d)

    for layer in range(depth):
        nm = norm_mix_g[layer].reshape(1, d)
        nf = norm_ffn_g[layer].reshape(1, d)
        ffn_f32 = (ffn_w1, ffn_w3, ffn_w2)
        if layer % 2 == 0:
            e = layer // 2
            casts = [(a, layer) for a in ffn_f32] + [(ev_w_out, e), (s5_w_glu, e)]
            if layer + 1 < depth:
                casts += [(od_w_in, e), (od_w_out, e)]
            (u, u2, hq, hk, hgl, hv, hgate, w1t, w3t, w2t, w_out_even, w_glu,
             *odd_bf16) = _inproj_even(x2, mod[layer], nm, ev_w_in[e].astype(BF16), hg_lb_logits,
                                       layer, casts, tiles_per_batch)
            sw = u.shape[1]
            y2 = _s5_mix(u2, s5_lam_re[e], s5_lam_im[e], s5_b_re[e], s5_b_im[e], s5_c_re[e],
                         s5_c_im[e], s5_log_step[e], seq // S5_CHUNK)
            yb = _gated_recurrence(hq, hk, hgl, hv, hgate, hg_norm_g[e].reshape(1, -1), bsz, HG_HEADS)
            x2 = _block_even(x2, y2, u, yb, mod[layer], s5_d[e].reshape(1, sw),
                             w_glu, s5_b_glu[e].reshape(1, sw),
                             w_out_even, nf, w1t, w3t, w2t, tiles_per_batch)
        else:
            o = layer // 2
            kw = od_w_a2.shape[-1]
            vw = od_w_out.shape[1]
            rank = od_w_a1.shape[-1]
            w_a1 = jnp.pad(od_w_a1[o], ((0, 0), (0, LANES - rank))).astype(BF16)
            w_a2 = jnp.pad(od_w_a2[o], ((0, LANES - rank), (0, 0))).astype(BF16)
            w_in_odd, w_out_odd = odd_bf16
            q, k, g, v, gate, w1t, w3t, w2t = _inproj_odd(
                x2, mod[layer], nm, w_in_odd, w_a1, w_a2, od_b_a[o].reshape(1, kw),
                kw, vw, (kw // GLA_HEADS) ** -0.5, layer, ffn_f32, tiles_per_batch)
            yc = _gated_recurrence(q, k, g, v, gate, gla_norm_g[o].reshape(1, -1), bsz, GLA_HEADS)
            x2 = _block_odd(x2, yc, mod[layer], w_out_odd, nf, w1t, w3t, w2t,
                            final_norm_g.reshape(1, d), layer == depth - 1, tiles_per_batch)
    return x2.reshape(bsz, seq, d)
```
